```python
import math
import jax
import jax.numpy as jnp
from jax import lax
import numpy as np

D_MODEL = 1024
BATCH = 16
SEQ = 2048
DEPTH = 4

D_FF = 2816
RMS_EPS = 1e-6

HY_WIDTH = D_MODEL // 2
HY_ORDER = 2
HY_SHORT_CONV = 3
HY_EMB = 33
HY_BANDS = (HY_EMB - 1) // 2
HY_FILTER_HIDDEN = 64
HY_DECAY_TARGET = 1e-2
HY_FAST_DECAY = 0.3
HY_SLOW_DECAY = 1.5
HY_MOD_SHIFT = 0.05
HY_NORM_EPS = 1e-6

ATTN_GROUPS = ((128, 1), (512, 4), (2048, 16))
N_GROUPS = len(ATTN_GROUPS)
HEADS_PER_GROUP = 8
HEAD_DIM = 64
N_ATTN_HEADS = N_GROUPS * HEADS_PER_GROUP
ATTN_WIDTH = HEADS_PER_GROUP * HEAD_DIM
BAND = max(w // (2 * d) for (w, d) in ATTN_GROUPS)
N_BUCKETS = 32
BUCKET_MAX_EXACT = 8
BUCKET_MAX_DIST = 1024
NEG_INF = -1e30

RG_WIDTH = D_MODEL // 2
RG_BLOCKS = 8
RG_BLOCK = RG_WIDTH // RG_BLOCKS
RG_CONV = 4
RG_CONV_LEFT = 2
RG_C = 8.0

N_BRANCH = 3
HY_COLS = (HY_ORDER + 1) * HY_WIDTH
ATTN_QKV_COLS = 3 * N_ATTN_HEADS * HEAD_DIM
RG_COLS = 2 * RG_WIDTH
IN_COLS = HY_COLS + ATTN_QKV_COLS + RG_COLS

kernel_name = "hybrid_hyena_dilated_attn_rglru_macaron"


def rms_norm(x, g):
    xf = x.astype(jnp.float32)
    y = xf * lax.rsqrt(jnp.mean(xf * xf, axis=-1, keepdims=True) + RMS_EPS)
    return (y * g.astype(jnp.float32)).astype(x.dtype)


def swiglu(x, wg, wu, wd):
    return (jax.nn.silu(x @ wg) * (x @ wu)) @ wd


def depthwise_conv(x, w, b, pad_left):
    K = w.shape[0]
    S = x.shape[1]
    xp = jnp.pad(x, ((0, 0), (pad_left, K - 1 - pad_left), (0, 0)))
    return sum(xp[:, k:k + S] * w[k] for k in range(K)) + b


def hyena_filter_spectra(L, w1, b1, w2, b2, w3, b3, freq, wout):
    f32 = jnp.float32
    t = jnp.linspace(0.0, 1.0, L, dtype=f32)[:, None]
    tr = jnp.arange(L, dtype=f32)[:, None]
    wpos = 2.0 * math.pi * tr / L
    fb = jnp.linspace(1e-4, HY_BANDS - 1, HY_BANDS, dtype=f32)[None, :]
    z = jnp.concatenate([t, jnp.cos(fb * wpos), -jnp.sin(fb * wpos)], axis=-1)
    fr = freq.astype(f32)
    hdn = jnp.sin(fr * (z @ w1.astype(f32) + b1.astype(f32)))
    hdn = jnp.sin(fr * (hdn @ w2.astype(f32) + b2.astype(f32)))
    hdn = jnp.sin(fr * (hdn @ w3.astype(f32) + b3.astype(f32)))
    h = (hdn @ wout.astype(f32)).reshape(L, 2, HY_ORDER, HY_WIDTH)
    deltas = jnp.abs(jnp.linspace(math.log(HY_DECAY_TARGET) / HY_FAST_DECAY,
                                  math.log(HY_DECAY_TARGET) / HY_SLOW_DECAY,
                                  HY_WIDTH, dtype=f32))
    decay = jnp.exp(-t * deltas[None, :]) + HY_MOD_SHIFT
    h = h * decay[:, None, None, :]
    kf = h[:, 0]
    kb = h[:, 1]
    K = jnp.concatenate([kf, jnp.zeros((1, HY_ORDER, HY_WIDTH), f32), kb[:0:-1]], axis=0)
    K = K / (jnp.sum(jnp.abs(K), axis=0, keepdims=True) + HY_NORM_EPS)
    return jnp.fft.rfft(K, axis=0)


def fft_long_conv(z, kf_spec):
    L = z.shape[1]
    Z = jnp.fft.rfft(z.astype(jnp.float32), n=2 * L, axis=1)
    y = jnp.fft.irfft(Z * kf_spec[None], n=2 * L, axis=1)[:, :L]
    return y.astype(z.dtype)


def t5_bucket(rel):
    half = N_BUCKETS // 2
    ret = jnp.where(rel > 0, half, 0)
    n = jnp.abs(rel)
    nf = jnp.maximum(n, 1).astype(jnp.float32)
    large = BUCKET_MAX_EXACT + (jnp.log(nf / BUCKET_MAX_EXACT)
                                / math.log(BUCKET_MAX_DIST / BUCKET_MAX_EXACT)
                                * (half - BUCKET_MAX_EXACT)).astype(jnp.int32)
    large = jnp.minimum(large, half - 1)
    return ret + jnp.where(n < BUCKET_MAX_EXACT, n, large)


def dilated_band_attention(q, k, v, dilation, half_span, bias_table):
    B, S, H, E = q.shape
    d = dilation
    Ls = S // d
    nb = -(-Ls // BAND)
    Lp = nb * BAND

    def to_sub(t):
        return t.reshape(B, Ls, d, H, E).transpose(0, 2, 3, 1, 4)

    pad_q = ((0, 0), (0, 0), (0, 0), (0, Lp - Ls), (0, 0))
    pad_k = ((0, 0), (0, 0), (0, 0), (BAND, Lp - Ls + BAND), (0, 0))
    qs = jnp.pad(to_sub(q), pad_q).reshape(B, d, H, nb, BAND, E)
    ks = jnp.pad(to_sub(k), pad_k).reshape(B, d, H, nb + 2, BAND, E)
    vs = jnp.pad(to_sub(v), pad_k).reshape(B, d, H, nb + 2, BAND, E)
    kb = jnp.concatenate([ks[:, :, :, 0:nb], ks[:, :, :, 1:nb + 1], ks[:, :, :, 2:nb + 2]], axis=-2)
    vb = jnp.concatenate([vs[:, :, :, 0:nb], vs[:, :, :, 1:nb + 1], vs[:, :, :, 2:nb + 2]], axis=-2)

    s = jnp.einsum('bdhnqe,bdhnke->bdhnqk', qs, kb,
                   preferred_element_type=jnp.float32) * (HEAD_DIM ** -0.5)
    qi = jnp.arange(BAND, dtype=jnp.int32)[:, None]
    kj = jnp.arange(3 * BAND, dtype=jnp.int32)[None, :]
    delta = kj - BAND - qi
    key_idx = jnp.arange(nb, dtype=jnp.int32)[:, None, None] * BAND + kj[None] - BAND
    valid = (jnp.abs(delta)[None] <= half_span) & (key_idx >= 0) & (key_idx < Ls)
    bias = bias_table.astype(jnp.float32)[t5_bucket(delta * d)]
    s = s + jnp.transpose(bias, (2, 0, 1))[None, None, :, None]
    s = jnp.where(valid[None, None, None], s, NEG_INF)
    m = jnp.max(s, axis=-1, keepdims=True)
    p = jnp.exp(s - m)
    den = jnp.sum(p, axis=-1)
    o = jnp.einsum('bdhnqk,bdhnke->bdhnqe', p, vb.astype(jnp.float32)) / den[..., None]
    lse = m[..., 0] + jnp.log(den)

    o = o.reshape(B, d, H, Lp, E)[:, :, :, :Ls].transpose(0, 3, 1, 2, 4).reshape(B, S, H, E)
    lse = lse.reshape(B, d, H, Lp)[:, :, :, :Ls].transpose(0, 3, 1, 2).reshape(B, S, H)
    return o, lse


def rg_lru_scan(xc, wa, ba, wx, bx, lam):
    B, S, _ = xc.shape
    xb = xc.reshape(B, S, RG_BLOCKS, RG_BLOCK)
    r = jax.nn.sigmoid(jnp.einsum('bshi,hij->bshj', xb, wa).reshape(B, S, RG_WIDTH) + ba)
    gi = jax.nn.sigmoid(jnp.einsum('bshi,hij->bshj', xb, wx).reshape(B, S, RG_WIDTH) + bx)
    log_a = -RG_C * r.astype(jnp.float32) * jax.nn.softplus(-lam.astype(jnp.float32))
    a = jnp.exp(log_a)
    u = jnp.sqrt(-jnp.expm1(2.0 * log_a)) * (gi * xc).astype(jnp.float32)

    def combine(e1, e2):
        a1, b1 = e1
        a2, b2 = e2
        return a1 * a2, a2 * b1 + b2

    _, h = lax.associative_scan(combine, (a, u), axis=1)
    return h


def hybrid_mixer(xn, w_in, hy_conv_w, hy_conv_b, hy_w1, hy_b1, hy_w2, hy_b2, hy_w3, hy_b3,
                 hy_freq, hy_wout, hy_skip, rel_bias, rg_conv_w, rg_conv_b, rg_wa, rg_ba,
                 rg_wx, rg_bx, rg_lambda, w_gate, b_gate, w_proj_hy, w_proj_attn, w_proj_rg, w_out):
    B, S, _ = xn.shape
    proj = xn @ w_in
    u_hy = proj[..., :HY_COLS]
    qkv = proj[..., HY_COLS:HY_COLS + ATTN_QKV_COLS]
    u_rg = proj[..., HY_COLS + ATTN_QKV_COLS:]

    uc = depthwise_conv(u_hy, hy_conv_w, hy_conv_b, (HY_SHORT_CONV - 1) // 2)
    v_hy = uc[..., :HY_WIDTH]
    gates_hy = (uc[..., HY_WIDTH:2 * HY_WIDTH], uc[..., 2 * HY_WIDTH:])
    spec = hyena_filter_spectra(S, hy_w1, hy_b1, hy_w2, hy_b2, hy_w3, hy_b3, hy_freq, hy_wout)
    z = v_hy
    for o in range(HY_ORDER):
        z = gates_hy[o] * (fft_long_conv(z, spec[:, o]) + hy_skip[o] * z)
    y_a = z

    qkv = qkv.reshape(B, S, 3, N_GROUPS, HEADS_PER_GROUP, HEAD_DIM)
    outs = []
    lses = []
    for g, (win, dil) in enumerate(ATTN_GROUPS):
        o_g, l_g = dilated_band_attention(
            qkv[:, :, 0, g], qkv[:, :, 1, g], qkv[:, :, 2, g], dil, win // (2 * dil),
            rel_bias[:, g * HEADS_PER_GROUP:(g + 1) * HEADS_PER_GROUP])
        outs.append(o_g)
        lses.append(l_g)
    wts = jax.nn.softmax(jnp.stack(lses, axis=-1), axis=-1)
    o_att = jnp.einsum('gbshe,bshg->bshe', jnp.stack(outs, axis=0), wts)
    y_b = o_att.reshape(B, S, ATTN_WIDTH).astype(xn.dtype)

    x_rg = u_rg[..., :RG_WIDTH]
    gate_rg = u_rg[..., RG_WIDTH:]
    xc = depthwise_conv(x_rg, rg_conv_w, rg_conv_b, RG_CONV_LEFT)
    h_f = rg_lru_scan(xc, rg_wa[0], rg_ba[0], rg_wx[0], rg_bx[0], rg_lambda[0])
    h_b = jnp.flip(rg_lru_scan(jnp.flip(xc, axis=1), rg_wa[1], rg_ba[1], rg_wx[1],
                               rg_bx[1], rg_lambda[1]), axis=1)
    y_c = (h_f + h_b).astype(xn.dtype) * jax.nn.gelu(gate_rg)

    gates = jax.nn.sigmoid(xn @ w_gate + b_gate).reshape(B, S, N_BRANCH, D_MODEL)
    merged = (gates[:, :, 0] * (y_a @ w_proj_hy)
              + gates[:, :, 1] * (y_b @ w_proj_attn)
              + gates[:, :, 2] * (y_c @ w_proj_rg))
    return merged @ w_out


def setup_inputs(seed: int = 0) -> dict:
    key = jax.random.key(seed)
    ks = iter(jax.random.split(key, 64))
    f32 = jnp.float32
    L = DEPTH

    def nrm(shape, scale):
        return jax.random.normal(next(ks), shape, f32) * scale

    def gain(shape):
        return 1.0 + nrm(shape, 0.05)

    lam_u = jax.random.uniform(next(ks), (L, 2, RG_WIDTH), f32, 0.9, 0.999) ** (1.0 / RG_C)
    inputs = {
        "x": nrm((BATCH, SEQ, D_MODEL), 1.0),
        "ffn1_norm": gain((L, D_MODEL)),
        "ffn1_wg": nrm((L, D_MODEL, D_FF), D_MODEL ** -0.5),
        "ffn1_wu": nrm((L, D_MODEL, D_FF), D_MODEL ** -0.5),
        "ffn1_wd": nrm((L, D_FF, D_MODEL), D_FF ** -0.5),
        "mix_norm": gain((L, D_MODEL)),
        "w_in": nrm((L, D_MODEL, IN_COLS), D_MODEL ** -0.5),
        "hy_conv_w": nrm((L, HY_SHORT_CONV, HY_COLS), HY_SHORT_CONV ** -0.5),
        "hy_conv_b": nrm((L, HY_COLS), 0.02),
        "hy_w1": nrm((L, HY_EMB, HY_FILTER_HIDDEN), HY_EMB ** -0.5),
        "hy_b1": nrm((L, HY_FILTER_HIDDEN), 0.1),
        "hy_w2": nrm((L, HY_FILTER_HIDDEN, HY_FILTER_HIDDEN), HY_FILTER_HIDDEN ** -0.5),
        "hy_b2": nrm((L, HY_FILTER_HIDDEN), 0.1),
        "hy_w3": nrm((L, HY_FILTER_HIDDEN, HY_FILTER_HIDDEN), HY_FILTER_HIDDEN ** -0.5),
        "hy_b3": nrm((L, HY_FILTER_HIDDEN), 0.1),
        "hy_freq": gain((L, HY_FILTER_HIDDEN)),
        "hy_wout": nrm((L, HY_FILTER_HIDDEN, 2 * HY_ORDER * HY_WIDTH), HY_FILTER_HIDDEN ** -0.5),
        "hy_skip": nrm((L, HY_ORDER, HY_WIDTH), 0.5),
        "rel_bias": nrm((N_BUCKETS, N_ATTN_HEADS), 0.2),
        "rg_conv_w": nrm((L, RG_CONV, RG_WIDTH), RG_CONV ** -0.5),
        "rg_conv_b": nrm((L, RG_WIDTH), 0.02),
        "rg_wa": nrm((L, 2, RG_BLOCKS, RG_BLOCK, RG_BLOCK), RG_BLOCK ** -0.5),
        "rg_ba": nrm((L, 2, RG_WIDTH), 0.02),
        "rg_wx": nrm((L, 2, RG_BLOCKS, RG_BLOCK, RG_BLOCK), RG_BLOCK ** -0.5),
        "rg_bx": nrm((L, 2, RG_WIDTH), 0.02),
        "rg_lambda": jnp.log(lam_u / (1.0 - lam_u)),
        "w_gate": nrm((L, D_MODEL, N_BRANCH * D_MODEL), D_MODEL ** -0.5),
        "b_gate": nrm((L, N_BRANCH * D_MODEL), 0.02),
        "w_proj_hy": nrm((L, HY_WIDTH, D_MODEL), HY_WIDTH ** -0.5),
        "w_proj_attn": nrm((L, ATTN_WIDTH, D_MODEL), ATTN_WIDTH ** -0.5),
        "w_proj_rg": nrm((L, RG_WIDTH, D_MODEL), RG_WIDTH ** -0.5),
        "w_out": nrm((L, D_MODEL, D_MODEL), D_MODEL ** -0.5),
        "ffn2_norm": gain((L, D_MODEL)),
        "ffn2_wg": nrm((L, D_MODEL, D_FF), D_MODEL ** -0.5),
        "ffn2_wu": nrm((L, D_MODEL, D_FF), D_MODEL ** -0.5),
        "ffn2_wd": nrm((L, D_FF, D_MODEL), D_FF ** -0.5),
        "final_norm": gain((D_MODEL,)),
    }
    return inputs


def reference(x, ffn1_norm, ffn1_wg, ffn1_wu, ffn1_wd, mix_norm, w_in, hy_conv_w, hy_conv_b,
              hy_w1, hy_b1, hy_w2, hy_b2, hy_w3, hy_b3, hy_freq, hy_wout, hy_skip, rel_bias,
              rg_conv_w, rg_conv_b, rg_wa, rg_ba, rg_wx, rg_bx, rg_lambda, w_gate, b_gate,
              w_proj_hy, w_proj_attn, w_proj_rg, w_out, ffn2_norm, ffn2_wg, ffn2_wu, ffn2_wd,
              final_norm):
    h = x
    for l in range(DEPTH):
        h = h + 0.5 * swiglu(rms_norm(h, ffn1_norm[l]), ffn1_wg[l], ffn1_wu[l], ffn1_wd[l])
        h = h + hybrid_mixer(
            rms_norm(h, mix_norm[l]), w_in[l], hy_conv_w[l], hy_conv_b[l],
            hy_w1[l], hy_b1[l], hy_w2[l], hy_b2[l], hy_w3[l], hy_b3[l], hy_freq[l], hy_wout[l],
            hy_skip[l], rel_bias, rg_conv_w[l], rg_conv_b[l], rg_wa[l], rg_ba[l], rg_wx[l],
            rg_bx[l], rg_lambda[l], w_gate[l], b_gate[l], w_proj_hy[l], w_proj_attn[l],
            w_proj_rg[l], w_out[l])
        h = h + 0.5 * swiglu(rms_norm(h, ffn2_norm[l]), ffn2_wg[l], ffn2_wu[l], ffn2_wd[l])
    return rms_norm(h, final_norm)
```

```python
import functools
import math

import numpy as np
import jax
import jax.numpy as jnp
from jax import lax
from jax.experimental import pallas as pl
from jax.experimental.pallas import tpu as pltpu

F32 = jnp.float32
BF16 = jnp.bfloat16

D_MODEL = 1024
D_FF = 2816
DEPTH = 4
RMS_EPS = 1e-6

HY_WIDTH = 512
HY_ORDER = 2
HY_EMB = 33
HY_BANDS = 16
HY_HIDDEN = 64
HY_COLS = 3 * HY_WIDTH
HY_DECAY_TARGET = 1e-2
HY_FAST_DECAY = 0.3
HY_SLOW_DECAY = 1.5
HY_MOD_SHIFT = 0.05
HY_NORM_EPS = 1e-6

ATTN_GROUPS = ((128, 1), (512, 4), (2048, 16))
N_GROUPS = 3
HEADS = 8
HEAD_DIM = 64
ATTN_WIDTH = HEADS * HEAD_DIM
QKV_COLS = 3 * N_GROUPS * ATTN_WIDTH
BAND = 64
N_BUCKETS = 32
BUCKET_MAX_EXACT = 8
BUCKET_MAX_DIST = 1024
NEG_INF = -1e30

RG_WIDTH = 512
RG_BLOCKS = 8
RG_BLOCK = 64
RG_CONV = 4
RG_C = 8.0
RG_COLS = 2 * RG_WIDTH

N_BRANCH = 3

LANES = 128
VMEM_LIMIT_BYTES = 56 * 1024 * 1024

TOKEN_TILE = 512
Q_TILE = 128
K_WIN = Q_TILE + 2 * BAND
HY_CBLK = 256
RG_TT = 128
HALO_T = 8


def _const_spec(shape):
    nd = len(shape)
    return pl.BlockSpec(shape, lambda *_: (0,) * nd, pipeline_mode=pl.Buffered(1))


def _rms(x, g):
    ms = jnp.mean(x * x, axis=-1, keepdims=True)
    return x * lax.rsqrt(ms + RMS_EPS) * g


def _sigmoid(x):
    return 1.0 / (1.0 + jnp.exp(-x))


def _ffn_body(h_ref, g_ref, wgu_ref, wd_ref, o_ref):
    x = h_ref[...]
    xb = _rms(x, g_ref[...]).astype(BF16)
    gu = jnp.dot(xb, wgu_ref[...], preferred_element_type=F32)
    gate = gu[:, :D_FF]
    up = gu[:, D_FF:]
    act = (gate * _sigmoid(gate)) * up
    out = jnp.dot(act.astype(BF16), wd_ref[...], preferred_element_type=F32)
    o_ref[...] = x + 0.5 * out


def _ffn(h, g, wgu, wd):
    n_tok = h.shape[0]
    return pl.pallas_call(
        _ffn_body,
        out_shape=jax.ShapeDtypeStruct(h.shape, F32),
        grid=(n_tok // TOKEN_TILE,),
        in_specs=[
            pl.BlockSpec((TOKEN_TILE, D_MODEL), lambda i: (i, 0)),
            _const_spec((1, D_MODEL)),
            _const_spec((D_MODEL, 2 * D_FF)),
            _const_spec((D_FF, D_MODEL)),
        ],
        out_specs=pl.BlockSpec((TOKEN_TILE, D_MODEL), lambda i: (i, 0)),
        compiler_params=pltpu.CompilerParams(
            dimension_semantics=("arbitrary",), vmem_limit_bytes=VMEM_LIMIT_BYTES),
        name="ffn",
    )(h, g, wgu, wd)


def _inproj_body(h_ref, g_ref, why_ref, wqkv_ref, wrg_ref, hy_ref, qkv_ref, rg_ref):
    xb = _rms(h_ref[0], g_ref[...]).astype(BF16)
    hy_ref[0] = jnp.dot(xb, why_ref[...], preferred_element_type=F32).astype(BF16)
    qkv_ref[0] = jnp.dot(xb, wqkv_ref[...], preferred_element_type=F32).astype(BF16)
    rg_ref[...] = jnp.dot(xb, wrg_ref[...], preferred_element_type=F32).astype(BF16)


def _inproj(h3, g, w_hy, w_qkv, w_rg):
    nb, seq, _ = h3.shape
    return pl.pallas_call(
        _inproj_body,
        out_shape=(
            jax.ShapeDtypeStruct((nb, seq, HY_COLS), BF16),
            jax.ShapeDtypeStruct((nb, seq, QKV_COLS), BF16),
            jax.ShapeDtypeStruct((seq, nb * RG_COLS), BF16),
        ),
        grid=(nb, seq // TOKEN_TILE),
        in_specs=[
            pl.BlockSpec((1, TOKEN_TILE, D_MODEL), lambda b, i: (b, i, 0)),
            _const_spec((1, D_MODEL)),
            _const_spec((D_MODEL, HY_COLS)),
            _const_spec((D_MODEL, QKV_COLS)),
            _const_spec((D_MODEL, RG_COLS)),
        ],
        out_specs=(
            pl.BlockSpec((1, TOKEN_TILE, HY_COLS), lambda b, i: (b, i, 0)),
            pl.BlockSpec((1, TOKEN_TILE, QKV_COLS), lambda b, i: (b, i, 0)),
            pl.BlockSpec((TOKEN_TILE, RG_COLS), lambda b, i: (i, b)),
        ),
        compiler_params=pltpu.CompilerParams(
            dimension_semantics=("arbitrary", "arbitrary"), vmem_limit_bytes=VMEM_LIMIT_BYTES),
        name="inproj",
    )(h3, g, w_hy, w_qkv, w_rg)


def _dft_mats(seq):
    n = jnp.arange(seq, dtype=jnp.int32)
    kn = (n[:, None] * n[None, :]) % (2 * seq)
    ang = kn.astype(F32) * (math.pi / seq)
    return jnp.cos(ang).astype(BF16), (-jnp.sin(ang)).astype(BF16)


def _hy_positions(seq):
    t = jnp.linspace(0.0, 1.0, seq, dtype=F32)[:, None]
    tr = jnp.arange(seq, dtype=F32)[:, None]
    wpos = 2.0 * math.pi * tr / seq
    fb = jnp.linspace(1e-4, HY_BANDS - 1, HY_BANDS, dtype=F32)[None, :]
    z = jnp.concatenate([t, jnp.cos(fb * wpos), -jnp.sin(fb * wpos)], axis=-1)
    return jnp.pad(z, ((0, 0), (0, LANES - HY_EMB)))


def _hy_filter_body(z_ref, w1_ref, b1_ref, w2_ref, b2_ref, w3_ref, b3_ref, fr_ref,
                    wf_ref, wb_ref, dl_ref, fre_ref, fim_ref,
                    hre_ref, him_ref, hny_ref):
    hi = lax.Precision.HIGHEST
    seq = z_ref.shape[0]
    fr = fr_ref[...]
    hdn = jnp.sin(fr * (jnp.dot(z_ref[...], w1_ref[...], precision=hi,
                                preferred_element_type=F32) + b1_ref[...]))
    hdn = jnp.sin(fr * (jnp.dot(hdn, w2_ref[...], precision=hi,
                                preferred_element_type=F32) + b2_ref[...]))
    hdn = jnp.sin(fr * (jnp.dot(hdn, w3_ref[...], precision=hi,
                                preferred_element_type=F32) + b3_ref[...]))
    row = lax.broadcasted_iota(jnp.int32, (seq, HY_WIDTH), 0)
    t = row.astype(F32) * (1.0 / (seq - 1))
    decay = jnp.exp(-t * dl_ref[...]) + HY_MOD_SHIFT
    kf = jnp.dot(hdn, wf_ref[0], precision=hi, preferred_element_type=F32) * decay
    kb = jnp.dot(hdn, wb_ref[0], precision=hi, preferred_element_type=F32) * decay
    kb = jnp.where(row == 0, 0.0, kb)
    norm = (jnp.sum(jnp.abs(kf), axis=0, keepdims=True)
            + jnp.sum(jnp.abs(kb), axis=0, keepdims=True) + HY_NORM_EPS)
    kp = (kf + kb) / norm
    km = (kf - kb) / norm
    scale = jnp.where(row == 0, 0.5 / seq, 1.0 / seq)
    sgn = (1 - 2 * (row & 1)).astype(F32)
    hre_ref[0] = (jnp.dot(fre_ref[...], kp.astype(BF16), preferred_element_type=F32)
                  * scale).astype(BF16)
    him_ref[0] = (jnp.dot(fim_ref[...], km.astype(BF16), preferred_element_type=F32)
                  * scale).astype(BF16)
    nyq = jnp.sum(kp * sgn, axis=0, keepdims=True) * (0.5 / seq)
    hny_ref[0] = jnp.broadcast_to(nyq, (8, HY_WIDTH))


def _hy_filter(z, w1, b1, w2, b2, w3, b3, freq, wout, deltas, fre, fim):
    seq = z.shape[0]
    wf = wout[:, :HY_ORDER * HY_WIDTH].reshape(HY_HIDDEN, HY_ORDER, HY_WIDTH).transpose(1, 0, 2)
    wb = wout[:, HY_ORDER * HY_WIDTH:].reshape(HY_HIDDEN, HY_ORDER, HY_WIDTH).transpose(1, 0, 2)
    w1p = jnp.pad(w1, ((0, LANES - HY_EMB), (0, 0)))
    row = lambda a: a.reshape(1, -1)
    return pl.pallas_call(
        _hy_filter_body,
        out_shape=(
            jax.ShapeDtypeStruct((HY_ORDER, seq, HY_WIDTH), BF16),
            jax.ShapeDtypeStruct((HY_ORDER, seq, HY_WIDTH), BF16),
            jax.ShapeDtypeStruct((HY_ORDER, 8, HY_WIDTH), F32),
        ),
        grid=(HY_ORDER,),
        in_specs=[
            _const_spec((seq, LANES)),
            _const_spec((LANES, HY_HIDDEN)), _const_spec((1, HY_HIDDEN)),
            _const_spec((HY_HIDDEN, HY_HIDDEN)), _const_spec((1, HY_HIDDEN)),
            _const_spec((HY_HIDDEN, HY_HIDDEN)), _const_spec((1, HY_HIDDEN)),
            _const_spec((1, HY_HIDDEN)),
            pl.BlockSpec((1, HY_HIDDEN, HY_WIDTH), lambda o: (o, 0, 0)),
            pl.BlockSpec((1, HY_HIDDEN, HY_WIDTH), lambda o: (o, 0, 0)),
            _const_spec((1, HY_WIDTH)),
            _const_spec((seq, seq)), _const_spec((seq, seq)),
        ],
        out_specs=(
            pl.BlockSpec((1, seq, HY_WIDTH), lambda o: (o, 0, 0)),
            pl.BlockSpec((1, seq, HY_WIDTH), lambda o: (o, 0, 0)),
            pl.BlockSpec((1, 8, HY_WIDTH), lambda o: (o, 0, 0)),
        ),
        compiler_params=pltpu.CompilerParams(
            dimension_semantics=("arbitrary",), vmem_limit_bytes=VMEM_LIMIT_BYTES),
        name="hy_filter",
    )(z, w1p, row(b1), w2, row(b2), w3, row(b3), row(freq), wf, wb, row(deltas), fre, fim)


def _hy_conv_body(v_ref, x1_ref, x2_ref, wv_ref, w1_ref, w2_ref, bv_ref, b1_ref, b2_ref,
                  fre_ref, fim_ref, hre_ref, him_ref, hny_ref, skip_ref, o_ref):
    seq = v_ref.shape[1]
    row = lax.broadcasted_iota(jnp.int32, (seq, HY_CBLK), 0)
    sgn = (1 - 2 * (row & 1)).astype(F32)

    def short_conv(x_ref, w_ref, b_ref):
        x = x_ref[0].astype(F32)
        w = w_ref[...]
        prev = jnp.where(row == 0, 0.0, pltpu.roll(x, 1, 0))
        nxt = jnp.where(row == seq - 1, 0.0, pltpu.roll(x, seq - 1, 0))
        return w[0:1] * prev + w[1:2] * x + w[2:3] * nxt + b_ref[...]

    def long_conv(z, o):
        zb = z.astype(BF16)
        zre = jnp.dot(fre_ref[...], zb, preferred_element_type=F32)
        zim = jnp.dot(fim_ref[...], zb, preferred_element_type=F32)
        hre = hre_ref[o].astype(F32)
        him = him_ref[o].astype(F32)
        pre = (zre * hre - zim * him).astype(BF16)
        pim = (zre * him + zim * hre).astype(BF16)
        znyq = jnp.sum(z * sgn, axis=0, keepdims=True)
        y = jnp.dot(fre_ref[...], pre, preferred_element_type=F32)
        y = y + jnp.dot(fim_ref[...], pim, preferred_element_type=F32)
        return y + sgn * (znyq * hny_ref[o][0:1])

    v = short_conv(v_ref, wv_ref, bv_ref)
    x1 = short_conv(x1_ref, w1_ref, b1_ref)
    x2 = short_conv(x2_ref, w2_ref, b2_ref)
    skip = skip_ref[...]
    z = x1 * (long_conv(v, 0) + skip[0:1] * v)
    z = x2 * (long_conv(z, 1) + skip[1:2] * z)
    o_ref[0] = z.astype(BF16)


def _hy_conv(u_hy, conv_w, conv_b, fre, fim, hre, him, hny, skip):
    nb, seq, _ = u_hy.shape
    ncb = HY_WIDTH // HY_CBLK
    data = lambda part: pl.BlockSpec((1, seq, HY_CBLK), lambda c, b: (b, 0, part * ncb + c))
    wspec = lambda part: pl.BlockSpec((3, HY_CBLK), lambda c, b: (0, part * ncb + c))
    bspec = lambda part: pl.BlockSpec((1, HY_CBLK), lambda c, b: (0, part * ncb + c))
    return pl.pallas_call(
        _hy_conv_body,
        out_shape=jax.ShapeDtypeStruct((nb, seq, HY_WIDTH), BF16),
        grid=(ncb, nb),
        in_specs=[
            data(0), data(1), data(2),
            wspec(0), wspec(1), wspec(2),
            bspec(0), bspec(1), bspec(2),
            _const_spec((seq, seq)), _const_spec((seq, seq)),
            pl.BlockSpec((HY_ORDER, seq, HY_CBLK), lambda c, b: (0, 0, c),
                         pipeline_mode=pl.Buffered(1)),
            pl.BlockSpec((HY_ORDER, seq, HY_CBLK), lambda c, b: (0, 0, c),
                         pipeline_mode=pl.Buffered(1)),
            pl.BlockSpec((HY_ORDER, 8, HY_CBLK), lambda c, b: (0, 0, c)),
            pl.BlockSpec((HY_ORDER, HY_CBLK), lambda c, b: (0, c)),
        ],
        out_specs=pl.BlockSpec((1, seq, HY_CBLK), lambda c, b: (b, 0, c)),
        compiler_params=pltpu.CompilerParams(
            dimension_semantics=("arbitrary", "arbitrary"), vmem_limit_bytes=VMEM_LIMIT_BYTES),
        name="hy_conv",
    )(u_hy, u_hy, u_hy, conv_w, conv_w, conv_w, conv_b, conv_b, conv_b,
      fre, fim, hre, him, hny, skip)


def _bucket_matrix(dilation):
    qi = np.arange(Q_TILE, dtype=np.int64)[:, None]
    kj = np.arange(K_WIN, dtype=np.int64)[None, :]
    delta = kj - BAND - qi
    rel = delta * dilation
    half = N_BUCKETS // 2
    n = np.abs(rel)
    nf = np.maximum(n, 1).astype(np.float32)
    large = BUCKET_MAX_EXACT + (
        np.log(nf / np.float32(BUCKET_MAX_EXACT)) / np.float32(math.log(BUCKET_MAX_DIST / BUCKET_MAX_EXACT))
        * np.float32(half - BUCKET_MAX_EXACT)).astype(np.int32)
    large = np.minimum(large, half - 1)
    bucket = np.where(rel > 0, half, 0) + np.where(n < BUCKET_MAX_EXACT, n, large)
    return np.where(np.abs(delta) <= BAND, bucket, -1).astype(np.int32)


def _attn_body(group, ls, tbl_ref, bkt_ref, q_ref, k_ref, v_ref, o_ref, l_ref,
               bias_ref, kpad_ref, vpad_ref):
    first = (pl.program_id(0) == 0) & (pl.program_id(1) == 0)

    @pl.when(first)
    def _():
        bkt = bkt_ref[...]
        for h in range(HEADS):
            acc = jnp.full((Q_TILE, K_WIN), NEG_INF, F32)
            for bk in range(N_BUCKETS):
                acc = jnp.where(bkt == bk, tbl_ref[bk, group * HEADS + h], acc)
            bias_ref[h] = acc
        zeros = jnp.zeros((BAND, ATTN_WIDTH), BF16)
        kpad_ref[0:BAND] = zeros
        kpad_ref[BAND + ls:BAND + ls + BAND] = zeros
        vpad_ref[0:BAND] = zeros
        vpad_ref[BAND + ls:BAND + ls + BAND] = zeros

    kpad_ref[BAND:BAND + ls] = k_ref[0]
    vpad_ref[BAND:BAND + ls] = v_ref[0]

    lane = lax.broadcasted_iota(jnp.int32, (Q_TILE, LANES), 1)
    low_half = lane < HEAD_DIM
    kcol = lax.broadcasted_iota(jnp.int32, (1, K_WIN), 1)
    scale = HEAD_DIM ** -0.5

    def tile(t, carry):
        j0 = pl.multiple_of(t * Q_TILE, Q_TILE)
        kidx = j0 - BAND + kcol
        kmask = jnp.where((kidx >= 0) & (kidx < ls), 0.0, NEG_INF)
        lse_tile = jnp.zeros((Q_TILE, LANES), F32)
        for hp in range(HEADS // 2):
            cols = slice(hp * LANES, (hp + 1) * LANES)
            q2 = q_ref[0, pl.ds(j0, Q_TILE), cols] * scale
            kw = kpad_ref[pl.ds(j0, K_WIN), cols]
            vw = vpad_ref[pl.ds(j0, K_WIN), cols]
            outs = []
            for half in range(2):
                h = 2 * hp + half
                qh = jnp.where(low_half if half == 0 else ~low_half, q2, 0.0).astype(BF16)
                s = lax.dot_general(qh, kw, (((1,), (1,)), ((), ())), preferred_element_type=F32)
                s = s + bias_ref[h] + kmask
                m = jnp.max(s, axis=-1, keepdims=True)
                p = jnp.exp(s - m)
                den = jnp.sum(p, axis=-1, keepdims=True)
                pv = jnp.dot(p.astype(BF16), vw, preferred_element_type=F32)
                outs.append(pv / den)
                lse_tile = jnp.where(lane == h, m + jnp.log(den), lse_tile)
            o_ref[0, pl.ds(j0, Q_TILE), cols] = jnp.where(low_half, outs[0], outs[1]).astype(BF16)
        l_ref[0, pl.ds(j0, Q_TILE), :] = lse_tile
        return carry

    lax.fori_loop(0, ls // Q_TILE, tile, 0)


def _attn_group(qkv, tbl, group, dilation):
    nb, seq, _ = qkv.shape
    ls = seq // dilation
    qkv_v = qkv.reshape(nb, ls, dilation * QKV_COLS)
    nblk = QKV_COLS // ATTN_WIDTH
    part = lambda which: pl.BlockSpec(
        (1, ls, ATTN_WIDTH), lambda b, r: (b, 0, r * nblk + which * N_GROUPS + group))
    bkt = jnp.asarray(_bucket_matrix(dilation))
    o, lse = pl.pallas_call(
        functools.partial(_attn_body, group, ls),
        out_shape=(
            jax.ShapeDtypeStruct((nb, ls, dilation * ATTN_WIDTH), BF16),
            jax.ShapeDtypeStruct((nb, ls, dilation * LANES), F32),
        ),
        grid=(nb, dilation),
        in_specs=[
            pl.BlockSpec(memory_space=pltpu.SMEM),
            _const_spec((Q_TILE, K_WIN)),
            part(0), part(1), part(2),
        ],
        out_specs=(
            pl.BlockSpec((1, ls, ATTN_WIDTH), lambda b, r: (b, 0, r)),
            pl.BlockSpec((1, ls, LANES), lambda b, r: (b, 0, r)),
        ),
        scratch_shapes=[
            pltpu.VMEM((HEADS, Q_TILE, K_WIN), F32),
            pltpu.VMEM((ls + 2 * BAND, ATTN_WIDTH), BF16),
            pltpu.VMEM((ls + 2 * BAND, ATTN_WIDTH), BF16),
        ],
        compiler_params=pltpu.CompilerParams(
            dimension_semantics=("arbitrary", "arbitrary"), vmem_limit_bytes=VMEM_LIMIT_BYTES),
        name=f"attn_g{group}",
    )(tbl, bkt, qkv_v, qkv_v, qkv_v)
    return o.reshape(nb, seq, ATTN_WIDTH), lse.reshape(nb, seq, LANES)


def _rglru_body(nt, cw_ref, cb_ref, wf_ref, wb_ref, bf_ref, bb_ref, lam_ref,
                xf_ref, pf_ref, nf_ref, xb_ref, pb_ref, nb_ref,
                hf_ref, hb_ref, a_ref, u_ref, cf_ref, cbk_ref):
    i = pl.program_id(0)
    tt, nbat, width = xf_ref.shape
    rows = tt * nbat

    @pl.when(i == 0)
    def _():
        cf_ref[...] = jnp.zeros_like(cf_ref)
        cbk_ref[...] = jnp.zeros_like(cbk_ref)

    def gates(x_ref, p_ref, n_ref, tile_idx, w_ref, b_ref, lam):
        x = x_ref[...].astype(F32).reshape(rows, width)
        prev = p_ref[...].astype(F32).reshape(HALO_T * nbat, width)
        nxt = n_ref[...].astype(F32).reshape(HALO_T * nbat, width)
        prev = jnp.where(tile_idx == 0, 0.0, prev)
        nxt = jnp.where(tile_idx == nt - 1, 0.0, nxt)
        xm2 = jnp.concatenate([prev[(HALO_T - 2) * nbat:], x[:rows - 2 * nbat]], axis=0)
        xm1 = jnp.concatenate([prev[(HALO_T - 1) * nbat:], x[:rows - nbat]], axis=0)
        xp1 = jnp.concatenate([x[nbat:], nxt[:nbat]], axis=0)
        cw = cw_ref[...]
        xc = cw[0:1] * xm2 + cw[1:2] * xm1 + cw[2:3] * x + cw[3:4] * xp1 + cb_ref[...]
        g = jnp.dot(xc.astype(BF16), w_ref[...], preferred_element_type=F32) + b_ref[...]
        r = _sigmoid(g[:, :width])
        gi = _sigmoid(g[:, width:])
        softplus = jnp.maximum(-lam, 0.0) + jnp.log(1.0 + jnp.exp(-jnp.abs(lam)))
        log_a = (-RG_C * softplus) * r
        a = jnp.exp(log_a)
        a_ref[...] = a
        u_ref[...] = jnp.sqrt(1.0 - a * a) * (gi * xc)

    gates(xf_ref, pf_ref, nf_ref, i, wf_ref, bf_ref, lam_ref[0:1])

    def fwd(t, h):
        r0 = pl.multiple_of(t * nbat, nbat)
        h = a_ref[pl.ds(r0, nbat), :] * h + u_ref[pl.ds(r0, nbat), :]
        hf_ref[t] = h.astype(BF16)
        return h

    cf_ref[...] = lax.fori_loop(0, tt, fwd, cf_ref[...], unroll=8)

    gates(xb_ref, pb_ref, nb_ref, nt - 1 - i, wb_ref, bb_ref, lam_ref[1:2])

    def bwd(s, h):
        t = tt - 1 - s
        r0 = pl.multiple_of(t * nbat, nbat)
        h = a_ref[pl.ds(r0, nbat), :] * h + u_ref[pl.ds(r0, nbat), :]
        hb_ref[t] = h.astype(BF16)
        return h

    cbk_ref[...] = lax.fori_loop(0, tt, bwd, cbk_ref[...], unroll=8)


def _rglru(rg3, conv_w, conv_b, w_f, w_b, b_f, b_b, lam):
    seq, nbat, _ = rg3.shape
    nt = seq // RG_TT
    hpt = RG_TT // HALO_T
    nh = seq // HALO_T
    cur = lambda f: pl.BlockSpec((RG_TT, nbat, RG_WIDTH), lambda i: (f(i), 0, 0))
    prv = lambda f: pl.BlockSpec(
        (HALO_T, nbat, RG_WIDTH), lambda i: (jnp.maximum(f(i) * hpt - 1, 0), 0, 0))
    nxt = lambda f: pl.BlockSpec(
        (HALO_T, nbat, RG_WIDTH), lambda i: (jnp.minimum((f(i) + 1) * hpt, nh - 1), 0, 0))
    fw = lambda i: i
    bw = lambda i: nt - 1 - i
    return pl.pallas_call(
        functools.partial(_rglru_body, nt),
        out_shape=(
            jax.ShapeDtypeStruct((seq, nbat, RG_WIDTH), BF16),
            jax.ShapeDtypeStruct((seq, nbat, RG_WIDTH), BF16),
        ),
        grid=(nt,),
        in_specs=[
            _const_spec((RG_CONV, RG_WIDTH)), _const_spec((1, RG_WIDTH)),
            _const_spec((RG_WIDTH, 2 * RG_WIDTH)), _const_spec((RG_WIDTH, 2 * RG_WIDTH)),
            _const_spec((1, 2 * RG_WIDTH)), _const_spec((1, 2 * RG_WIDTH)),
            _const_spec((2, RG_WIDTH)),
            cur(fw), prv(fw), nxt(fw), cur(bw), prv(bw), nxt(bw),
        ],
        out_specs=(cur(fw), cur(bw)),
        scratch_shapes=[
            pltpu.VMEM((RG_TT * nbat, RG_WIDTH), F32),
            pltpu.VMEM((RG_TT * nbat, RG_WIDTH), F32),
            pltpu.VMEM((nbat, RG_WIDTH), F32),
            pltpu.VMEM((nbat, RG_WIDTH), F32),
        ],
        compiler_params=pltpu.CompilerParams(
            dimension_semantics=("arbitrary",), vmem_limit_bytes=VMEM_LIMIT_BYTES),
        name="rglru",
    )(conv_w, conv_b, w_f, w_b, b_f, b_b, lam, rg3, rg3, rg3, rg3, rg3, rg3)


def _block_diag(w):
    eye = jnp.eye(RG_BLOCKS, dtype=w.dtype)
    return jnp.einsum("hij,hk->hikj", w, eye).reshape(RG_WIDTH, RG_WIDTH)


def _merge_body(h_ref, g_ref, wg_ref, bg_ref, ya_ref, o0_ref, o1_ref, o2_ref,
                l0_ref, l1_ref, l2_ref, hf_ref, hb_ref, gate_ref, exp_ref,
                wphy_ref, wpat_ref, wprg_ref, wout_ref, out_ref):
    x = h_ref[0]
    xb = _rms(x, g_ref[...]).astype(BF16)
    gates = _sigmoid(jnp.dot(xb, wg_ref[...], preferred_element_type=F32) + bg_ref[...])

    l0, l1, l2 = l0_ref[0], l1_ref[0], l2_ref[0]
    m = jnp.maximum(jnp.maximum(l0, l1), l2)
    e0, e1, e2 = jnp.exp(l0 - m), jnp.exp(l1 - m), jnp.exp(l2 - m)
    inv = 1.0 / (e0 + e1 + e2)

    def widen(w):
        hi = w.astype(BF16)
        lo = (w - hi.astype(F32)).astype(BF16)
        return (jnp.dot(hi, exp_ref[...], preferred_element_type=F32)
                + jnp.dot(lo, exp_ref[...], preferred_element_type=F32))

    y_b = (widen(e0 * inv) * o0_ref[0].astype(F32)
           + widen(e1 * inv) * o1_ref[0].astype(F32)
           + widen(e2 * inv) * o2_ref[0].astype(F32))

    gt = gate_ref[...].astype(F32)
    gelu = 0.5 * gt * (1.0 + jnp.tanh(math.sqrt(2.0 / math.pi) * (gt + 0.044715 * (gt * gt * gt))))
    y_c = (hf_ref[...].astype(F32) + hb_ref[...].astype(F32)) * gelu

    merged = (gates[:, :D_MODEL]
              * jnp.dot(ya_ref[0], wphy_ref[...], preferred_element_type=F32)
              + gates[:, D_MODEL:2 * D_MODEL]
              * jnp.dot(y_b.astype(BF16), wpat_ref[...], preferred_element_type=F32)
              + gates[:, 2 * D_MODEL:]
              * jnp.dot(y_c.astype(BF16), wprg_ref[...], preferred_element_type=F32))
    out_ref[0] = x + jnp.dot(merged.astype(BF16), wout_ref[...], preferred_element_type=F32)


def _merge(h3, g, w_gate, b_gate, y_a, attn, hf2, hb2, rg2, expand, wp_hy, wp_attn, wp_rg, w_out):
    nb, seq, _ = h3.shape
    tok = lambda width: pl.BlockSpec((1, TOKEN_TILE, width), lambda b, i: (b, i, 0))
    (o0, l0), (o1, l1), (o2, l2) = attn
    return pl.pallas_call(
        _merge_body,
        out_shape=jax.ShapeDtypeStruct(h3.shape, F32),
        grid=(nb, seq // TOKEN_TILE),
        in_specs=[
            tok(D_MODEL), _const_spec((1, D_MODEL)),
            _const_spec((D_MODEL, N_BRANCH * D_MODEL)), _const_spec((1, N_BRANCH * D_MODEL)),
            tok(HY_WIDTH), tok(ATTN_WIDTH), tok(ATTN_WIDTH), tok(ATTN_WIDTH),
            tok(LANES), tok(LANES), tok(LANES),
            pl.BlockSpec((TOKEN_TILE, RG_WIDTH), lambda b, i: (i, b)),
            pl.BlockSpec((TOKEN_TILE, RG_WIDTH), lambda b, i: (i, b)),
            pl.BlockSpec((TOKEN_TILE, RG_WIDTH), lambda b, i: (i, 2 * b + 1)),
            _const_spec((LANES, ATTN_WIDTH)),
            _const_spec((HY_WIDTH, D_MODEL)), _const_spec((ATTN_WIDTH, D_MODEL)),
            _const_spec((RG_WIDTH, D_MODEL)), _const_spec((D_MODEL, D_MODEL)),
        ],
        out_specs=tok(D_MODEL),
        compiler_params=pltpu.CompilerParams(
            dimension_semantics=("arbitrary", "arbitrary"), vmem_limit_bytes=VMEM_LIMIT_BYTES),
        name="merge",
    )(h3, g, w_gate, b_gate, y_a, o0, o1, o2, l0, l1, l2, hf2, hb2, rg2, expand,
      wp_hy, wp_attn, wp_rg, w_out)


def _final_norm_body(h_ref, g_ref, o_ref):
    o_ref[...] = _rms(h_ref[...], g_ref[...])


def _final_norm(h, g):
    n_tok = h.shape[0]
    return pl.pallas_call(
        _final_norm_body,
        out_shape=jax.ShapeDtypeStruct(h.shape, F32),
        grid=(n_tok // TOKEN_TILE,),
        in_specs=[pl.BlockSpec((TOKEN_TILE, D_MODEL), lambda i: (i, 0)), _const_spec((1, D_MODEL))],
        out_specs=pl.BlockSpec((TOKEN_TILE, D_MODEL), lambda i: (i, 0)),
        compiler_params=pltpu.CompilerParams(dimension_semantics=("arbitrary",)),
        name="final_norm",
    )(h, g)


def _mixer(h3, l, p, consts):
    nb, seq, _ = h3.shape
    fre, fim, zpos, deltas, expand = consts
    w_in = p["w_in"][l].astype(BF16)
    hy, qkv, rg = _inproj(
        h3, p["mix_norm"][l].reshape(1, -1),
        w_in[:, :HY_COLS], w_in[:, HY_COLS:HY_COLS + QKV_COLS], w_in[:, HY_COLS + QKV_COLS:])

    hre, him, hny = _hy_filter(
        zpos, p["hy_w1"][l], p["hy_b1"][l], p["hy_w2"][l], p["hy_b2"][l], p["hy_w3"][l],
        p["hy_b3"][l], p["hy_freq"][l], p["hy_wout"][l], deltas, fre, fim)
    y_a = _hy_conv(hy, p["hy_conv_w"][l], p["hy_conv_b"][l].reshape(1, -1),
                   fre, fim, hre, him, hny, p["hy_skip"][l])

    attn = [_attn_group(qkv, p["rel_bias"], g, dil) for g, (_, dil) in enumerate(ATTN_GROUPS)]

    rg_w = lambda d: jnp.concatenate(
        [_block_diag(p["rg_wa"][l, d]), _block_diag(p["rg_wx"][l, d])], axis=1).astype(BF16)
    rg_b = lambda d: jnp.concatenate([p["rg_ba"][l, d], p["rg_bx"][l, d]]).reshape(1, -1)
    hf, hb = _rglru(rg.reshape(seq, nb, RG_COLS), p["rg_conv_w"][l],
                    p["rg_conv_b"][l].reshape(1, -1), rg_w(0), rg_w(1), rg_b(0), rg_b(1),
                    p["rg_lambda"][l])

    return _merge(
        h3, p["mix_norm"][l].reshape(1, -1), p["w_gate"][l].astype(BF16),
        p["b_gate"][l].reshape(1, -1), y_a, attn,
        hf.reshape(seq, nb * RG_WIDTH), hb.reshape(seq, nb * RG_WIDTH), rg, expand,
        p["w_proj_hy"][l].astype(BF16), p["w_proj_attn"][l].astype(BF16),
        p["w_proj_rg"][l].astype(BF16), p["w_out"][l].astype(BF16))


def _forward(x, p):
    nb, seq, _ = x.shape
    fre, fim = _dft_mats(seq)
    deltas = jnp.abs(jnp.linspace(math.log(HY_DECAY_TARGET) / HY_FAST_DECAY,
                                  math.log(HY_DECAY_TARGET) / HY_SLOW_DECAY, HY_WIDTH, dtype=F32))
    head_of_lane = jnp.arange(ATTN_WIDTH, dtype=jnp.int32) // HEAD_DIM
    expand = (jnp.arange(LANES, dtype=jnp.int32)[:, None] == head_of_lane[None, :]).astype(BF16)
    consts = (fre, fim, _hy_positions(seq), deltas, expand)

    def ffn(h3, norm, wg, wu, wd, l):
        wgu = jnp.concatenate([wg[l], wu[l]], axis=1).astype(BF16)
        out = _ffn(h3.reshape(nb * seq, D_MODEL), norm[l].reshape(1, -1), wgu, wd[l].astype(BF16))
        return out.reshape(nb, seq, D_MODEL)

    h = x
    for l in range(DEPTH):
        h = ffn(h, p["ffn1_norm"], p["ffn1_wg"], p["ffn1_wu"], p["ffn1_wd"], l)
        h = _mixer(h, l, p, consts)
        h = ffn(h, p["ffn2_norm"], p["ffn2_wg"], p["ffn2_wu"], p["ffn2_wd"], l)
    out = _final_norm(h.reshape(nb * seq, D_MODEL), p["final_norm"].reshape(1, -1))
    return out.reshape(nb, seq, D_MODEL)


def kernel(x, ffn1_norm, ffn1_wg, ffn1_wu, ffn1_wd, mix_norm, w_in, hy_conv_w, hy_conv_b, hy_w1, hy_b1, hy_w2, hy_b2, hy_w3, hy_b3, hy_freq, hy_wout, hy_skip, rel_bias, rg_conv_w, rg_conv_b, rg_wa, rg_ba, rg_wx, rg_bx, rg_lambda, w_gate, b_gate, w_proj_hy, w_proj_attn, w_proj_rg, w_out, ffn2_norm, ffn2_wg, ffn2_wu, ffn2_wd, final_norm):
    p = dict(
        ffn1_norm=ffn1_norm, ffn1_wg=ffn1_wg, ffn1_wu=ffn1_wu, ffn1_wd=ffn1_wd, mix_norm=mix_norm,
        w_in=w_in, hy_conv_w=hy_conv_w, hy_conv_b=hy_conv_b, hy_w1=hy_w1, hy_b1=hy_b1, hy_w2=hy_w2,
        hy_b2=hy_b2, hy_w3=hy_w3, hy_b3=hy_b3, hy_freq=hy_freq, hy_wout=hy_wout, hy_skip=hy_skip,
        rel_bias=rel_bias, rg_conv_w=rg_conv_w, rg_conv_b=rg_conv_b, rg_wa=rg_wa, rg_ba=rg_ba,
        rg_wx=rg_wx, rg_bx=rg_bx, rg_lambda=rg_lambda, w_gate=w_gate, b_gate=b_gate,
        w_proj_hy=w_proj_hy, w_proj_attn=w_proj_attn, w_proj_rg=w_proj_rg, w_out=w_out,
        ffn2_norm=ffn2_norm, ffn2_wg=ffn2_wg, ffn2_wu=ffn2_wu, ffn2_wd=ffn2_wd,
        final_norm=final_norm)
    return _forward(x, p)
```

```python
import functools
import math

import numpy as np
import jax
import jax.numpy as jnp
from jax import lax
from jax.experimental import pallas as pl
from jax.experimental.pallas import tpu as pltpu

F32 = jnp.float32
BF16 = jnp.bfloat16

D_MODEL = 1024
D_FF = 2816
DEPTH = 4
RMS_EPS = 1e-6

HY_WIDTH = 512
HY_ORDER = 2
HY_EMB = 33
HY_BANDS = 16
HY_HIDDEN = 64
HY_COLS = 3 * HY_WIDTH
HY_DECAY_TARGET = 1e-2
HY_FAST_DECAY = 0.3
HY_SLOW_DECAY = 1.5
HY_MOD_SHIFT = 0.05
HY_NORM_EPS = 1e-6

ATTN_GROUPS = ((128, 1), (512, 4), (2048, 16))
N_GROUPS = 3
HEADS = 8
HEAD_DIM = 64
ATTN_WIDTH = HEADS * HEAD_DIM
QKV_COLS = 3 * N_GROUPS * ATTN_WIDTH
BAND = 64
N_BUCKETS = 32
BUCKET_MAX_EXACT = 8
BUCKET_MAX_DIST = 1024
NEG_INF = -1e30

RG_WIDTH = 512
RG_BLOCKS = 8
RG_BLOCK = 64
RG_CONV = 4
RG_C = 8.0
RG_COLS = 2 * RG_WIDTH

N_BRANCH = 3

LANES = 128
VMEM_LIMIT_BYTES = 56 * 1024 * 1024

TOKEN_TILE = 512
Q_TILE = 128
K_WIN = Q_TILE + 2 * BAND
HY_CBLK = 256
RG_TT = 128
HALO_T = 8


def _const_spec(shape):
    nd = len(shape)
    return pl.BlockSpec(shape, lambda *_: (0,) * nd, pipeline_mode=pl.Buffered(1))


def _rms(x, g):
    ms = jnp.mean(x * x, axis=-1, keepdims=True)
    return x * lax.rsqrt(ms + RMS_EPS) * g


def _sigmoid(x):
    return 1.0 / (1.0 + jnp.exp(-x))


def _lane_slabs(x):
    return [x[:, s * LANES:(s + 1) * LANES] for s in range(x.shape[1] // LANES)]


def _ffn_body(h_ref, g_ref, wgu_ref, wd_ref, o_ref):
    x = h_ref[...]
    xb = _rms(x, g_ref[...]).astype(BF16)
    gu = jnp.dot(xb, wgu_ref[...], preferred_element_type=F32)
    gate = gu[:, :D_FF]
    up = gu[:, D_FF:]
    act = (gate * _sigmoid(gate)) * up
    out = jnp.dot(act.astype(BF16), wd_ref[...], preferred_element_type=F32)
    o_ref[...] = x + 0.5 * out


def _ffn(h, g, wgu, wd):
    n_tok = h.shape[0]
    return pl.pallas_call(
        _ffn_body,
        out_shape=jax.ShapeDtypeStruct(h.shape, F32),
        grid=(n_tok // TOKEN_TILE,),
        in_specs=[
            pl.BlockSpec((TOKEN_TILE, D_MODEL), lambda i: (i, 0)),
            _const_spec((1, D_MODEL)),
            _const_spec((D_MODEL, 2 * D_FF)),
            _const_spec((D_FF, D_MODEL)),
        ],
        out_specs=pl.BlockSpec((TOKEN_TILE, D_MODEL), lambda i: (i, 0)),
        compiler_params=pltpu.CompilerParams(
            dimension_semantics=("arbitrary",), vmem_limit_bytes=VMEM_LIMIT_BYTES),
        name="ffn",
    )(h, g, wgu, wd)


def _inproj_body(h_ref, g_ref, why_ref, wq0_ref, wq1_ref, wq2_ref, wrg_ref,
                 hy_ref, q0_ref, q1_ref, q2_ref, rg_ref, xs_ref):
    xn = _rms(h_ref[0], g_ref[...])
    xb = xn.astype(BF16)
    hy_ref[0] = jnp.dot(xb, why_ref[...], preferred_element_type=F32).astype(BF16)
    q0_ref[0, 0] = jnp.dot(xb, wq0_ref[...], preferred_element_type=F32).astype(BF16)
    rg_ref[0] = jnp.dot(xb, wrg_ref[...], preferred_element_type=F32).astype(BF16)

    for s, slab in enumerate(_lane_slabs(xn)):
        xs_ref[s] = slab

    def by_residue(d):
        n = TOKEN_TILE // d
        cols = [jnp.concatenate([xs_ref.at[s][pl.ds(r, n, stride=d), :] for r in range(d)], axis=0)
                for s in range(D_MODEL // LANES)]
        return jnp.concatenate(cols, axis=1).astype(BF16)

    for q_ref, w_ref, (_, d) in ((q1_ref, wq1_ref, ATTN_GROUPS[1]), (q2_ref, wq2_ref, ATTN_GROUPS[2])):
        out = jnp.dot(by_residue(d), w_ref[...], preferred_element_type=F32).astype(BF16)
        q_ref[0] = out.reshape(d, TOKEN_TILE // d, 3 * ATTN_WIDTH)


def _inproj(h3, g, w_hy, w_q, w_rg):
    nb, seq, _ = h3.shape
    dils = [d for _, d in ATTN_GROUPS]
    qshape = lambda d: jax.ShapeDtypeStruct((nb, d, seq // d, 3 * ATTN_WIDTH), BF16)
    qspec = lambda d: pl.BlockSpec((1, d, TOKEN_TILE // d, 3 * ATTN_WIDTH), lambda b, i: (b, 0, i, 0))
    return pl.pallas_call(
        _inproj_body,
        out_shape=(
            jax.ShapeDtypeStruct((nb, seq, HY_COLS), BF16),
            qshape(dils[0]), qshape(dils[1]), qshape(dils[2]),
            jax.ShapeDtypeStruct((nb, seq, RG_COLS), BF16),
        ),
        grid=(nb, seq // TOKEN_TILE),
        in_specs=[
            pl.BlockSpec((1, TOKEN_TILE, D_MODEL), lambda b, i: (b, i, 0)),
            _const_spec((1, D_MODEL)),
            _const_spec((D_MODEL, HY_COLS)),
            _const_spec((D_MODEL, 3 * ATTN_WIDTH)),
            _const_spec((D_MODEL, 3 * ATTN_WIDTH)),
            _const_spec((D_MODEL, 3 * ATTN_WIDTH)),
            _const_spec((D_MODEL, RG_COLS)),
        ],
        out_specs=(
            pl.BlockSpec((1, TOKEN_TILE, HY_COLS), lambda b, i: (b, i, 0)),
            qspec(dils[0]), qspec(dils[1]), qspec(dils[2]),
            pl.BlockSpec((1, TOKEN_TILE, RG_COLS), lambda b, i: (b, i, 0)),
        ),
        scratch_shapes=[pltpu.VMEM((D_MODEL // LANES, TOKEN_TILE, LANES), F32)],
        compiler_params=pltpu.CompilerParams(
            dimension_semantics=("arbitrary", "arbitrary"), vmem_limit_bytes=VMEM_LIMIT_BYTES),
        name="inproj",
    )(h3, g, w_hy, w_q[0], w_q[1], w_q[2], w_rg)


def _dft_mats(seq):
    n = jnp.arange(seq, dtype=jnp.int32)
    kn = (n[:, None] * n[None, :]) % (2 * seq)
    ang = kn.astype(F32) * (math.pi / seq)
    return jnp.cos(ang).astype(BF16), (-jnp.sin(ang)).astype(BF16)


def _hy_positions(seq):
    t = jnp.linspace(0.0, 1.0, seq, dtype=F32)[:, None]
    tr = jnp.arange(seq, dtype=F32)[:, None]
    wpos = 2.0 * math.pi * tr / seq
    fb = jnp.linspace(1e-4, HY_BANDS - 1, HY_BANDS, dtype=F32)[None, :]
    z = jnp.concatenate([t, jnp.cos(fb * wpos), -jnp.sin(fb * wpos)], axis=-1)
    return jnp.pad(z, ((0, 0), (0, LANES - HY_EMB)))


def _hy_filter_body(z_ref, w1_ref, b1_ref, w2_ref, b2_ref, w3_ref, b3_ref, fr_ref,
                    wf_ref, wb_ref, dl_ref, fre_ref, fim_ref,
                    hre_ref, him_ref, hny_ref):
    hi = lax.Precision.HIGHEST
    seq = z_ref.shape[0]
    fr = fr_ref[...]
    hdn = jnp.sin(fr * (jnp.dot(z_ref[...], w1_ref[...], precision=hi,
                                preferred_element_type=F32) + b1_ref[...]))
    hdn = jnp.sin(fr * (jnp.dot(hdn, w2_ref[...], precision=hi,
                                preferred_element_type=F32) + b2_ref[...]))
    hdn = jnp.sin(fr * (jnp.dot(hdn, w3_ref[...], precision=hi,
                                preferred_element_type=F32) + b3_ref[...]))
    row = lax.broadcasted_iota(jnp.int32, (seq, HY_WIDTH), 0)
    t = row.astype(F32) * (1.0 / (seq - 1))
    decay = jnp.exp(-t * dl_ref[...]) + HY_MOD_SHIFT
    kf = jnp.dot(hdn, wf_ref[0], precision=hi, preferred_element_type=F32) * decay
    kb = jnp.dot(hdn, wb_ref[0], precision=hi, preferred_element_type=F32) * decay
    kb = jnp.where(row == 0, 0.0, kb)
    norm = (jnp.sum(jnp.abs(kf), axis=0, keepdims=True)
            + jnp.sum(jnp.abs(kb), axis=0, keepdims=True) + HY_NORM_EPS)
    kp = (kf + kb) / norm
    km = (kf - kb) / norm
    scale = jnp.where(row == 0, 0.5 / seq, 1.0 / seq)
    sgn = (1 - 2 * (row & 1)).astype(F32)
    hre_ref[0] = (jnp.dot(fre_ref[...], kp.astype(BF16), preferred_element_type=F32)
                  * scale).astype(BF16)
    him_ref[0] = (jnp.dot(fim_ref[...], km.astype(BF16), preferred_element_type=F32)
                  * scale).astype(BF16)
    nyq = jnp.sum(kp * sgn, axis=0, keepdims=True) * (0.5 / seq)
    hny_ref[0] = jnp.broadcast_to(nyq, (8, HY_WIDTH))


def _hy_filter(z, w1, b1, w2, b2, w3, b3, freq, wout, deltas, fre, fim):
    seq = z.shape[0]
    wf = wout[:, :HY_ORDER * HY_WIDTH].reshape(HY_HIDDEN, HY_ORDER, HY_WIDTH).transpose(1, 0, 2)
    wb = wout[:, HY_ORDER * HY_WIDTH:].reshape(HY_HIDDEN, HY_ORDER, HY_WIDTH).transpose(1, 0, 2)
    w1p = jnp.pad(w1, ((0, LANES - HY_EMB), (0, 0)))
    row = lambda a: a.reshape(1, -1)
    return pl.pallas_call(
        _hy_filter_body,
        out_shape=(
            jax.ShapeDtypeStruct((HY_ORDER, seq, HY_WIDTH), BF16),
            jax.ShapeDtypeStruct((HY_ORDER, seq, HY_WIDTH), BF16),
            jax.ShapeDtypeStruct((HY_ORDER, 8, HY_WIDTH), F32),
        ),
        grid=(HY_ORDER,),
        in_specs=[
            _const_spec((seq, LANES)),
            _const_spec((LANES, HY_HIDDEN)), _const_spec((1, HY_HIDDEN)),
            _const_spec((HY_HIDDEN, HY_HIDDEN)), _const_spec((1, HY_HIDDEN)),
            _const_spec((HY_HIDDEN, HY_HIDDEN)), _const_spec((1, HY_HIDDEN)),
            _const_spec((1, HY_HIDDEN)),
            pl.BlockSpec((1, HY_HIDDEN, HY_WIDTH), lambda o: (o, 0, 0)),
            pl.BlockSpec((1, HY_HIDDEN, HY_WIDTH), lambda o: (o, 0, 0)),
            _const_spec((1, HY_WIDTH)),
            _const_spec((seq, seq)), _const_spec((seq, seq)),
        ],
        out_specs=(
            pl.BlockSpec((1, seq, HY_WIDTH), lambda o: (o, 0, 0)),
            pl.BlockSpec((1, seq, HY_WIDTH), lambda o: (o, 0, 0)),
            pl.BlockSpec((1, 8, HY_WIDTH), lambda o: (o, 0, 0)),
        ),
        compiler_params=pltpu.CompilerParams(
            dimension_semantics=("arbitrary",), vmem_limit_bytes=VMEM_LIMIT_BYTES),
        name="hy_filter",
    )(z, w1p, row(b1), w2, row(b2), w3, row(b3), row(freq), wf, wb, row(deltas), fre, fim)


def _hy_conv_body(v_ref, x1_ref, x2_ref, wv_ref, w1_ref, w2_ref, bv_ref, b1_ref, b2_ref,
                  fre_ref, fim_ref, hre_ref, him_ref, hny_ref, skip_ref, o_ref):
    seq = v_ref.shape[1]
    row = lax.broadcasted_iota(jnp.int32, (seq, HY_CBLK), 0)
    sgn = (1 - 2 * (row & 1)).astype(F32)

    def short_conv(x_ref, w_ref, b_ref):
        x = x_ref[0].astype(F32)
        w = w_ref[...]
        prev = jnp.where(row == 0, 0.0, pltpu.roll(x, 1, 0))
        nxt = jnp.where(row == seq - 1, 0.0, pltpu.roll(x, seq - 1, 0))
        return w[0:1] * prev + w[1:2] * x + w[2:3] * nxt + b_ref[...]

    def long_conv(z, o):
        zb = z.astype(BF16)
        zre = jnp.dot(fre_ref[...], zb, preferred_element_type=F32)
        zim = jnp.dot(fim_ref[...], zb, preferred_element_type=F32)
        hre = hre_ref[o].astype(F32)
        him = him_ref[o].astype(F32)
        pre = (zre * hre - zim * him).astype(BF16)
        pim = (zre * him + zim * hre).astype(BF16)
        znyq = jnp.sum(z * sgn, axis=0, keepdims=True)
        y = jnp.dot(fre_ref[...], pre, preferred_element_type=F32)
        y = y + jnp.dot(fim_ref[...], pim, preferred_element_type=F32)
        return y + sgn * (znyq * hny_ref[o][0:1])

    v = short_conv(v_ref, wv_ref, bv_ref)
    x1 = short_conv(x1_ref, w1_ref, b1_ref)
    x2 = short_conv(x2_ref, w2_ref, b2_ref)
    skip = skip_ref[...]
    z = x1 * (long_conv(v, 0) + skip[0:1] * v)
    z = x2 * (long_conv(z, 1) + skip[1:2] * z)
    o_ref[0] = z.astype(BF16)


def _hy_conv(u_hy, conv_w, conv_b, fre, fim, hre, him, hny, skip):
    nb, seq, _ = u_hy.shape
    ncb = HY_WIDTH // HY_CBLK
    data = lambda part: pl.BlockSpec((1, seq, HY_CBLK), lambda c, b: (b, 0, part * ncb + c))
    wspec = lambda part: pl.BlockSpec((3, HY_CBLK), lambda c, b: (0, part * ncb + c))
    bspec = lambda part: pl.BlockSpec((1, HY_CBLK), lambda c, b: (0, part * ncb + c))
    spec_once = lambda rows: pl.BlockSpec(
        (HY_ORDER, rows, HY_CBLK), lambda c, b: (0, 0, c), pipeline_mode=pl.Buffered(1))
    return pl.pallas_call(
        _hy_conv_body,
        out_shape=jax.ShapeDtypeStruct((nb, seq, HY_WIDTH), BF16),
        grid=(ncb, nb),
        in_specs=[
            data(0), data(1), data(2),
            wspec(0), wspec(1), wspec(2),
            bspec(0), bspec(1), bspec(2),
            _const_spec((seq, seq)), _const_spec((seq, seq)),
            spec_once(seq), spec_once(seq),
            pl.BlockSpec((HY_ORDER, 8, HY_CBLK), lambda c, b: (0, 0, c)),
            pl.BlockSpec((HY_ORDER, HY_CBLK), lambda c, b: (0, c)),
        ],
        out_specs=pl.BlockSpec((1, seq, HY_CBLK), lambda c, b: (b, 0, c)),
        compiler_params=pltpu.CompilerParams(
            dimension_semantics=("arbitrary", "arbitrary"), vmem_limit_bytes=VMEM_LIMIT_BYTES),
        name="hy_conv",
    )(u_hy, u_hy, u_hy, conv_w, conv_w, conv_w, conv_b, conv_b, conv_b,
      fre, fim, hre, him, hny, skip)


def _bucket_matrix(dilation):
    qi = np.arange(Q_TILE, dtype=np.int64)[:, None]
    kj = np.arange(K_WIN, dtype=np.int64)[None, :]
    delta = kj - BAND - qi
    rel = delta * dilation
    half = N_BUCKETS // 2
    n = np.abs(rel)
    nf = np.maximum(n, 1).astype(np.float32)
    large = BUCKET_MAX_EXACT + (
        np.log(nf / np.float32(BUCKET_MAX_EXACT)) / np.float32(math.log(BUCKET_MAX_DIST / BUCKET_MAX_EXACT))
        * np.float32(half - BUCKET_MAX_EXACT)).astype(np.int32)
    large = np.minimum(large, half - 1)
    bucket = np.where(rel > 0, half, 0) + np.where(n < BUCKET_MAX_EXACT, n, large)
    return np.where(np.abs(delta) <= BAND, bucket, -1).astype(np.int32)


def _attn_body(group, ls, tbl_ref, bkt_ref, q_ref, k_ref, v_ref, o_ref, l_ref,
               bias_ref, kpad_ref, vpad_ref):
    first = (pl.program_id(0) == 0) & (pl.program_id(1) == 0)

    @pl.when(first)
    def _():
        bkt = bkt_ref[...]
        for h in range(HEADS):
            acc = jnp.full((Q_TILE, K_WIN), NEG_INF, F32)
            for bk in range(N_BUCKETS):
                acc = jnp.where(bkt == bk, tbl_ref[bk, group * HEADS + h], acc)
            bias_ref[h] = acc
        zeros = jnp.zeros((BAND, ATTN_WIDTH), BF16)
        kpad_ref[0:BAND] = zeros
        kpad_ref[BAND + ls:BAND + ls + BAND] = zeros
        vpad_ref[0:BAND] = zeros
        vpad_ref[BAND + ls:BAND + ls + BAND] = zeros

    kpad_ref[BAND:BAND + ls] = k_ref[0, 0]
    vpad_ref[BAND:BAND + ls] = v_ref[0, 0]

    lane = lax.broadcasted_iota(jnp.int32, (Q_TILE, LANES), 1)
    low_half = lane < HEAD_DIM
    kcol = lax.broadcasted_iota(jnp.int32, (1, K_WIN), 1)
    scale = HEAD_DIM ** -0.5

    def tile(t, carry):
        j0 = pl.multiple_of(t * Q_TILE, Q_TILE)
        kidx = j0 - BAND + kcol
        kmask = jnp.where((kidx >= 0) & (kidx < ls), 0.0, NEG_INF)
        lse_tile = jnp.zeros((Q_TILE, LANES), F32)
        for hp in range(HEADS // 2):
            cols = slice(hp * LANES, (hp + 1) * LANES)
            q2 = q_ref[0, 0, pl.ds(j0, Q_TILE), cols] * scale
            kw = kpad_ref[pl.ds(j0, K_WIN), cols]
            vw = vpad_ref[pl.ds(j0, K_WIN), cols]
            outs = []
            for half in range(2):
                h = 2 * hp + half
                qh = jnp.where(low_half if half == 0 else ~low_half, q2, 0.0).astype(BF16)
                s = lax.dot_general(qh, kw, (((1,), (1,)), ((), ())), preferred_element_type=F32)
                s = s + bias_ref[h] + kmask
                m = jnp.max(s, axis=-1, keepdims=True)
                p = jnp.exp(s - m)
                den = jnp.sum(p, axis=-1, keepdims=True)
                pv = jnp.dot(p.astype(BF16), vw, preferred_element_type=F32)
                outs.append(pv / den)
                lse_tile = jnp.where(lane == h, m + jnp.log(den), lse_tile)
            o_ref[0, 0, pl.ds(j0, Q_TILE), cols] = jnp.where(low_half, outs[0], outs[1]).astype(BF16)
        l_ref[0, 0, pl.ds(j0, Q_TILE), :] = lse_tile
        return carry

    lax.fori_loop(0, ls // Q_TILE, tile, 0)


def _attn_group(qkv_g, tbl, group, dilation):
    nb, _, ls, _ = qkv_g.shape
    part = lambda which: pl.BlockSpec((1, 1, ls, ATTN_WIDTH), lambda b, r: (b, r, 0, which))
    bkt = jnp.asarray(_bucket_matrix(dilation))
    return pl.pallas_call(
        functools.partial(_attn_body, group, ls),
        out_shape=(
            jax.ShapeDtypeStruct((nb, dilation, ls, ATTN_WIDTH), BF16),
            jax.ShapeDtypeStruct((nb, dilation, ls, LANES), F32),
        ),
        grid=(nb, dilation),
        in_specs=[
            pl.BlockSpec(memory_space=pltpu.SMEM),
            _const_spec((Q_TILE, K_WIN)),
            part(0), part(1), part(2),
        ],
        out_specs=(
            pl.BlockSpec((1, 1, ls, ATTN_WIDTH), lambda b, r: (b, r, 0, 0)),
            pl.BlockSpec((1, 1, ls, LANES), lambda b, r: (b, r, 0, 0)),
        ),
        scratch_shapes=[
            pltpu.VMEM((HEADS, Q_TILE, K_WIN), F32),
            pltpu.VMEM((ls + 2 * BAND, ATTN_WIDTH), BF16),
            pltpu.VMEM((ls + 2 * BAND, ATTN_WIDTH), BF16),
        ],
        compiler_params=pltpu.CompilerParams(
            dimension_semantics=("arbitrary", "arbitrary"), vmem_limit_bytes=VMEM_LIMIT_BYTES),
        name=f"attn_g{group}",
    )(tbl, bkt, qkv_g, qkv_g, qkv_g)


def _rglru_body(nt, cw_ref, cb_ref, wf_ref, wb_ref, bf_ref, bb_ref, lam_ref,
                xf_ref, pf_ref, nf_ref, xb_ref, pb_ref, nb_ref,
                hf_ref, hb_ref, xw_ref, a_ref, u_ref, hs_ref, cf_ref, cbk_ref):
    i = pl.program_id(0)
    nbat, tt, width = xf_ref.shape
    rows = tt * nbat
    n_slab = width // LANES
    halo_rows = HALO_T * nbat

    @pl.when(i == 0)
    def _():
        cf_ref[...] = jnp.zeros_like(cf_ref)
        cbk_ref[...] = jnp.zeros_like(cbk_ref)

    def to_time_major(src_ref, n_t, row0, keep):
        for b in range(nbat):
            x = src_ref[b].astype(F32)
            x = x if keep is None else jnp.where(keep, x, 0.0)
            for s, slab in enumerate(_lane_slabs(x)):
                xw_ref.at[s][pl.ds(row0 + b, n_t, stride=nbat), :] = slab

    def gates(x_ref, p_ref, n_ref, tile_idx, w_ref, b_ref, lam):
        to_time_major(p_ref, HALO_T, 0, tile_idx > 0)
        to_time_major(x_ref, tt, halo_rows, None)
        to_time_major(n_ref, HALO_T, halo_rows + rows, tile_idx < nt - 1)
        cw = cw_ref[...]

        def tap(k):
            start = halo_rows + (k - 2) * nbat
            return jnp.concatenate([xw_ref[s, start:start + rows, :] for s in range(n_slab)], axis=1)

        xc = cw[0:1] * tap(0) + cw[1:2] * tap(1) + cw[2:3] * tap(2) + cw[3:4] * tap(3) + cb_ref[...]
        g = jnp.dot(xc.astype(BF16), w_ref[...], preferred_element_type=F32) + b_ref[...]
        r = _sigmoid(g[:, :width])
        gi = _sigmoid(g[:, width:])
        softplus = jnp.maximum(-lam, 0.0) + jnp.log(1.0 + jnp.exp(-jnp.abs(lam)))
        log_a = (-RG_C * softplus) * r
        a = jnp.exp(log_a)
        a_ref[...] = a
        u_ref[...] = jnp.sqrt(1.0 - a * a) * (gi * xc)

    def step(t, h):
        r0 = pl.multiple_of(t * nbat, nbat)
        h = a_ref[pl.ds(r0, nbat), :] * h + u_ref[pl.ds(r0, nbat), :]
        for s, slab in enumerate(_lane_slabs(h)):
            hs_ref[s, pl.ds(r0, nbat), :] = slab
        return h

    def to_batch_major(dst_ref):
        for b in range(nbat):
            dst_ref[b] = jnp.concatenate(
                [hs_ref.at[s][pl.ds(b, tt, stride=nbat), :] for s in range(n_slab)], axis=1).astype(BF16)

    gates(xf_ref, pf_ref, nf_ref, i, wf_ref, bf_ref, lam_ref[0:1])
    cf_ref[...] = lax.fori_loop(0, tt, step, cf_ref[...], unroll=8)
    to_batch_major(hf_ref)

    gates(xb_ref, pb_ref, nb_ref, nt - 1 - i, wb_ref, bb_ref, lam_ref[1:2])
    cbk_ref[...] = lax.fori_loop(0, tt, lambda s, h: step(tt - 1 - s, h), cbk_ref[...], unroll=8)
    to_batch_major(hb_ref)


def _rglru(rg, conv_w, conv_b, w_f, w_b, b_f, b_b, lam):
    nbat, seq, _ = rg.shape
    nt = seq // RG_TT
    hpt = RG_TT // HALO_T
    nh = seq // HALO_T
    cur = lambda f: pl.BlockSpec((nbat, RG_TT, RG_WIDTH), lambda i: (0, f(i), 0))
    prv = lambda f: pl.BlockSpec(
        (nbat, HALO_T, RG_WIDTH), lambda i: (0, jnp.maximum(f(i) * hpt - 1, 0), 0))
    nxt = lambda f: pl.BlockSpec(
        (nbat, HALO_T, RG_WIDTH), lambda i: (0, jnp.minimum((f(i) + 1) * hpt, nh - 1), 0))
    fw = lambda i: i
    bw = lambda i: nt - 1 - i
    n_slab = RG_WIDTH // LANES
    return pl.pallas_call(
        functools.partial(_rglru_body, nt),
        out_shape=(
            jax.ShapeDtypeStruct((nbat, seq, RG_WIDTH), BF16),
            jax.ShapeDtypeStruct((nbat, seq, RG_WIDTH), BF16),
        ),
        grid=(nt,),
        in_specs=[
            _const_spec((RG_CONV, RG_WIDTH)), _const_spec((1, RG_WIDTH)),
            _const_spec((RG_WIDTH, 2 * RG_WIDTH)), _const_spec((RG_WIDTH, 2 * RG_WIDTH)),
            _const_spec((1, 2 * RG_WIDTH)), _const_spec((1, 2 * RG_WIDTH)),
            _const_spec((2, RG_WIDTH)),
            cur(fw), prv(fw), nxt(fw), cur(bw), prv(bw), nxt(bw),
        ],
        out_specs=(cur(fw), cur(bw)),
        scratch_shapes=[
            pltpu.VMEM((n_slab, (RG_TT + 2 * HALO_T) * nbat, LANES), F32),
            pltpu.VMEM((RG_TT * nbat, RG_WIDTH), F32),
            pltpu.VMEM((RG_TT * nbat, RG_WIDTH), F32),
            pltpu.VMEM((n_slab, RG_TT * nbat, LANES), F32),
            pltpu.VMEM((nbat, RG_WIDTH), F32),
            pltpu.VMEM((nbat, RG_WIDTH), F32),
        ],
        compiler_params=pltpu.CompilerParams(
            dimension_semantics=("arbitrary",), vmem_limit_bytes=VMEM_LIMIT_BYTES),
        name="rglru",
    )(conv_w, conv_b, w_f, w_b, b_f, b_b, lam, rg, rg, rg, rg, rg, rg)


def _block_diag(w):
    eye = jnp.eye(RG_BLOCKS, dtype=w.dtype)
    return jnp.einsum("hij,hk->hikj", w, eye).reshape(RG_WIDTH, RG_WIDTH)


def _merge_body(h_ref, g_ref, wg_ref, bg_ref, ya_ref, o0_ref, o1_ref, o2_ref,
                l0_ref, l1_ref, l2_ref, hf_ref, hb_ref, gate_ref, exp_ref,
                wphy_ref, wpat_ref, wprg_ref, wout_ref, out_ref, os_ref, ls_ref):
    x = h_ref[0]
    xb = _rms(x, g_ref[...]).astype(BF16)
    gates = _sigmoid(jnp.dot(xb, wg_ref[...], preferred_element_type=F32) + bg_ref[...])

    def token_order(o_ref, l_ref):
        d, n = o_ref.shape[1], o_ref.shape[2]
        if d == 1:
            return o_ref[0, 0].astype(F32), l_ref[0, 0]
        for r in range(d):
            for s, slab in enumerate(_lane_slabs(o_ref[0, r].astype(F32))):
                os_ref.at[s][pl.ds(r, n, stride=d), :] = slab
            ls_ref[pl.ds(r, n, stride=d), :] = l_ref[0, r]
        o = jnp.concatenate([os_ref[s] for s in range(ATTN_WIDTH // LANES)], axis=1)
        return o, ls_ref[...]

    o0, l0 = token_order(o0_ref, l0_ref)
    o1, l1 = token_order(o1_ref, l1_ref)
    o2, l2 = token_order(o2_ref, l2_ref)

    m = jnp.maximum(jnp.maximum(l0, l1), l2)
    e0, e1, e2 = jnp.exp(l0 - m), jnp.exp(l1 - m), jnp.exp(l2 - m)
    inv = 1.0 / (e0 + e1 + e2)

    def widen(w):
        hi = w.astype(BF16)
        lo = (w - hi.astype(F32)).astype(BF16)
        return (jnp.dot(hi, exp_ref[...], preferred_element_type=F32)
                + jnp.dot(lo, exp_ref[...], preferred_element_type=F32))

    y_b = widen(e0 * inv) * o0 + widen(e1 * inv) * o1 + widen(e2 * inv) * o2

    gt = gate_ref[0].astype(F32)
    gelu = 0.5 * gt * (1.0 + jnp.tanh(math.sqrt(2.0 / math.pi) * (gt + 0.044715 * (gt * gt * gt))))
    y_c = (hf_ref[0].astype(F32) + hb_ref[0].astype(F32)) * gelu

    merged = (gates[:, :D_MODEL]
              * jnp.dot(ya_ref[0], wphy_ref[...], preferred_element_type=F32)
              + gates[:, D_MODEL:2 * D_MODEL]
              * jnp.dot(y_b.astype(BF16), wpat_ref[...], preferred_element_type=F32)
              + gates[:, 2 * D_MODEL:]
              * jnp.dot(y_c.astype(BF16), wprg_ref[...], preferred_element_type=F32))
    out_ref[0] = x + jnp.dot(merged.astype(BF16), wout_ref[...], preferred_element_type=F32)


def _merge(h3, g, w_gate, b_gate, y_a, attn, hf, hb, rg, expand, wp_hy, wp_attn, wp_rg, w_out):
    nb, seq, _ = h3.shape
    tok = lambda width, col=0: pl.BlockSpec((1, TOKEN_TILE, width), lambda b, i: (b, i, col))
    res = lambda d, width: pl.BlockSpec((1, d, TOKEN_TILE // d, width), lambda b, i: (b, 0, i, 0))
    (o0, l0), (o1, l1), (o2, l2) = attn
    dils = [d for _, d in ATTN_GROUPS]
    return pl.pallas_call(
        _merge_body,
        out_shape=jax.ShapeDtypeStruct(h3.shape, F32),
        grid=(nb, seq // TOKEN_TILE),
        in_specs=[
            tok(D_MODEL), _const_spec((1, D_MODEL)),
            _const_spec((D_MODEL, N_BRANCH * D_MODEL)), _const_spec((1, N_BRANCH * D_MODEL)),
            tok(HY_WIDTH),
            res(dils[0], ATTN_WIDTH), res(dils[1], ATTN_WIDTH), res(dils[2], ATTN_WIDTH),
            res(dils[0], LANES), res(dils[1], LANES), res(dils[2], LANES),
            tok(RG_WIDTH), tok(RG_WIDTH), tok(RG_WIDTH, 1),
            _const_spec((LANES, ATTN_WIDTH)),
            _const_spec((HY_WIDTH, D_MODEL)), _const_spec((ATTN_WIDTH, D_MODEL)),
            _const_spec((RG_WIDTH, D_MODEL)), _const_spec((D_MODEL, D_MODEL)),
        ],
        out_specs=tok(D_MODEL),
        scratch_shapes=[
            pltpu.VMEM((ATTN_WIDTH // LANES, TOKEN_TILE, LANES), F32),
            pltpu.VMEM((TOKEN_TILE, LANES), F32),
        ],
        compiler_params=pltpu.CompilerParams(
            dimension_semantics=("arbitrary", "arbitrary"), vmem_limit_bytes=VMEM_LIMIT_BYTES),
        name="merge",
    )(h3, g, w_gate, b_gate, y_a, o0, o1, o2, l0, l1, l2, hf, hb, rg, expand,
      wp_hy, wp_attn, wp_rg, w_out)


def _final_norm_body(h_ref, g_ref, o_ref):
    o_ref[...] = _rms(h_ref[...], g_ref[...])


def _final_norm(h, g):
    n_tok = h.shape[0]
    return pl.pallas_call(
        _final_norm_body,
        out_shape=jax.ShapeDtypeStruct(h.shape, F32),
        grid=(n_tok // TOKEN_TILE,),
        in_specs=[pl.BlockSpec((TOKEN_TILE, D_MODEL), lambda i: (i, 0)), _const_spec((1, D_MODEL))],
        out_specs=pl.BlockSpec((TOKEN_TILE, D_MODEL), lambda i: (i, 0)),
        compiler_params=pltpu.CompilerParams(dimension_semantics=("arbitrary",)),
        name="final_norm",
    )(h, g)


def _mixer(h3, l, p, consts):
    fre, fim, zpos, deltas, expand = consts
    w_in = p["w_in"][l].astype(BF16)
    w_qkv = w_in[:, HY_COLS:HY_COLS + QKV_COLS].reshape(D_MODEL, 3, N_GROUPS, ATTN_WIDTH)
    w_q = [w_qkv[:, :, g, :].reshape(D_MODEL, 3 * ATTN_WIDTH) for g in range(N_GROUPS)]
    hy, q0, q1, q2, rg = _inproj(
        h3, p["mix_norm"][l].reshape(1, -1), w_in[:, :HY_COLS], w_q, w_in[:, HY_COLS + QKV_COLS:])

    hre, him, hny = _hy_filter(
        zpos, p["hy_w1"][l], p["hy_b1"][l], p["hy_w2"][l], p["hy_b2"][l], p["hy_w3"][l],
        p["hy_b3"][l], p["hy_freq"][l], p["hy_wout"][l], deltas, fre, fim)
    y_a = _hy_conv(hy, p["hy_conv_w"][l], p["hy_conv_b"][l].reshape(1, -1),
                   fre, fim, hre, him, hny, p["hy_skip"][l])

    attn = [_attn_group(q, p["rel_bias"], g, dil)
            for g, (q, (_, dil)) in enumerate(zip((q0, q1, q2), ATTN_GROUPS))]

    rg_w = lambda d: jnp.concatenate(
        [_block_diag(p["rg_wa"][l, d]), _block_diag(p["rg_wx"][l, d])], axis=1).astype(BF16)
    rg_b = lambda d: jnp.concatenate([p["rg_ba"][l, d], p["rg_bx"][l, d]]).reshape(1, -1)
    hf, hb = _rglru(rg, p["rg_conv_w"][l], p["rg_conv_b"][l].reshape(1, -1),
                    rg_w(0), rg_w(1), rg_b(0), rg_b(1), p["rg_lambda"][l])

    return _merge(
        h3, p["mix_norm"][l].reshape(1, -1), p["w_gate"][l].astype(BF16),
        p["b_gate"][l].reshape(1, -1), y_a, attn, hf, hb, rg, expand,
        p["w_proj_hy"][l].astype(BF16), p["w_proj_attn"][l].astype(BF16),
        p["w_proj_rg"][l].astype(BF16), p["w_out"][l].astype(BF16))


def _forward(x, p):
    nb, seq, _ = x.shape
    fre, fim = _dft_mats(seq)
    deltas = jnp.abs(jnp.linspace(math.log(HY_DECAY_TARGET) / HY_FAST_DECAY,
                                  math.log(HY_DECAY_TARGET) / HY_SLOW_DECAY, HY_WIDTH, dtype=F32))
    head_of_lane = jnp.arange(ATTN_WIDTH, dtype=jnp.int32) // HEAD_DIM
    expand = (jnp.arange(LANES, dtype=jnp.int32)[:, None] == head_of_lane[None, :]).astype(BF16)
    consts = (fre, fim, _hy_positions(seq), deltas, expand)

    def ffn(h3, norm, wg, wu, wd, l):
        wgu = jnp.concatenate([wg[l], wu[l]], axis=1).astype(BF16)
        out = _ffn(h3.reshape(nb * seq, D_MODEL), norm[l].reshape(1, -1), wgu, wd[l].astype(BF16))
        return out.reshape(nb, seq, D_MODEL)

    h = x
    for l in range(DEPTH):
        h = ffn(h, p["ffn1_norm"], p["ffn1_wg"], p["ffn1_wu"], p["ffn1_wd"], l)
        h = _mixer(h, l, p, consts)
        h = ffn(h, p["ffn2_norm"], p["ffn2_wg"], p["ffn2_wu"], p["ffn2_wd"], l)
    out = _final_norm(h.reshape(nb * seq, D_MODEL), p["final_norm"].reshape(1, -1))
    return out.reshape(nb, seq, D_MODEL)


def kernel(x, ffn1_norm, ffn1_wg, ffn1_wu, ffn1_wd, mix_norm, w_in, hy_conv_w, hy_conv_b, hy_w1, hy_b1, hy_w2, hy_b2, hy_w3, hy_b3, hy_freq, hy_wout, hy_skip, rel_bias, rg_conv_w, rg_conv_b, rg_wa, rg_ba, rg_wx, rg_bx, rg_lambda, w_gate, b_gate, w_proj_hy, w_proj_attn, w_proj_rg, w_out, ffn2_norm, ffn2_wg, ffn2_wu, ffn2_wd, final_norm):
    p = dict(
        ffn1_norm=ffn1_norm, ffn1_wg=ffn1_wg, ffn1_wu=ffn1_wu, ffn1_wd=ffn1_wd, mix_norm=mix_norm,
        w_in=w_in, hy_conv_w=hy_conv_w, hy_conv_b=hy_conv_b, hy_w1=hy_w1, hy_b1=hy_b1, hy_w2=hy_w2,
        hy_b2=hy_b2, hy_w3=hy_w3, hy_b3=hy_b3, hy_freq=hy_freq, hy_wout=hy_wout, hy_skip=hy_skip,
        rel_bias=rel_bias, rg_conv_w=rg_conv_w, rg_conv_b=rg_conv_b, rg_wa=rg_wa, rg_ba=rg_ba,
        rg_wx=rg_wx, rg_bx=rg_bx, rg_lambda=rg_lambda, w_gate=w_gate, b_gate=b_gate,
        w_proj_hy=w_proj_hy, w_proj_attn=w_proj_attn, w_proj_rg=w_proj_rg, w_out=w_out,
        ffn2_norm=ffn2_norm, ffn2_wg=ffn2_wg, ffn2_wu=ffn2_wu, ffn2_wd=ffn2_wd,
        final_norm=final_norm)
    return _forward(x, p)
```

```python
import functools
import math

import numpy as np
import jax
import jax.numpy as jnp
from jax import lax
from jax.experimental import pallas as pl
from jax.experimental.pallas import tpu as pltpu

F32 = jnp.float32
BF16 = jnp.bfloat16

D_MODEL = 1024
D_FF = 2816
DEPTH = 4
RMS_EPS = 1e-6

HY_WIDTH = 512
HY_ORDER = 2
HY_EMB = 33
HY_BANDS = 16
HY_HIDDEN = 64
HY_COLS = 3 * HY_WIDTH
HY_DECAY_TARGET = 1e-2
HY_FAST_DECAY = 0.3
HY_SLOW_DECAY = 1.5
HY_MOD_SHIFT = 0.05
HY_NORM_EPS = 1e-6

ATTN_GROUPS = ((128, 1), (512, 4), (2048, 16))
N_GROUPS = 3
HEADS = 8
HEAD_DIM = 64
ATTN_WIDTH = HEADS * HEAD_DIM
QKV_COLS = 3 * N_GROUPS * ATTN_WIDTH
BAND = 64
N_BUCKETS = 32
BUCKET_MAX_EXACT = 8
BUCKET_MAX_DIST = 1024
NEG_INF = -1e30

RG_WIDTH = 512
RG_BLOCKS = 8
RG_BLOCK = 64
RG_CONV = 4
RG_C = 8.0
RG_COLS = 2 * RG_WIDTH

N_BRANCH = 3

LANES = 128
VMEM_LIMIT_BYTES = 56 * 1024 * 1024

TOKEN_TILE = 512
Q_TILE = 128
K_WIN = Q_TILE + 2 * BAND
ATTN_UNROLL = 4
HY_CBLK = 256
HY_RBLK = 512
RG_TT = 128
HALO_T = 8


def _const_spec(shape):
    nd = len(shape)
    return pl.BlockSpec(shape, lambda *_: (0,) * nd, pipeline_mode=pl.Buffered(1))


def _rms(x, g):
    ms = jnp.mean(x * x, axis=-1, keepdims=True)
    return x * lax.rsqrt(ms + RMS_EPS) * g


def _sigmoid(x):
    return 1.0 / (1.0 + jnp.exp(-x))


def _lane_slabs(x):
    return [x[:, s * LANES:(s + 1) * LANES] for s in range(x.shape[1] // LANES)]


def _ffn_body(h_ref, g_ref, wgu_ref, wd_ref, o_ref):
    x = h_ref[...]
    xb = _rms(x, g_ref[...]).astype(BF16)
    gu = jnp.dot(xb, wgu_ref[...], preferred_element_type=F32)
    gate = gu[:, :D_FF]
    up = gu[:, D_FF:]
    act = (gate * _sigmoid(gate)) * up
    out = jnp.dot(act.astype(BF16), wd_ref[...], preferred_element_type=F32)
    o_ref[...] = x + 0.5 * out


def _ffn(h, g, wgu, wd):
    n_tok = h.shape[0]
    return pl.pallas_call(
        _ffn_body,
        out_shape=jax.ShapeDtypeStruct(h.shape, F32),
        grid=(n_tok // TOKEN_TILE,),
        in_specs=[
            pl.BlockSpec((TOKEN_TILE, D_MODEL), lambda i: (i, 0)),
            _const_spec((1, D_MODEL)),
            _const_spec((D_MODEL, 2 * D_FF)),
            _const_spec((D_FF, D_MODEL)),
        ],
        out_specs=pl.BlockSpec((TOKEN_TILE, D_MODEL), lambda i: (i, 0)),
        compiler_params=pltpu.CompilerParams(
            dimension_semantics=("arbitrary",), vmem_limit_bytes=VMEM_LIMIT_BYTES),
        name="ffn",
    )(h, g, wgu, wd)


def _inproj_body(h_ref, g_ref, why_ref, wq0_ref, wq1_ref, wq2_ref, wrg_ref,
                 hy_ref, q0_ref, q1_ref, q2_ref, rg_ref, xs_ref):
    xn = _rms(h_ref[0], g_ref[...])
    xb = xn.astype(BF16)
    hy_ref[0] = jnp.dot(xb, why_ref[...], preferred_element_type=F32).astype(BF16)
    q0_ref[0, 0] = jnp.dot(xb, wq0_ref[...], preferred_element_type=F32).astype(BF16)
    rg_ref[0] = jnp.dot(xb, wrg_ref[...], preferred_element_type=F32).astype(BF16)

    for s, slab in enumerate(_lane_slabs(xn)):
        xs_ref[s] = slab

    def by_residue(d):
        n = TOKEN_TILE // d
        cols = [jnp.concatenate([xs_ref.at[s][pl.ds(r, n, stride=d), :] for r in range(d)], axis=0)
                for s in range(D_MODEL // LANES)]
        return jnp.concatenate(cols, axis=1).astype(BF16)

    for q_ref, w_ref, (_, d) in ((q1_ref, wq1_ref, ATTN_GROUPS[1]), (q2_ref, wq2_ref, ATTN_GROUPS[2])):
        out = jnp.dot(by_residue(d), w_ref[...], preferred_element_type=F32).astype(BF16)
        q_ref[0] = out.reshape(d, TOKEN_TILE // d, 3 * ATTN_WIDTH)


def _inproj(h3, g, w_hy, w_q, w_rg):
    nb, seq, _ = h3.shape
    dils = [d for _, d in ATTN_GROUPS]
    qshape = lambda d: jax.ShapeDtypeStruct((nb, d, seq // d, 3 * ATTN_WIDTH), BF16)
    qspec = lambda d: pl.BlockSpec((1, d, TOKEN_TILE // d, 3 * ATTN_WIDTH), lambda b, i: (b, 0, i, 0))
    return pl.pallas_call(
        _inproj_body,
        out_shape=(
            jax.ShapeDtypeStruct((nb, seq, HY_COLS), BF16),
            qshape(dils[0]), qshape(dils[1]), qshape(dils[2]),
            jax.ShapeDtypeStruct((nb, seq, RG_COLS), BF16),
        ),
        grid=(nb, seq // TOKEN_TILE),
        in_specs=[
            pl.BlockSpec((1, TOKEN_TILE, D_MODEL), lambda b, i: (b, i, 0)),
            _const_spec((1, D_MODEL)),
            _const_spec((D_MODEL, HY_COLS)),
            _const_spec((D_MODEL, 3 * ATTN_WIDTH)),
            _const_spec((D_MODEL, 3 * ATTN_WIDTH)),
            _const_spec((D_MODEL, 3 * ATTN_WIDTH)),
            _const_spec((D_MODEL, RG_COLS)),
        ],
        out_specs=(
            pl.BlockSpec((1, TOKEN_TILE, HY_COLS), lambda b, i: (b, i, 0)),
            qspec(dils[0]), qspec(dils[1]), qspec(dils[2]),
            pl.BlockSpec((1, TOKEN_TILE, RG_COLS), lambda b, i: (b, i, 0)),
        ),
        scratch_shapes=[pltpu.VMEM((D_MODEL // LANES, TOKEN_TILE, LANES), F32)],
        compiler_params=pltpu.CompilerParams(
            dimension_semantics=("arbitrary", "arbitrary"), vmem_limit_bytes=VMEM_LIMIT_BYTES),
        name="inproj",
    )(h3, g, w_hy, w_q[0], w_q[1], w_q[2], w_rg)


def _dft_mats(seq):
    n = jnp.arange(seq, dtype=jnp.int32)
    kn = (n[:, None] * n[None, :]) % (2 * seq)
    ang = kn.astype(F32) * (math.pi / seq)
    return jnp.cos(ang).astype(BF16), (-jnp.sin(ang)).astype(BF16)


def _hy_positions(seq):
    t = jnp.linspace(0.0, 1.0, seq, dtype=F32)[:, None]
    tr = jnp.arange(seq, dtype=F32)[:, None]
    wpos = 2.0 * math.pi * tr / seq
    fb = jnp.linspace(1e-4, HY_BANDS - 1, HY_BANDS, dtype=F32)[None, :]
    z = jnp.concatenate([t, jnp.cos(fb * wpos), -jnp.sin(fb * wpos)], axis=-1)
    return jnp.pad(z, ((0, 0), (0, LANES - HY_EMB)))


def _hy_filter_body(z_ref, w1_ref, b1_ref, w2_ref, b2_ref, w3_ref, b3_ref, fr_ref,
                    wf_ref, wb_ref, dl_ref, fre_ref, fim_ref,
                    hre_ref, him_ref, hny_ref):
    hi = lax.Precision.HIGHEST
    seq = z_ref.shape[0]
    fr = fr_ref[...]
    hdn = jnp.sin(fr * (jnp.dot(z_ref[...], w1_ref[...], precision=hi,
                                preferred_element_type=F32) + b1_ref[...]))
    hdn = jnp.sin(fr * (jnp.dot(hdn, w2_ref[...], precision=hi,
                                preferred_element_type=F32) + b2_ref[...]))
    hdn = jnp.sin(fr * (jnp.dot(hdn, w3_ref[...], precision=hi,
                                preferred_element_type=F32) + b3_ref[...]))
    row = lax.broadcasted_iota(jnp.int32, (seq, HY_WIDTH), 0)
    t = row.astype(F32) * (1.0 / (seq - 1))
    decay = jnp.exp(-t * dl_ref[...]) + HY_MOD_SHIFT
    kf = jnp.dot(hdn, wf_ref[0], precision=hi, preferred_element_type=F32) * decay
    kb = jnp.dot(hdn, wb_ref[0], precision=hi, preferred_element_type=F32) * decay
    kb = jnp.where(row == 0, 0.0, kb)
    norm = (jnp.sum(jnp.abs(kf), axis=0, keepdims=True)
            + jnp.sum(jnp.abs(kb), axis=0, keepdims=True) + HY_NORM_EPS)
    kp = (kf + kb) / norm
    km = (kf - kb) / norm
    scale = jnp.where(row == 0, 0.5 / seq, 1.0 / seq)
    sgn = (1 - 2 * (row & 1)).astype(F32)
    hre_ref[0] = (jnp.dot(fre_ref[...], kp.astype(BF16), preferred_element_type=F32)
                  * scale).astype(BF16)
    him_ref[0] = (jnp.dot(fim_ref[...], km.astype(BF16), preferred_element_type=F32)
                  * scale).astype(BF16)
    nyq = jnp.sum(kp * sgn, axis=0, keepdims=True) * (0.5 / seq)
    hny_ref[0] = jnp.broadcast_to(nyq, (8, HY_WIDTH))


def _hy_filter(z, w1, b1, w2, b2, w3, b3, freq, wout, deltas, fre, fim):
    seq = z.shape[0]
    wf = wout[:, :HY_ORDER * HY_WIDTH].reshape(HY_HIDDEN, HY_ORDER, HY_WIDTH).transpose(1, 0, 2)
    wb = wout[:, HY_ORDER * HY_WIDTH:].reshape(HY_HIDDEN, HY_ORDER, HY_WIDTH).transpose(1, 0, 2)
    w1p = jnp.pad(w1, ((0, LANES - HY_EMB), (0, 0)))
    row = lambda a: a.reshape(1, -1)
    return pl.pallas_call(
        _hy_filter_body,
        out_shape=(
            jax.ShapeDtypeStruct((HY_ORDER, seq, HY_WIDTH), BF16),
            jax.ShapeDtypeStruct((HY_ORDER, seq, HY_WIDTH), BF16),
            jax.ShapeDtypeStruct((HY_ORDER, 8, HY_WIDTH), F32),
        ),
        grid=(HY_ORDER,),
        in_specs=[
            _const_spec((seq, LANES)),
            _const_spec((LANES, HY_HIDDEN)), _const_spec((1, HY_HIDDEN)),
            _const_spec((HY_HIDDEN, HY_HIDDEN)), _const_spec((1, HY_HIDDEN)),
            _const_spec((HY_HIDDEN, HY_HIDDEN)), _const_spec((1, HY_HIDDEN)),
            _const_spec((1, HY_HIDDEN)),
            pl.BlockSpec((1, HY_HIDDEN, HY_WIDTH), lambda o: (o, 0, 0)),
            pl.BlockSpec((1, HY_HIDDEN, HY_WIDTH), lambda o: (o, 0, 0)),
            _const_spec((1, HY_WIDTH)),
            _const_spec((seq, seq)), _const_spec((seq, seq)),
        ],
        out_specs=(
            pl.BlockSpec((1, seq, HY_WIDTH), lambda o: (o, 0, 0)),
            pl.BlockSpec((1, seq, HY_WIDTH), lambda o: (o, 0, 0)),
            pl.BlockSpec((1, 8, HY_WIDTH), lambda o: (o, 0, 0)),
        ),
        compiler_params=pltpu.CompilerParams(
            dimension_semantics=("arbitrary",), vmem_limit_bytes=VMEM_LIMIT_BYTES),
        name="hy_filter",
    )(z, w1p, row(b1), w2, row(b2), w3, row(b3), row(freq), wf, wb, row(deltas), fre, fim)


def _hy_conv_body(v_ref, x1_ref, x2_ref, wv_ref, w1_ref, w2_ref, bv_ref, b1_ref, b2_ref,
                  fre_ref, fim_ref, hre_ref, him_ref, hny_ref, skip_ref, o_ref,
                  z_ref, g1_ref, g2_ref, zb_ref, p_ref):
    seq = v_ref.shape[1]
    n_blk = seq // HY_RBLK
    row = lax.broadcasted_iota(jnp.int32, (seq, HY_CBLK), 0)
    sgn_blk = (1 - 2 * (lax.broadcasted_iota(jnp.int32, (HY_RBLK, HY_CBLK), 0) & 1)).astype(F32)

    def short_conv(x_ref, w_ref, b_ref):
        x = x_ref[0].astype(F32)
        w = w_ref[...]
        prev = jnp.where(row == 0, 0.0, pltpu.roll(x, 1, 0))
        nxt = jnp.where(row == seq - 1, 0.0, pltpu.roll(x, seq - 1, 0))
        return w[0:1] * prev + w[1:2] * x + w[2:3] * nxt + b_ref[...]

    def alt_sum(z):
        zr = z.reshape(z.shape[0] // HY_RBLK, HY_RBLK, HY_CBLK)
        return jnp.sum(jnp.sum(zr, axis=0) * sgn_blk, axis=0, keepdims=True)

    v = short_conv(v_ref, wv_ref, bv_ref)
    z_ref[...] = v
    zb_ref[...] = v.astype(BF16)
    g1_ref[...] = short_conv(x1_ref, w1_ref, b1_ref)
    g2_ref[...] = short_conv(x2_ref, w2_ref, b2_ref)
    znyq = alt_sum(v)
    skip = skip_ref[...]

    def rows_of(i):
        return pl.ds(pl.multiple_of(i * HY_RBLK, HY_RBLK), HY_RBLK)

    for o, gate_ref in enumerate((g1_ref, g2_ref)):
        def spectrum(i, carry, o=o):
            rows = rows_of(i)
            zre = jnp.dot(fre_ref[rows, :], zb_ref[...], preferred_element_type=F32)
            zim = jnp.dot(fim_ref[rows, :], zb_ref[...], preferred_element_type=F32)
            hre = hre_ref[o, rows, :].astype(F32)
            him = him_ref[o, rows, :].astype(F32)
            p_ref[0, rows, :] = (zre * hre - zim * him).astype(BF16)
            p_ref[1, rows, :] = (zre * him + zim * hre).astype(BF16)
            return carry

        lax.fori_loop(0, n_blk, spectrum, 0)
        nyq_term = sgn_blk * (znyq * hny_ref[o][0:1])

        def back(i, acc, o=o, gate_ref=gate_ref, nyq_term=nyq_term):
            rows = rows_of(i)
            y = (jnp.dot(fre_ref[rows, :], p_ref[0], preferred_element_type=F32)
                 + jnp.dot(fim_ref[rows, :], p_ref[1], preferred_element_type=F32))
            z = gate_ref[rows, :] * (y + nyq_term + skip[o:o + 1] * z_ref[rows, :])
            if o + 1 < HY_ORDER:
                z_ref[rows, :] = z
                zb_ref[rows, :] = z.astype(BF16)
                return acc + jnp.sum(z * sgn_blk, axis=0, keepdims=True)
            o_ref[0, rows, :] = z.astype(BF16)
            return acc

        znyq = lax.fori_loop(0, n_blk, back, jnp.zeros((1, HY_CBLK), F32))


def _hy_conv(u_hy, conv_w, conv_b, fre, fim, hre, him, hny, skip):
    nb, seq, _ = u_hy.shape
    ncb = HY_WIDTH // HY_CBLK
    data = lambda part: pl.BlockSpec((1, seq, HY_CBLK), lambda c, b: (b, 0, part * ncb + c))
    wspec = lambda part: pl.BlockSpec((3, HY_CBLK), lambda c, b: (0, part * ncb + c))
    bspec = lambda part: pl.BlockSpec((1, HY_CBLK), lambda c, b: (0, part * ncb + c))
    spec_once = lambda rows: pl.BlockSpec(
        (HY_ORDER, rows, HY_CBLK), lambda c, b: (0, 0, c), pipeline_mode=pl.Buffered(1))
    return pl.pallas_call(
        _hy_conv_body,
        out_shape=jax.ShapeDtypeStruct((nb, seq, HY_WIDTH), BF16),
        grid=(ncb, nb),
        in_specs=[
            data(0), data(1), data(2),
            wspec(0), wspec(1), wspec(2),
            bspec(0), bspec(1), bspec(2),
            _const_spec((seq, seq)), _const_spec((seq, seq)),
            spec_once(seq), spec_once(seq),
            pl.BlockSpec((HY_ORDER, 8, HY_CBLK), lambda c, b: (0, 0, c)),
            pl.BlockSpec((HY_ORDER, HY_CBLK), lambda c, b: (0, c)),
        ],
        out_specs=pl.BlockSpec((1, seq, HY_CBLK), lambda c, b: (b, 0, c)),
        scratch_shapes=[
            pltpu.VMEM((seq, HY_CBLK), F32), pltpu.VMEM((seq, HY_CBLK), F32),
            pltpu.VMEM((seq, HY_CBLK), F32), pltpu.VMEM((seq, HY_CBLK), BF16),
            pltpu.VMEM((2, seq, HY_CBLK), BF16),
        ],
        compiler_params=pltpu.CompilerParams(
            dimension_semantics=("arbitrary", "arbitrary"), vmem_limit_bytes=VMEM_LIMIT_BYTES),
        name="hy_conv",
    )(u_hy, u_hy, u_hy, conv_w, conv_w, conv_w, conv_b, conv_b, conv_b,
      fre, fim, hre, him, hny, skip)


def _bucket_matrix(dilation):
    qi = np.arange(Q_TILE, dtype=np.int64)[:, None]
    kj = np.arange(K_WIN, dtype=np.int64)[None, :]
    delta = kj - BAND - qi
    rel = delta * dilation
    half = N_BUCKETS // 2
    n = np.abs(rel)
    nf = np.maximum(n, 1).astype(np.float32)
    large = BUCKET_MAX_EXACT + (
        np.log(nf / np.float32(BUCKET_MAX_EXACT)) / np.float32(math.log(BUCKET_MAX_DIST / BUCKET_MAX_EXACT))
        * np.float32(half - BUCKET_MAX_EXACT)).astype(np.int32)
    large = np.minimum(large, half - 1)
    bucket = np.where(rel > 0, half, 0) + np.where(n < BUCKET_MAX_EXACT, n, large)
    return np.where(np.abs(delta) <= BAND, bucket, -1).astype(np.int32)


def _attn_body(group, tbl_ref, bkt_ref, q_ref, k_ref, v_ref, o_ref, l_ref,
               bias_ref, kpad_ref, vpad_ref):
    _, dil, ls, _ = q_ref.shape
    tiles_per_res = ls // Q_TILE

    @pl.when(pl.program_id(0) == 0)
    def _():
        bkt = bkt_ref[...]
        for h in range(HEADS):
            acc = jnp.full((Q_TILE, K_WIN), NEG_INF, F32)
            for bk in range(N_BUCKETS):
                acc = jnp.where(bkt == bk, tbl_ref[bk, group * HEADS + h], acc)
            bias_ref[h] = acc
        zeros = jnp.zeros((dil, BAND, ATTN_WIDTH), BF16)
        kpad_ref[:, 0:BAND] = zeros
        kpad_ref[:, BAND + ls:BAND + ls + BAND] = zeros
        vpad_ref[:, 0:BAND] = zeros
        vpad_ref[:, BAND + ls:BAND + ls + BAND] = zeros

    kpad_ref[:, BAND:BAND + ls] = k_ref[0]
    vpad_ref[:, BAND:BAND + ls] = v_ref[0]

    lane = lax.broadcasted_iota(jnp.int32, (Q_TILE, LANES), 1)
    low_half = lane < HEAD_DIM
    kcol = lax.broadcasted_iota(jnp.int32, (1, K_WIN), 1)
    scale = HEAD_DIM ** -0.5

    def tile(t, carry):
        r = t // tiles_per_res
        j0 = pl.multiple_of((t % tiles_per_res) * Q_TILE, Q_TILE)
        kidx = j0 - BAND + kcol
        kmask = jnp.where((kidx >= 0) & (kidx < ls), 0.0, NEG_INF)
        lse_tile = jnp.zeros((Q_TILE, LANES), F32)
        for hp in range(HEADS // 2):
            cols = slice(hp * LANES, (hp + 1) * LANES)
            q2 = q_ref[0, r, pl.ds(j0, Q_TILE), cols] * scale
            kw = kpad_ref[r, pl.ds(j0, K_WIN), cols]
            vw = vpad_ref[r, pl.ds(j0, K_WIN), cols]
            outs = []
            for half in range(2):
                h = 2 * hp + half
                qh = jnp.where(low_half if half == 0 else ~low_half, q2, 0.0).astype(BF16)
                s = lax.dot_general(qh, kw, (((1,), (1,)), ((), ())), preferred_element_type=F32)
                s = s + bias_ref[h] + kmask
                m = jnp.max(s, axis=-1, keepdims=True)
                p = jnp.exp(s - m)
                den = jnp.sum(p, axis=-1, keepdims=True)
                pv = jnp.dot(p.astype(BF16), vw, preferred_element_type=F32)
                outs.append(pv / den)
                lse_tile = jnp.where(lane == h, m + jnp.log(den), lse_tile)
            o_ref[0, r, pl.ds(j0, Q_TILE), cols] = jnp.where(low_half, outs[0], outs[1]).astype(BF16)
        l_ref[0, r, pl.ds(j0, Q_TILE), :] = lse_tile
        return carry

    lax.fori_loop(0, dil * tiles_per_res, tile, 0, unroll=ATTN_UNROLL)


def _attn_group(qkv_g, tbl, group, dilation):
    nb, _, ls, _ = qkv_g.shape
    part = lambda which: pl.BlockSpec((1, dilation, ls, ATTN_WIDTH), lambda b: (b, 0, 0, which))
    bkt = jnp.asarray(_bucket_matrix(dilation))
    return pl.pallas_call(
        functools.partial(_attn_body, group),
        out_shape=(
            jax.ShapeDtypeStruct((nb, dilation, ls, ATTN_WIDTH), BF16),
            jax.ShapeDtypeStruct((nb, dilation, ls, LANES), F32),
        ),
        grid=(nb,),
        in_specs=[
            pl.BlockSpec(memory_space=pltpu.SMEM),
            _const_spec((Q_TILE, K_WIN)),
            part(0), part(1), part(2),
        ],
        out_specs=(
            pl.BlockSpec((1, dilation, ls, ATTN_WIDTH), lambda b: (b, 0, 0, 0)),
            pl.BlockSpec((1, dilation, ls, LANES), lambda b: (b, 0, 0, 0)),
        ),
        scratch_shapes=[
            pltpu.VMEM((HEADS, Q_TILE, K_WIN), F32),
            pltpu.VMEM((dilation, ls + 2 * BAND, ATTN_WIDTH), BF16),
            pltpu.VMEM((dilation, ls + 2 * BAND, ATTN_WIDTH), BF16),
        ],
        compiler_params=pltpu.CompilerParams(
            dimension_semantics=("arbitrary",), vmem_limit_bytes=VMEM_LIMIT_BYTES),
        name=f"attn_g{group}",
    )(tbl, bkt, qkv_g, qkv_g, qkv_g)


def _rglru_body(nt, cw_ref, cb_ref, wf_ref, wb_ref, bf_ref, bb_ref, lam_ref,
                xf_ref, pf_ref, nf_ref, xb_ref, pb_ref, nb_ref,
                hf_ref, hb_ref, xw_ref, a_ref, u_ref, hs_ref, cf_ref, cbk_ref):
    i = pl.program_id(0)
    nbat, tt, width = xf_ref.shape
    rows = tt * nbat
    n_slab = width // LANES
    halo_rows = HALO_T * nbat

    @pl.when(i == 0)
    def _():
        cf_ref[...] = jnp.zeros_like(cf_ref)
        cbk_ref[...] = jnp.zeros_like(cbk_ref)

    def to_time_major(src_ref, n_t, row0, keep):
        for b in range(nbat):
            x = src_ref[b].astype(F32)
            x = x if keep is None else jnp.where(keep, x, 0.0)
            for s, slab in enumerate(_lane_slabs(x)):
                xw_ref.at[s][pl.ds(row0 + b, n_t, stride=nbat), :] = slab

    def gates(x_ref, p_ref, n_ref, tile_idx, w_ref, b_ref, lam):
        to_time_major(p_ref, HALO_T, 0, tile_idx > 0)
        to_time_major(x_ref, tt, halo_rows, None)
        to_time_major(n_ref, HALO_T, halo_rows + rows, tile_idx < nt - 1)
        cw = cw_ref[...]

        def tap(k):
            start = halo_rows + (k - 2) * nbat
            return jnp.concatenate([xw_ref[s, start:start + rows, :] for s in range(n_slab)], axis=1)

        xc = cw[0:1] * tap(0) + cw[1:2] * tap(1) + cw[2:3] * tap(2) + cw[3:4] * tap(3) + cb_ref[...]
        g = jnp.dot(xc.astype(BF16), w_ref[...], preferred_element_type=F32) + b_ref[...]
        r = _sigmoid(g[:, :width])
        gi = _sigmoid(g[:, width:])
        softplus = jnp.maximum(-lam, 0.0) + jnp.log(1.0 + jnp.exp(-jnp.abs(lam)))
        log_a = (-RG_C * softplus) * r
        a = jnp.exp(log_a)
        a_ref[...] = a
        u_ref[...] = jnp.sqrt(1.0 - a * a) * (gi * xc)

    def step(t, h):
        r0 = pl.multiple_of(t * nbat, nbat)
        h = a_ref[pl.ds(r0, nbat), :] * h + u_ref[pl.ds(r0, nbat), :]
        for s, slab in enumerate(_lane_slabs(h)):
            hs_ref[s, pl.ds(r0, nbat), :] = slab
        return h

    def to_batch_major(dst_ref):
        for b in range(nbat):
            dst_ref[b] = jnp.concatenate(
                [hs_ref.at[s][pl.ds(b, tt, stride=nbat), :] for s in range(n_slab)], axis=1).astype(BF16)

    gates(xf_ref, pf_ref, nf_ref, i, wf_ref, bf_ref, lam_ref[0:1])
    cf_ref[...] = lax.fori_loop(0, tt, step, cf_ref[...], unroll=8)
    to_batch_major(hf_ref)

    gates(xb_ref, pb_ref, nb_ref, nt - 1 - i, wb_ref, bb_ref, lam_ref[1:2])
    cbk_ref[...] = lax.fori_loop(0, tt, lambda s, h: step(tt - 1 - s, h), cbk_ref[...], unroll=8)
    to_batch_major(hb_ref)


def _rglru(rg, conv_w, conv_b, w_f, w_b, b_f, b_b, lam):
    nbat, seq, _ = rg.shape
    nt = seq // RG_TT
    hpt = RG_TT // HALO_T
    nh = seq // HALO_T
    cur = lambda f: pl.BlockSpec((nbat, RG_TT, RG_WIDTH), lambda i: (0, f(i), 0))
    prv = lambda f: pl.BlockSpec(
        (nbat, HALO_T, RG_WIDTH), lambda i: (0, jnp.maximum(f(i) * hpt - 1, 0), 0))
    nxt = lambda f: pl.BlockSpec(
        (nbat, HALO_T, RG_WIDTH), lambda i: (0, jnp.minimum((f(i) + 1) * hpt, nh - 1), 0))
    fw = lambda i: i
    bw = lambda i: nt - 1 - i
    n_slab = RG_WIDTH // LANES
    return pl.pallas_call(
        functools.partial(_rglru_body, nt),
        out_shape=(
            jax.ShapeDtypeStruct((nbat, seq, RG_WIDTH), BF16),
            jax.ShapeDtypeStruct((nbat, seq, RG_WIDTH), BF16),
        ),
        grid=(nt,),
        in_specs=[
            _const_spec((RG_CONV, RG_WIDTH)), _const_spec((1, RG_WIDTH)),
            _const_spec((RG_WIDTH, 2 * RG_WIDTH)), _const_spec((RG_WIDTH, 2 * RG_WIDTH)),
            _const_spec((1, 2 * RG_WIDTH)), _const_spec((1, 2 * RG_WIDTH)),
            _const_spec((2, RG_WIDTH)),
            cur(fw), prv(fw), nxt(fw), cur(bw), prv(bw), nxt(bw),
        ],
        out_specs=(cur(fw), cur(bw)),
        scratch_shapes=[
            pltpu.VMEM((n_slab, (RG_TT + 2 * HALO_T) * nbat, LANES), F32),
            pltpu.VMEM((RG_TT * nbat, RG_WIDTH), F32),
            pltpu.VMEM((RG_TT * nbat, RG_WIDTH), F32),
            pltpu.VMEM((n_slab, RG_TT * nbat, LANES), F32),
            pltpu.VMEM((nbat, RG_WIDTH), F32),
            pltpu.VMEM((nbat, RG_WIDTH), F32),
        ],
        compiler_params=pltpu.CompilerParams(
            dimension_semantics=("arbitrary",), vmem_limit_bytes=VMEM_LIMIT_BYTES),
        name="rglru",
    )(conv_w, conv_b, w_f, w_b, b_f, b_b, lam, rg, rg, rg, rg, rg, rg)


def _block_diag(w):
    eye = jnp.eye(RG_BLOCKS, dtype=w.dtype)
    return jnp.einsum("hij,hk->hikj", w, eye).reshape(RG_WIDTH, RG_WIDTH)


def _merge_body(h_ref, g_ref, wg_ref, bg_ref, ya_ref, o0_ref, o1_ref, o2_ref,
                l0_ref, l1_ref, l2_ref, hf_ref, hb_ref, gate_ref, exp_ref,
                wphy_ref, wpat_ref, wprg_ref, wout_ref, out_ref, os_ref, ls_ref):
    x = h_ref[0]
    xb = _rms(x, g_ref[...]).astype(BF16)
    gates = _sigmoid(jnp.dot(xb, wg_ref[...], preferred_element_type=F32) + bg_ref[...])

    def token_order(o_ref, l_ref):
        d, n = o_ref.shape[1], o_ref.shape[2]
        if d == 1:
            return o_ref[0, 0].astype(F32), l_ref[0, 0]
        for r in range(d):
            for s, slab in enumerate(_lane_slabs(o_ref[0, r].astype(F32))):
                os_ref.at[s][pl.ds(r, n, stride=d), :] = slab
            ls_ref[pl.ds(r, n, stride=d), :] = l_ref[0, r]
        o = jnp.concatenate([os_ref[s] for s in range(ATTN_WIDTH // LANES)], axis=1)
        return o, ls_ref[...]

    o0, l0 = token_order(o0_ref, l0_ref)
    o1, l1 = token_order(o1_ref, l1_ref)
    o2, l2 = token_order(o2_ref, l2_ref)

    m = jnp.maximum(jnp.maximum(l0, l1), l2)
    e0, e1, e2 = jnp.exp(l0 - m), jnp.exp(l1 - m), jnp.exp(l2 - m)
    inv = 1.0 / (e0 + e1 + e2)

    def widen(w):
        hi = w.astype(BF16)
        lo = (w - hi.astype(F32)).astype(BF16)
        return (jnp.dot(hi, exp_ref[...], preferred_element_type=F32)
                + jnp.dot(lo, exp_ref[...], preferred_element_type=F32))

    y_b = widen(e0 * inv) * o0 + widen(e1 * inv) * o1 + widen(e2 * inv) * o2

    gt = gate_ref[0].astype(F32)
    gelu = 0.5 * gt * (1.0 + jnp.tanh(math.sqrt(2.0 / math.pi) * (gt + 0.044715 * (gt * gt * gt))))
    y_c = (hf_ref[0].astype(F32) + hb_ref[0].astype(F32)) * gelu

    merged = (gates[:, :D_MODEL]
              * jnp.dot(ya_ref[0], wphy_ref[...], preferred_element_type=F32)
              + gates[:, D_MODEL:2 * D_MODEL]
              * jnp.dot(y_b.astype(BF16), wpat_ref[...], preferred_element_type=F32)
              + gates[:, 2 * D_MODEL:]
              * jnp.dot(y_c.astype(BF16), wprg_ref[...], preferred_element_type=F32))
    out_ref[0] = x + jnp.dot(merged.astype(BF16), wout_ref[...], preferred_element_type=F32)


def _merge(h3, g, w_gate, b_gate, y_a, attn, hf, hb, rg, expand, wp_hy, wp_attn, wp_rg, w_out):
    nb, seq, _ = h3.shape
    tok = lambda width, col=0: pl.BlockSpec((1, TOKEN_TILE, width), lambda b, i: (b, i, col))
    res = lambda d, width: pl.BlockSpec((1, d, TOKEN_TILE // d, width), lambda b, i: (b, 0, i, 0))
    (o0, l0), (o1, l1), (o2, l2) = attn
    dils = [d for _, d in ATTN_GROUPS]
    return pl.pallas_call(
        _merge_body,
        out_shape=jax.ShapeDtypeStruct(h3.shape, F32),
        grid=(nb, seq // TOKEN_TILE),
        in_specs=[
            tok(D_MODEL), _const_spec((1, D_MODEL)),
            _const_spec((D_MODEL, N_BRANCH * D_MODEL)), _const_spec((1, N_BRANCH * D_MODEL)),
            tok(HY_WIDTH),
            res(dils[0], ATTN_WIDTH), res(dils[1], ATTN_WIDTH), res(dils[2], ATTN_WIDTH),
            res(dils[0], LANES), res(dils[1], LANES), res(dils[2], LANES),
            tok(RG_WIDTH), tok(RG_WIDTH), tok(RG_WIDTH, 1),
            _const_spec((LANES, ATTN_WIDTH)),
            _const_spec((HY_WIDTH, D_MODEL)), _const_spec((ATTN_WIDTH, D_MODEL)),
            _const_spec((RG_WIDTH, D_MODEL)), _const_spec((D_MODEL, D_MODEL)),
        ],
        out_specs=tok(D_MODEL),
        scratch_shapes=[
            pltpu.VMEM((ATTN_WIDTH // LANES, TOKEN_TILE, LANES), F32),
            pltpu.VMEM((TOKEN_TILE, LANES), F32),
        ],
        compiler_params=pltpu.CompilerParams(
            dimension_semantics=("arbitrary", "arbitrary"), vmem_limit_bytes=VMEM_LIMIT_BYTES),
        name="merge",
    )(h3, g, w_gate, b_gate, y_a, o0, o1, o2, l0, l1, l2, hf, hb, rg, expand,
      wp_hy, wp_attn, wp_rg, w_out)


def _final_norm_body(h_ref, g_ref, o_ref):
    o_ref[...] = _rms(h_ref[...], g_ref[...])


def _final_norm(h, g):
    n_tok = h.shape[0]
    return pl.pallas_call(
        _final_norm_body,
        out_shape=jax.ShapeDtypeStruct(h.shape, F32),
        grid=(n_tok // TOKEN_TILE,),
        in_specs=[pl.BlockSpec((TOKEN_TILE, D_MODEL), lambda i: (i, 0)), _const_spec((1, D_MODEL))],
        out_specs=pl.BlockSpec((TOKEN_TILE, D_MODEL), lambda i: (i, 0)),
        compiler_params=pltpu.CompilerParams(dimension_semantics=("arbitrary",)),
        name="final_norm",
    )(h, g)


def _mixer(h3, l, p, consts):
    fre, fim, zpos, deltas, expand = consts
    w_in = p["w_in"][l].astype(BF16)
    w_qkv = w_in[:, HY_COLS:HY_COLS + QKV_COLS].reshape(D_MODEL, 3, N_GROUPS, ATTN_WIDTH)
    w_q = [w_qkv[:, :, g, :].reshape(D_MODEL, 3 * ATTN_WIDTH) for g in range(N_GROUPS)]
    hy, q0, q1, q2, rg = _inproj(
        h3, p["mix_norm"][l].reshape(1, -1), w_in[:, :HY_COLS], w_q, w_in[:, HY_COLS + QKV_COLS:])

    hre, him, hny = _hy_filter(
        zpos, p["hy_w1"][l], p["hy_b1"][l], p["hy_w2"][l], p["hy_b2"][l], p["hy_w3"][l],
        p["hy_b3"][l], p["hy_freq"][l], p["hy_wout"][l], deltas, fre, fim)
    y_a = _hy_conv(hy, p["hy_conv_w"][l], p["hy_conv_b"][l].reshape(1, -1),
                   fre, fim, hre, him, hny, p["hy_skip"][l])

    attn = [_attn_group(q, p["rel_bias"], g, dil)
            for g, (q, (_, dil)) in enumerate(zip((q0, q1, q2), ATTN_GROUPS))]

    rg_w = lambda d: jnp.concatenate(
        [_block_diag(p["rg_wa"][l, d]), _block_diag(p["rg_wx"][l, d])], axis=1).astype(BF16)
    rg_b = lambda d: jnp.concatenate([p["rg_ba"][l, d], p["rg_bx"][l, d]]).reshape(1, -1)
    hf, hb = _rglru(rg, p["rg_conv_w"][l], p["rg_conv_b"][l].reshape(1, -1),
                    rg_w(0), rg_w(1), rg_b(0), rg_b(1), p["rg_lambda"][l])

    return _merge(
        h3, p["mix_norm"][l].reshape(1, -1), p["w_gate"][l].astype(BF16),
        p["b_gate"][l].reshape(1, -1), y_a, attn, hf, hb, rg, expand,
        p["w_proj_hy"][l].astype(BF16), p["w_proj_attn"][l].astype(BF16),
        p["w_proj_rg"][l].astype(BF16), p["w_out"][l].astype(BF16))


def _forward(x, p):
    nb, seq, _ = x.shape
    fre, fim = _dft_mats(seq)
    deltas = jnp.abs(jnp.linspace(math.log(HY_DECAY_TARGET) / HY_FAST_DECAY,
                                  math.log(HY_DECAY_TARGET) / HY_SLOW_DECAY, HY_WIDTH, dtype=F32))
    head_of_lane = jnp.arange(ATTN_WIDTH, dtype=jnp.int32) // HEAD_DIM
    expand = (jnp.arange(LANES, dtype=jnp.int32)[:, None] == head_of_lane[None, :]).astype(BF16)
    consts = (fre, fim, _hy_positions(seq), deltas, expand)

    def ffn(h3, norm, wg, wu, wd, l):
        wgu = jnp.concatenate([wg[l], wu[l]], axis=1).astype(BF16)
        out = _ffn(h3.reshape(nb * seq, D_MODEL), norm[l].reshape(1, -1), wgu, wd[l].astype(BF16))
        return out.reshape(nb, seq, D_MODEL)

    h = x
    for l in range(DEPTH):
        h = ffn(h, p["ffn1_norm"], p["ffn1_wg"], p["ffn1_wu"], p["ffn1_wd"], l)
        h = _mixer(h, l, p, consts)
        h = ffn(h, p["ffn2_norm"], p["ffn2_wg"], p["ffn2_wu"], p["ffn2_wd"], l)
    out = _final_norm(h.reshape(nb * seq, D_MODEL), p["final_norm"].reshape(1, -1))
    return out.reshape(nb, seq, D_MODEL)


def kernel(x, ffn1_norm, ffn1_wg, ffn1_wu, ffn1_wd, mix_norm, w_in, hy_conv_w, hy_conv_b, hy_w1, hy_b1, hy_w2, hy_b2, hy_w3, hy_b3, hy_freq, hy_wout, hy_skip, rel_bias, rg_conv_w, rg_conv_b, rg_wa, rg_ba, rg_wx, rg_bx, rg_lambda, w_gate, b_gate, w_proj_hy, w_proj_attn, w_proj_rg, w_out, ffn2_norm, ffn2_wg, ffn2_wu, ffn2_wd, final_norm):
    p = dict(
        ffn1_norm=ffn1_norm, ffn1_wg=ffn1_wg, ffn1_wu=ffn1_wu, ffn1_wd=ffn1_wd, mix_norm=mix_norm,
        w_in=w_in, hy_conv_w=hy_conv_w, hy_conv_b=hy_conv_b, hy_w1=hy_w1, hy_b1=hy_b1, hy_w2=hy_w2,
        hy_b2=hy_b2, hy_w3=hy_w3, hy_b3=hy_b3, hy_freq=hy_freq, hy_wout=hy_wout, hy_skip=hy_skip,
        rel_bias=rel_bias, rg_conv_w=rg_conv_w, rg_conv_b=rg_conv_b, rg_wa=rg_wa, rg_ba=rg_ba,
        rg_wx=rg_wx, rg_bx=rg_bx, rg_lambda=rg_lambda, w_gate=w_gate, b_gate=b_gate,
        w_proj_hy=w_proj_hy, w_proj_attn=w_proj_attn, w_proj_rg=w_proj_rg, w_out=w_out,
        ffn2_norm=ffn2_norm, ffn2_wg=ffn2_wg, ffn2_wu=ffn2_wu, ffn2_wd=ffn2_wd,
        final_norm=final_norm)
    return _forward(x, p)
```

```python
import functools
import math

import numpy as np
import jax
import jax.numpy as jnp
from jax import lax
from jax.experimental import pallas as pl
from jax.experimental.pallas import tpu as pltpu

F32 = jnp.float32
BF16 = jnp.bfloat16

D_MODEL = 1024
D_FF = 2816
DEPTH = 4
RMS_EPS = 1e-6

HY_WIDTH = 512
HY_ORDER = 2
HY_EMB = 33
HY_BANDS = 16
HY_HIDDEN = 64
HY_COLS = 3 * HY_WIDTH
HY_DECAY_TARGET = 1e-2
HY_FAST_DECAY = 0.3
HY_SLOW_DECAY = 1.5
HY_MOD_SHIFT = 0.05
HY_NORM_EPS = 1e-6

ATTN_GROUPS = ((128, 1), (512, 4), (2048, 16))
N_GROUPS = 3
HEADS = 8
HEAD_DIM = 64
ATTN_WIDTH = HEADS * HEAD_DIM
QKV_COLS = 3 * N_GROUPS * ATTN_WIDTH
BAND = 64
N_BUCKETS = 32
BUCKET_MAX_EXACT = 8
BUCKET_MAX_DIST = 1024
NEG_INF = -1e30
LOG2E = math.log2(math.e)
LN2 = math.log(2.0)

RG_WIDTH = 512
RG_BLOCKS = 8
RG_BLOCK = 64
RG_CONV = 4
RG_C = 8.0
RG_COLS = 2 * RG_WIDTH

N_BRANCH = 3

LANES = 128
VMEM_LIMIT_BYTES = 56 * 1024 * 1024

TOKEN_TILE = 512
Q_TILE = 128
K_WIN = Q_TILE + 2 * BAND
ATTN_UNROLL = 4
HY_CBLK = 256
HY_RBLK = 512
RG_TT = 128
HALO_T = 8


def _const_spec(shape):
    nd = len(shape)
    return pl.BlockSpec(shape, lambda *_: (0,) * nd, pipeline_mode=pl.Buffered(1))


def _rms(x, g):
    ms = jnp.mean(x * x, axis=-1, keepdims=True)
    return x * lax.rsqrt(ms + RMS_EPS) * g


def _sigmoid(x):
    return 1.0 / (1.0 + jnp.exp(-x))


def _lane_slabs(x):
    return [x[:, s * LANES:(s + 1) * LANES] for s in range(x.shape[1] // LANES)]


def _ffn_body(h_ref, g_ref, wgu_ref, wd_ref, *rest):
    gout_ref, o_ref = rest if len(rest) == 2 else (None, rest[0])
    x = h_ref[...]
    xb = _rms(x, g_ref[...]).astype(BF16)
    gu = jnp.dot(xb, wgu_ref[...], preferred_element_type=F32)
    gate = gu[:, :D_FF]
    up = gu[:, D_FF:]
    act = (gate * _sigmoid(gate)) * up
    out = x + 0.5 * jnp.dot(act.astype(BF16), wd_ref[...], preferred_element_type=F32)
    o_ref[...] = out if gout_ref is None else _rms(out, gout_ref[...])


def _ffn(h, g, wgu, wd, g_out=None):
    n_tok = h.shape[0]
    extra = [] if g_out is None else [g_out]
    return pl.pallas_call(
        _ffn_body,
        out_shape=jax.ShapeDtypeStruct(h.shape, F32),
        grid=(n_tok // TOKEN_TILE,),
        in_specs=[
            pl.BlockSpec((TOKEN_TILE, D_MODEL), lambda i: (i, 0)),
            _const_spec((1, D_MODEL)),
            _const_spec((D_MODEL, 2 * D_FF)),
            _const_spec((D_FF, D_MODEL)),
        ] + [_const_spec((1, D_MODEL))] * len(extra),
        out_specs=pl.BlockSpec((TOKEN_TILE, D_MODEL), lambda i: (i, 0)),
        compiler_params=pltpu.CompilerParams(
            dimension_semantics=("arbitrary",), vmem_limit_bytes=VMEM_LIMIT_BYTES),
        name="ffn",
    )(h, g, wgu, wd, *extra)


def _inproj_body(h_ref, g_ref, why_ref, wq0_ref, wq1_ref, wq2_ref, wrg_ref,
                 hy_ref, q0_ref, q1_ref, q2_ref, rg_ref, xs_ref):
    xn = _rms(h_ref[0], g_ref[...])
    xb = xn.astype(BF16)
    hy_ref[0] = jnp.dot(xb, why_ref[...], preferred_element_type=F32).astype(BF16)
    q0_ref[0, 0] = jnp.dot(xb, wq0_ref[...], preferred_element_type=F32).astype(BF16)
    rg_ref[0] = jnp.dot(xb, wrg_ref[...], preferred_element_type=F32).astype(BF16)

    for s, slab in enumerate(_lane_slabs(xn)):
        xs_ref[s] = slab

    def by_residue(d):
        n = TOKEN_TILE // d
        cols = [jnp.concatenate([xs_ref.at[s][pl.ds(r, n, stride=d), :] for r in range(d)], axis=0)
                for s in range(D_MODEL // LANES)]
        return jnp.concatenate(cols, axis=1).astype(BF16)

    for q_ref, w_ref, (_, d) in ((q1_ref, wq1_ref, ATTN_GROUPS[1]), (q2_ref, wq2_ref, ATTN_GROUPS[2])):
        out = jnp.dot(by_residue(d), w_ref[...], preferred_element_type=F32).astype(BF16)
        q_ref[0] = out.reshape(d, TOKEN_TILE // d, 3 * ATTN_WIDTH)


def _inproj(h3, g, w_hy, w_q, w_rg):
    nb, seq, _ = h3.shape
    dils = [d for _, d in ATTN_GROUPS]
    qshape = lambda d: jax.ShapeDtypeStruct((nb, d, seq // d, 3 * ATTN_WIDTH), BF16)
    qspec = lambda d: pl.BlockSpec((1, d, TOKEN_TILE // d, 3 * ATTN_WIDTH), lambda b, i: (b, 0, i, 0))
    return pl.pallas_call(
        _inproj_body,
        out_shape=(
            jax.ShapeDtypeStruct((nb, seq, HY_COLS), BF16),
            qshape(dils[0]), qshape(dils[1]), qshape(dils[2]),
            jax.ShapeDtypeStruct((nb, seq, RG_COLS), BF16),
        ),
        grid=(nb, seq // TOKEN_TILE),
        in_specs=[
            pl.BlockSpec((1, TOKEN_TILE, D_MODEL), lambda b, i: (b, i, 0)),
            _const_spec((1, D_MODEL)),
            _const_spec((D_MODEL, HY_COLS)),
            _const_spec((D_MODEL, 3 * ATTN_WIDTH)),
            _const_spec((D_MODEL, 3 * ATTN_WIDTH)),
            _const_spec((D_MODEL, 3 * ATTN_WIDTH)),
            _const_spec((D_MODEL, RG_COLS)),
        ],
        out_specs=(
            pl.BlockSpec((1, TOKEN_TILE, HY_COLS), lambda b, i: (b, i, 0)),
            qspec(dils[0]), qspec(dils[1]), qspec(dils[2]),
            pl.BlockSpec((1, TOKEN_TILE, RG_COLS), lambda b, i: (b, i, 0)),
        ),
        scratch_shapes=[pltpu.VMEM((D_MODEL // LANES, TOKEN_TILE, LANES), F32)],
        compiler_params=pltpu.CompilerParams(
            dimension_semantics=("arbitrary", "arbitrary"), vmem_limit_bytes=VMEM_LIMIT_BYTES),
        name="inproj",
    )(h3, g, w_hy, w_q[0], w_q[1], w_q[2], w_rg)


def _dft_mats(seq):
    n = jnp.arange(seq, dtype=jnp.int32)
    kn = (n[:, None] * n[None, :]) % (2 * seq)
    ang = kn.astype(F32) * (math.pi / seq)
    return jnp.cos(ang).astype(BF16), (-jnp.sin(ang)).astype(BF16)


def _hy_positions(seq):
    t = jnp.linspace(0.0, 1.0, seq, dtype=F32)[:, None]
    tr = jnp.arange(seq, dtype=F32)[:, None]
    wpos = 2.0 * math.pi * tr / seq
    fb = jnp.linspace(1e-4, HY_BANDS - 1, HY_BANDS, dtype=F32)[None, :]
    z = jnp.concatenate([t, jnp.cos(fb * wpos), -jnp.sin(fb * wpos)], axis=-1)
    return jnp.pad(z, ((0, 0), (0, LANES - HY_EMB)))


def _hy_filter_body(z_ref, w1_ref, b1_ref, w2_ref, b2_ref, w3_ref, b3_ref, fr_ref,
                    wf_ref, wb_ref, dl_ref, fre_ref, fim_ref,
                    hre_ref, him_ref, hny_ref):
    hi = lax.Precision.HIGHEST
    seq = z_ref.shape[0]
    fr = fr_ref[...]
    hdn = jnp.sin(fr * (jnp.dot(z_ref[...], w1_ref[...], precision=hi,
                                preferred_element_type=F32) + b1_ref[...]))
    hdn = jnp.sin(fr * (jnp.dot(hdn, w2_ref[...], precision=hi,
                                preferred_element_type=F32) + b2_ref[...]))
    hdn = jnp.sin(fr * (jnp.dot(hdn, w3_ref[...], precision=hi,
                                preferred_element_type=F32) + b3_ref[...]))
    row = lax.broadcasted_iota(jnp.int32, (seq, HY_WIDTH), 0)
    t = row.astype(F32) * (1.0 / (seq - 1))
    decay = jnp.exp(-t * dl_ref[...]) + HY_MOD_SHIFT
    kf = jnp.dot(hdn, wf_ref[0], precision=hi, preferred_element_type=F32) * decay
    kb = jnp.dot(hdn, wb_ref[0], precision=hi, preferred_element_type=F32) * decay
    kb = jnp.where(row == 0, 0.0, kb)
    norm = (jnp.sum(jnp.abs(kf), axis=0, keepdims=True)
            + jnp.sum(jnp.abs(kb), axis=0, keepdims=True) + HY_NORM_EPS)
    kp = (kf + kb) / norm
    km = (kf - kb) / norm
    scale = jnp.where(row == 0, 0.5 / seq, 1.0 / seq)
    sgn = (1 - 2 * (row & 1)).astype(F32)
    hre_ref[0] = (jnp.dot(fre_ref[...], kp.astype(BF16), preferred_element_type=F32)
                  * scale).astype(BF16)
    him_ref[0] = (jnp.dot(fim_ref[...], km.astype(BF16), preferred_element_type=F32)
                  * scale).astype(BF16)
    nyq = jnp.sum(kp * sgn, axis=0, keepdims=True) * (0.5 / seq)
    hny_ref[0] = jnp.broadcast_to(nyq, (8, HY_WIDTH))


def _hy_filter(z, w1, b1, w2, b2, w3, b3, freq, wout, deltas, fre, fim):
    seq = z.shape[0]
    wf = wout[:, :HY_ORDER * HY_WIDTH].reshape(HY_HIDDEN, HY_ORDER, HY_WIDTH).transpose(1, 0, 2)
    wb = wout[:, HY_ORDER * HY_WIDTH:].reshape(HY_HIDDEN, HY_ORDER, HY_WIDTH).transpose(1, 0, 2)
    w1p = jnp.pad(w1, ((0, LANES - HY_EMB), (0, 0)))
    row = lambda a: a.reshape(1, -1)
    return pl.pallas_call(
        _hy_filter_body,
        out_shape=(
            jax.ShapeDtypeStruct((HY_ORDER, seq, HY_WIDTH), BF16),
            jax.ShapeDtypeStruct((HY_ORDER, seq, HY_WIDTH), BF16),
            jax.ShapeDtypeStruct((HY_ORDER, 8, HY_WIDTH), F32),
        ),
        grid=(HY_ORDER,),
        in_specs=[
            _const_spec((seq, LANES)),
            _const_spec((LANES, HY_HIDDEN)), _const_spec((1, HY_HIDDEN)),
            _const_spec((HY_HIDDEN, HY_HIDDEN)), _const_spec((1, HY_HIDDEN)),
            _const_spec((HY_HIDDEN, HY_HIDDEN)), _const_spec((1, HY_HIDDEN)),
            _const_spec((1, HY_HIDDEN)),
            pl.BlockSpec((1, HY_HIDDEN, HY_WIDTH), lambda o: (o, 0, 0)),
            pl.BlockSpec((1, HY_HIDDEN, HY_WIDTH), lambda o: (o, 0, 0)),
            _const_spec((1, HY_WIDTH)),
            _const_spec((seq, seq)), _const_spec((seq, seq)),
        ],
        out_specs=(
            pl.BlockSpec((1, seq, HY_WIDTH), lambda o: (o, 0, 0)),
            pl.BlockSpec((1, seq, HY_WIDTH), lambda o: (o, 0, 0)),
            pl.BlockSpec((1, 8, HY_WIDTH), lambda o: (o, 0, 0)),
        ),
        compiler_params=pltpu.CompilerParams(
            dimension_semantics=("arbitrary",), vmem_limit_bytes=VMEM_LIMIT_BYTES),
        name="hy_filter",
    )(z, w1p, row(b1), w2, row(b2), w3, row(b3), row(freq), wf, wb, row(deltas), fre, fim)


def _hy_conv_body(v_ref, x1_ref, x2_ref, wv_ref, w1_ref, w2_ref, bv_ref, b1_ref, b2_ref,
                  fre_ref, fim_ref, hre_ref, him_ref, hny_ref, skip_ref, o_ref,
                  z_ref, g1_ref, g2_ref, zb_ref, p_ref):
    seq = v_ref.shape[1]
    n_blk = seq // HY_RBLK
    row = lax.broadcasted_iota(jnp.int32, (seq, HY_CBLK), 0)
    sgn_blk = (1 - 2 * (lax.broadcasted_iota(jnp.int32, (HY_RBLK, HY_CBLK), 0) & 1)).astype(F32)

    def short_conv(x_ref, w_ref, b_ref):
        x = x_ref[0].astype(F32)
        w = w_ref[...]
        prev = jnp.where(row == 0, 0.0, pltpu.roll(x, 1, 0))
        nxt = jnp.where(row == seq - 1, 0.0, pltpu.roll(x, seq - 1, 0))
        return w[0:1] * prev + w[1:2] * x + w[2:3] * nxt + b_ref[...]

    def alt_sum(z):
        zr = z.reshape(z.shape[0] // HY_RBLK, HY_RBLK, HY_CBLK)
        return jnp.sum(jnp.sum(zr, axis=0) * sgn_blk, axis=0, keepdims=True)

    v = short_conv(v_ref, wv_ref, bv_ref)
    z_ref[...] = v
    zb_ref[...] = v.astype(BF16)
    g1_ref[...] = short_conv(x1_ref, w1_ref, b1_ref)
    g2_ref[...] = short_conv(x2_ref, w2_ref, b2_ref)
    znyq = alt_sum(v)
    skip = skip_ref[...]

    def rows_of(i):
        return pl.ds(pl.multiple_of(i * HY_RBLK, HY_RBLK), HY_RBLK)

    for o, gate_ref in enumerate((g1_ref, g2_ref)):
        def spectrum(i, carry, o=o):
            rows = rows_of(i)
            zre = jnp.dot(fre_ref[rows, :], zb_ref[...], preferred_element_type=F32)
            zim = jnp.dot(fim_ref[rows, :], zb_ref[...], preferred_element_type=F32)
            hre = hre_ref[o, rows, :].astype(F32)
            him = him_ref[o, rows, :].astype(F32)
            p_ref[0, rows, :] = (zre * hre - zim * him).astype(BF16)
            p_ref[1, rows, :] = (zre * him + zim * hre).astype(BF16)
            return carry

        lax.fori_loop(0, n_blk, spectrum, 0)
        nyq_term = sgn_blk * (znyq * hny_ref[o][0:1])

        def back(i, acc, o=o, gate_ref=gate_ref, nyq_term=nyq_term):
            rows = rows_of(i)
            y = (jnp.dot(fre_ref[rows, :], p_ref[0], preferred_element_type=F32)
                 + jnp.dot(fim_ref[rows, :], p_ref[1], preferred_element_type=F32))
            z = gate_ref[rows, :] * (y + nyq_term + skip[o:o + 1] * z_ref[rows, :])
            if o + 1 < HY_ORDER:
                z_ref[rows, :] = z
                zb_ref[rows, :] = z.astype(BF16)
                return acc + jnp.sum(z * sgn_blk, axis=0, keepdims=True)
            o_ref[0, rows, :] = z.astype(BF16)
            return acc

        znyq = lax.fori_loop(0, n_blk, back, jnp.zeros((1, HY_CBLK), F32))


def _hy_conv(u_hy, conv_w, conv_b, fre, fim, hre, him, hny, skip):
    nb, seq, _ = u_hy.shape
    ncb = HY_WIDTH // HY_CBLK
    data = lambda part: pl.BlockSpec((1, seq, HY_CBLK), lambda c, b: (b, 0, part * ncb + c))
    wspec = lambda part: pl.BlockSpec((3, HY_CBLK), lambda c, b: (0, part * ncb + c))
    bspec = lambda part: pl.BlockSpec((1, HY_CBLK), lambda c, b: (0, part * ncb + c))
    spec_once = lambda rows: pl.BlockSpec(
        (HY_ORDER, rows, HY_CBLK), lambda c, b: (0, 0, c), pipeline_mode=pl.Buffered(1))
    return pl.pallas_call(
        _hy_conv_body,
        out_shape=jax.ShapeDtypeStruct((nb, seq, HY_WIDTH), BF16),
        grid=(ncb, nb),
        in_specs=[
            data(0), data(1), data(2),
            wspec(0), wspec(1), wspec(2),
            bspec(0), bspec(1), bspec(2),
            _const_spec((seq, seq)), _const_spec((seq, seq)),
            spec_once(seq), spec_once(seq),
            pl.BlockSpec((HY_ORDER, 8, HY_CBLK), lambda c, b: (0, 0, c)),
            pl.BlockSpec((HY_ORDER, HY_CBLK), lambda c, b: (0, c)),
        ],
        out_specs=pl.BlockSpec((1, seq, HY_CBLK), lambda c, b: (b, 0, c)),
        scratch_shapes=[
            pltpu.VMEM((seq, HY_CBLK), F32), pltpu.VMEM((seq, HY_CBLK), F32),
            pltpu.VMEM((seq, HY_CBLK), F32), pltpu.VMEM((seq, HY_CBLK), BF16),
            pltpu.VMEM((2, seq, HY_CBLK), BF16),
        ],
        compiler_params=pltpu.CompilerParams(
            dimension_semantics=("arbitrary", "arbitrary"), vmem_limit_bytes=VMEM_LIMIT_BYTES),
        name="hy_conv",
    )(u_hy, u_hy, u_hy, conv_w, conv_w, conv_w, conv_b, conv_b, conv_b,
      fre, fim, hre, him, hny, skip)


def _bucket_matrix(dilation):
    qi = np.arange(Q_TILE, dtype=np.int64)[:, None]
    kj = np.arange(K_WIN, dtype=np.int64)[None, :]
    delta = kj - BAND - qi
    rel = delta * dilation
    half = N_BUCKETS // 2
    n = np.abs(rel)
    nf = np.maximum(n, 1).astype(np.float32)
    large = BUCKET_MAX_EXACT + (
        np.log(nf / np.float32(BUCKET_MAX_EXACT)) / np.float32(math.log(BUCKET_MAX_DIST / BUCKET_MAX_EXACT))
        * np.float32(half - BUCKET_MAX_EXACT)).astype(np.int32)
    large = np.minimum(large, half - 1)
    bucket = np.where(rel > 0, half, 0) + np.where(n < BUCKET_MAX_EXACT, n, large)
    return np.where(np.abs(delta) <= BAND, bucket, -1).astype(np.int32)


def _attn_body(group, tbl_ref, bkt_ref, q_ref, k_ref, v_ref, o_ref, l_ref,
               bias_ref, kpad_ref, vpad_ref):
    _, dil, ls, _ = q_ref.shape
    tiles_per_res = ls // Q_TILE

    kcol = lax.broadcasted_iota(jnp.int32, (Q_TILE, K_WIN), 1)

    @pl.when(pl.program_id(0) == 0)
    def _():
        bkt = bkt_ref[...]
        for h in range(HEADS):
            acc = jnp.full((Q_TILE, K_WIN), NEG_INF, F32)
            for bk in range(N_BUCKETS):
                acc = jnp.where(bkt == bk, tbl_ref[bk, group * HEADS + h] * LOG2E, acc)
            no_head = jnp.where(kcol < BAND, NEG_INF, acc)
            bias_ref[0, h] = acc
            bias_ref[1, h] = no_head
            bias_ref[2, h] = jnp.where(kcol >= Q_TILE + BAND, NEG_INF, acc)
            bias_ref[3, h] = jnp.where(kcol >= Q_TILE + BAND, NEG_INF, no_head)
        zeros = jnp.zeros((dil, BAND, ATTN_WIDTH), BF16)
        kpad_ref[:, 0:BAND] = zeros
        kpad_ref[:, BAND + ls:BAND + ls + BAND] = zeros
        vpad_ref[:, 0:BAND] = zeros
        vpad_ref[:, BAND + ls:BAND + ls + BAND] = zeros

    kpad_ref[:, BAND:BAND + ls] = k_ref[0]
    vpad_ref[:, BAND:BAND + ls] = v_ref[0]

    lane = lax.broadcasted_iota(jnp.int32, (Q_TILE, LANES), 1)
    low_half = lane < HEAD_DIM

    def tile(t, carry):
        r = t // tiles_per_res
        jt = t % tiles_per_res
        j0 = pl.multiple_of(jt * Q_TILE, Q_TILE)
        edge = (jt == 0).astype(jnp.int32) + 2 * (jt == tiles_per_res - 1).astype(jnp.int32)
        lse_tile = jnp.zeros((Q_TILE, LANES), F32)
        for hp in range(HEADS // 2):
            cols = slice(hp * LANES, (hp + 1) * LANES)
            q2 = q_ref[0, r, pl.ds(j0, Q_TILE), cols]
            kw = kpad_ref[r, pl.ds(j0, K_WIN), cols]
            vw = vpad_ref[r, pl.ds(j0, K_WIN), cols]
            outs = []
            for half in range(2):
                h = 2 * hp + half
                qh = jnp.where(low_half if half == 0 else ~low_half, q2, 0.0).astype(BF16)
                s = lax.dot_general(qh, kw, (((1,), (1,)), ((), ())), preferred_element_type=F32)
                s = s + bias_ref[edge, h]
                m = jnp.max(s, axis=-1, keepdims=True)
                p = jnp.exp2(s - m)
                den = jnp.sum(p, axis=-1, keepdims=True)
                pv = jnp.dot(p.astype(BF16), vw, preferred_element_type=F32)
                outs.append(pv)
                lse_tile = jnp.where(lane == h, m, jnp.where(lane == HEADS + h, den, lse_tile))
            o_ref[0, r, pl.ds(j0, Q_TILE), cols] = jnp.where(low_half, outs[0], outs[1]).astype(BF16)
        l_ref[0, r, pl.ds(j0, Q_TILE), :] = lse_tile
        return carry

    lax.fori_loop(0, dil * tiles_per_res, tile, 0, unroll=ATTN_UNROLL)


def _attn_group(qkv_g, tbl, group, dilation):
    nb, _, ls, _ = qkv_g.shape
    part = lambda which: pl.BlockSpec((1, dilation, ls, ATTN_WIDTH), lambda b: (b, 0, 0, which))
    bkt = jnp.asarray(_bucket_matrix(dilation))
    return pl.pallas_call(
        functools.partial(_attn_body, group),
        out_shape=(
            jax.ShapeDtypeStruct((nb, dilation, ls, ATTN_WIDTH), BF16),
            jax.ShapeDtypeStruct((nb, dilation, ls, LANES), F32),
        ),
        grid=(nb,),
        in_specs=[
            pl.BlockSpec(memory_space=pltpu.SMEM),
            _const_spec((Q_TILE, K_WIN)),
            part(0), part(1), part(2),
        ],
        out_specs=(
            pl.BlockSpec((1, dilation, ls, ATTN_WIDTH), lambda b: (b, 0, 0, 0)),
            pl.BlockSpec((1, dilation, ls, LANES), lambda b: (b, 0, 0, 0)),
        ),
        scratch_shapes=[
            pltpu.VMEM((4, HEADS, Q_TILE, K_WIN), F32),
            pltpu.VMEM((dilation, ls + 2 * BAND, ATTN_WIDTH), BF16),
            pltpu.VMEM((dilation, ls + 2 * BAND, ATTN_WIDTH), BF16),
        ],
        compiler_params=pltpu.CompilerParams(
            dimension_semantics=("arbitrary",), vmem_limit_bytes=VMEM_LIMIT_BYTES),
        name=f"attn_g{group}",
    )(tbl, bkt, qkv_g, qkv_g, qkv_g)


def _rglru_body(nt, cw_ref, cb_ref, wf_ref, wb_ref, bf_ref, bb_ref, lam_ref,
                xf_ref, pf_ref, nf_ref, xb_ref, pb_ref, nb_ref,
                hf_ref, hb_ref, xw_ref, a_ref, u_ref, hs_ref, cf_ref, cbk_ref):
    i = pl.program_id(0)
    nbat, tt, width = xf_ref.shape
    rows = tt * nbat
    n_slab = width // LANES
    halo_rows = HALO_T * nbat

    @pl.when(i == 0)
    def _():
        cf_ref[...] = jnp.zeros_like(cf_ref)
        cbk_ref[...] = jnp.zeros_like(cbk_ref)

    def to_time_major(src_ref, n_t, row0, keep):
        for b in range(nbat):
            x = src_ref[b].astype(F32)
            x = x if keep is None else jnp.where(keep, x, 0.0)
            for s, slab in enumerate(_lane_slabs(x)):
                xw_ref.at[s][pl.ds(row0 + b, n_t, stride=nbat), :] = slab

    def gates(x_ref, p_ref, n_ref, tile_idx, w_ref, b_ref, lam):
        to_time_major(p_ref, HALO_T, 0, tile_idx > 0)
        to_time_major(x_ref, tt, halo_rows, None)
        to_time_major(n_ref, HALO_T, halo_rows + rows, tile_idx < nt - 1)
        cw = cw_ref[...]

        def tap(k):
            start = halo_rows + (k - 2) * nbat
            return jnp.concatenate([xw_ref[s, start:start + rows, :] for s in range(n_slab)], axis=1)

        xc = cw[0:1] * tap(0) + cw[1:2] * tap(1) + cw[2:3] * tap(2) + cw[3:4] * tap(3) + cb_ref[...]
        g = jnp.dot(xc.astype(BF16), w_ref[...], preferred_element_type=F32) + b_ref[...]
        r = _sigmoid(g[:, :width])
        gi = _sigmoid(g[:, width:])
        softplus = jnp.maximum(-lam, 0.0) + jnp.log(1.0 + jnp.exp(-jnp.abs(lam)))
        log_a = (-RG_C * softplus) * r
        a = jnp.exp(log_a)
        a_ref[...] = a
        u_ref[...] = jnp.sqrt(1.0 - a * a) * (gi * xc)

    def step(t, h):
        r0 = pl.multiple_of(t * nbat, nbat)
        h = a_ref[pl.ds(r0, nbat), :] * h + u_ref[pl.ds(r0, nbat), :]
        for s, slab in enumerate(_lane_slabs(h)):
            hs_ref[s, pl.ds(r0, nbat), :] = slab
        return h

    def to_batch_major(dst_ref):
        for b in range(nbat):
            dst_ref[b] = jnp.concatenate(
                [hs_ref.at[s][pl.ds(b, tt, stride=nbat), :] for s in range(n_slab)], axis=1).astype(BF16)

    gates(xf_ref, pf_ref, nf_ref, i, wf_ref, bf_ref, lam_ref[0:1])
    cf_ref[...] = lax.fori_loop(0, tt, step, cf_ref[...], unroll=8)
    to_batch_major(hf_ref)

    gates(xb_ref, pb_ref, nb_ref, nt - 1 - i, wb_ref, bb_ref, lam_ref[1:2])
    cbk_ref[...] = lax.fori_loop(0, tt, lambda s, h: step(tt - 1 - s, h), cbk_ref[...], unroll=8)
    to_batch_major(hb_ref)


def _rglru(rg, conv_w, conv_b, w_f, w_b, b_f, b_b, lam):
    nbat, seq, _ = rg.shape
    nt = seq // RG_TT
    hpt = RG_TT // HALO_T
    nh = seq // HALO_T
    cur = lambda f: pl.BlockSpec((nbat, RG_TT, RG_WIDTH), lambda i: (0, f(i), 0))
    prv = lambda f: pl.BlockSpec(
        (nbat, HALO_T, RG_WIDTH), lambda i: (0, jnp.maximum(f(i) * hpt - 1, 0), 0))
    nxt = lambda f: pl.BlockSpec(
        (nbat, HALO_T, RG_WIDTH), lambda i: (0, jnp.minimum((f(i) + 1) * hpt, nh - 1), 0))
    fw = lambda i: i
    bw = lambda i: nt - 1 - i
    n_slab = RG_WIDTH // LANES
    return pl.pallas_call(
        functools.partial(_rglru_body, nt),
        out_shape=(
            jax.ShapeDtypeStruct((nbat, seq, RG_WIDTH), BF16),
            jax.ShapeDtypeStruct((nbat, seq, RG_WIDTH), BF16),
        ),
        grid=(nt,),
        in_specs=[
            _const_spec((RG_CONV, RG_WIDTH)), _const_spec((1, RG_WIDTH)),
            _const_spec((RG_WIDTH, 2 * RG_WIDTH)), _const_spec((RG_WIDTH, 2 * RG_WIDTH)),
            _const_spec((1, 2 * RG_WIDTH)), _const_spec((1, 2 * RG_WIDTH)),
            _const_spec((2, RG_WIDTH)),
            cur(fw), prv(fw), nxt(fw), cur(bw), prv(bw), nxt(bw),
        ],
        out_specs=(cur(fw), cur(bw)),
        scratch_shapes=[
            pltpu.VMEM((n_slab, (RG_TT + 2 * HALO_T) * nbat, LANES), F32),
            pltpu.VMEM((RG_TT * nbat, RG_WIDTH), F32),
            pltpu.VMEM((RG_TT * nbat, RG_WIDTH), F32),
            pltpu.VMEM((n_slab, RG_TT * nbat, LANES), F32),
            pltpu.VMEM((nbat, RG_WIDTH), F32),
            pltpu.VMEM((nbat, RG_WIDTH), F32),
        ],
        compiler_params=pltpu.CompilerParams(
            dimension_semantics=("arbitrary",), vmem_limit_bytes=VMEM_LIMIT_BYTES),
        name="rglru",
    )(conv_w, conv_b, w_f, w_b, b_f, b_b, lam, rg, rg, rg, rg, rg, rg)


def _block_diag(w):
    eye = jnp.eye(RG_BLOCKS, dtype=w.dtype)
    return jnp.einsum("hij,hk->hikj", w, eye).reshape(RG_WIDTH, RG_WIDTH)


def _merge_body(h_ref, g_ref, wg_ref, bg_ref, ya_ref, o0_ref, o1_ref, o2_ref,
                l0_ref, l1_ref, l2_ref, hf_ref, hb_ref, gate_ref, exp_ref,
                wphy_ref, wpat_ref, wprg_ref, wout_ref, out_ref, os_ref, ls_ref):
    x = h_ref[0]
    xb = _rms(x, g_ref[...]).astype(BF16)
    gates = _sigmoid(jnp.dot(xb, wg_ref[...], preferred_element_type=F32) + bg_ref[...])

    def token_order(o_ref, l_ref):
        d, n = o_ref.shape[1], o_ref.shape[2]
        if d == 1:
            return o_ref[0, 0].astype(F32), l_ref[0, 0]
        for r in range(d):
            for s, slab in enumerate(_lane_slabs(o_ref[0, r].astype(F32))):
                os_ref.at[s][pl.ds(r, n, stride=d), :] = slab
            ls_ref[pl.ds(r, n, stride=d), :] = l_ref[0, r]
        o = jnp.concatenate([os_ref[s] for s in range(ATTN_WIDTH // LANES)], axis=1)
        return o, ls_ref[...]

    o0, l0 = token_order(o0_ref, l0_ref)
    o1, l1 = token_order(o1_ref, l1_ref)
    o2, l2 = token_order(o2_ref, l2_ref)

    head_lane = lax.broadcasted_iota(jnp.int32, l0.shape, 1) < HEADS
    d0, d1, d2 = (pltpu.roll(l, LANES - HEADS, 1) for l in (l0, l1, l2))
    m = jnp.maximum(jnp.maximum(l0, l1), l2)
    e0, e1, e2 = jnp.exp2(l0 - m), jnp.exp2(l1 - m), jnp.exp2(l2 - m)
    inv = 1.0 / jnp.where(head_lane, e0 * d0 + e1 * d1 + e2 * d2, 1.0)

    def widen(w):
        hi = w.astype(BF16)
        lo = (w - hi.astype(F32)).astype(BF16)
        return (jnp.dot(hi, exp_ref[...], preferred_element_type=F32)
                + jnp.dot(lo, exp_ref[...], preferred_element_type=F32))

    y_b = widen(e0 * inv) * o0 + widen(e1 * inv) * o1 + widen(e2 * inv) * o2

    gt = gate_ref[0].astype(F32)
    gelu = 0.5 * gt * (1.0 + jnp.tanh(math.sqrt(2.0 / math.pi) * (gt + 0.044715 * (gt * gt * gt))))
    y_c = (hf_ref[0].astype(F32) + hb_ref[0].astype(F32)) * gelu

    merged = (gates[:, :D_MODEL]
              * jnp.dot(ya_ref[0], wphy_ref[...], preferred_element_type=F32)
              + gates[:, D_MODEL:2 * D_MODEL]
              * jnp.dot(y_b.astype(BF16), wpat_ref[...], preferred_element_type=F32)
              + gates[:, 2 * D_MODEL:]
              * jnp.dot(y_c.astype(BF16), wprg_ref[...], preferred_element_type=F32))
    out_ref[0] = x + jnp.dot(merged.astype(BF16), wout_ref[...], preferred_element_type=F32)


def _merge(h3, g, w_gate, b_gate, y_a, attn, hf, hb, rg, expand, wp_hy, wp_attn, wp_rg, w_out):
    nb, seq, _ = h3.shape
    tok = lambda width, col=0: pl.BlockSpec((1, TOKEN_TILE, width), lambda b, i: (b, i, col))
    res = lambda d, width: pl.BlockSpec((1, d, TOKEN_TILE // d, width), lambda b, i: (b, 0, i, 0))
    (o0, l0), (o1, l1), (o2, l2) = attn
    dils = [d for _, d in ATTN_GROUPS]
    return pl.pallas_call(
        _merge_body,
        out_shape=jax.ShapeDtypeStruct(h3.shape, F32),
        grid=(nb, seq // TOKEN_TILE),
        in_specs=[
            tok(D_MODEL), _const_spec((1, D_MODEL)),
            _const_spec((D_MODEL, N_BRANCH * D_MODEL)), _const_spec((1, N_BRANCH * D_MODEL)),
            tok(HY_WIDTH),
            res(dils[0], ATTN_WIDTH), res(dils[1], ATTN_WIDTH), res(dils[2], ATTN_WIDTH),
            res(dils[0], LANES), res(dils[1], LANES), res(dils[2], LANES),
            tok(RG_WIDTH), tok(RG_WIDTH), tok(RG_WIDTH, 1),
            _const_spec((LANES, ATTN_WIDTH)),
            _const_spec((HY_WIDTH, D_MODEL)), _const_spec((ATTN_WIDTH, D_MODEL)),
            _const_spec((RG_WIDTH, D_MODEL)), _const_spec((D_MODEL, D_MODEL)),
        ],
        out_specs=tok(D_MODEL),
        scratch_shapes=[
            pltpu.VMEM((ATTN_WIDTH // LANES, TOKEN_TILE, LANES), F32),
            pltpu.VMEM((TOKEN_TILE, LANES), F32),
        ],
        compiler_params=pltpu.CompilerParams(
            dimension_semantics=("arbitrary", "arbitrary"), vmem_limit_bytes=VMEM_LIMIT_BYTES),
        name="merge",
    )(h3, g, w_gate, b_gate, y_a, o0, o1, o2, l0, l1, l2, hf, hb, rg, expand,
      wp_hy, wp_attn, wp_rg, w_out)


def _mixer(h3, l, p, consts):
    fre, fim, zpos, deltas, expand = consts
    w_in = p["w_in"][l].astype(BF16)
    w_qkv = p["w_in"][l][:, HY_COLS:HY_COLS + QKV_COLS].reshape(D_MODEL, 3, N_GROUPS, ATTN_WIDTH)
    qkv_scale = jnp.array([HEAD_DIM ** -0.5 * LOG2E, 1.0, 1.0], F32).reshape(1, 3, 1, 1)
    w_qkv = (w_qkv * qkv_scale).astype(BF16)
    w_q = [w_qkv[:, :, g, :].reshape(D_MODEL, 3 * ATTN_WIDTH) for g in range(N_GROUPS)]
    hy, q0, q1, q2, rg = _inproj(
        h3, p["mix_norm"][l].reshape(1, -1), w_in[:, :HY_COLS], w_q, w_in[:, HY_COLS + QKV_COLS:])

    hre, him, hny = _hy_filter(
        zpos, p["hy_w1"][l], p["hy_b1"][l], p["hy_w2"][l], p["hy_b2"][l], p["hy_w3"][l],
        p["hy_b3"][l], p["hy_freq"][l], p["hy_wout"][l], deltas, fre, fim)
    y_a = _hy_conv(hy, p["hy_conv_w"][l], p["hy_conv_b"][l].reshape(1, -1),
                   fre, fim, hre, him, hny, p["hy_skip"][l])

    attn = [_attn_group(q, p["rel_bias"], g, dil)
            for g, (q, (_, dil)) in enumerate(zip((q0, q1, q2), ATTN_GROUPS))]

    rg_w = lambda d: jnp.concatenate(
        [_block_diag(p["rg_wa"][l, d]), _block_diag(p["rg_wx"][l, d])], axis=1).astype(BF16)
    rg_b = lambda d: jnp.concatenate([p["rg_ba"][l, d], p["rg_bx"][l, d]]).reshape(1, -1)
    hf, hb = _rglru(rg, p["rg_conv_w"][l], p["rg_conv_b"][l].reshape(1, -1),
                    rg_w(0), rg_w(1), rg_b(0), rg_b(1), p["rg_lambda"][l])

    return _merge(
        h3, p["mix_norm"][l].reshape(1, -1), p["w_gate"][l].astype(BF16),
        p["b_gate"][l].reshape(1, -1), y_a, attn, hf, hb, rg, expand,
        p["w_proj_hy"][l].astype(BF16), p["w_proj_attn"][l].astype(BF16),
        p["w_proj_rg"][l].astype(BF16), p["w_out"][l].astype(BF16))


def _forward(x, p):
    nb, seq, _ = x.shape
    fre, fim = _dft_mats(seq)
    deltas = jnp.abs(jnp.linspace(math.log(HY_DECAY_TARGET) / HY_FAST_DECAY,
                                  math.log(HY_DECAY_TARGET) / HY_SLOW_DECAY, HY_WIDTH, dtype=F32))
    head_of_lane = jnp.arange(ATTN_WIDTH, dtype=jnp.int32) // HEAD_DIM
    expand = (jnp.arange(LANES, dtype=jnp.int32)[:, None] == head_of_lane[None, :]).astype(BF16)
    consts = (fre, fim, _hy_positions(seq), deltas, expand)

    def ffn(h3, norm, wg, wu, wd, l, g_out=None):
        wgu = jnp.concatenate([wg[l], wu[l]], axis=1).astype(BF16)
        out = _ffn(h3.reshape(nb * seq, D_MODEL), norm[l].reshape(1, -1), wgu, wd[l].astype(BF16), g_out)
        return out.reshape(nb, seq, D_MODEL)

    h = x
    for l in range(DEPTH):
        h = ffn(h, p["ffn1_norm"], p["ffn1_wg"], p["ffn1_wu"], p["ffn1_wd"], l)
        h = _mixer(h, l, p, consts)
        g_out = p["final_norm"].reshape(1, -1) if l == DEPTH - 1 else None
        h = ffn(h, p["ffn2_norm"], p["ffn2_wg"], p["ffn2_wu"], p["ffn2_wd"], l, g_out)
    return h


def kernel(x, ffn1_norm, ffn1_wg, ffn1_wu, ffn1_wd, mix_norm, w_in, hy_conv_w, hy_conv_b, hy_w1, hy_b1, hy_w2, hy_b2, hy_w3, hy_b3, hy_freq, hy_wout, hy_skip, rel_bias, rg_conv_w, rg_conv_b, rg_wa, rg_ba, rg_wx, rg_bx, rg_lambda, w_gate, b_gate, w_proj_hy, w_proj_attn, w_proj_rg, w_out, ffn2_norm, ffn2_wg, ffn2_wu, ffn2_wd, final_norm):
    p = dict(
        ffn1_norm=ffn1_norm, ffn1_wg=ffn1_wg, ffn1_wu=ffn1_wu, ffn1_wd=ffn1_wd, mix_norm=mix_norm,
        w_in=w_in, hy_conv_w=hy_conv_w, hy_conv_b=hy_conv_b, hy_w1=hy_w1, hy_b1=hy_b1, hy_w2=hy_w2,
        hy_b2=hy_b2, hy_w3=hy_w3, hy_b3=hy_b3, hy_freq=hy_freq, hy_wout=hy_wout, hy_skip=hy_skip,
        rel_bias=rel_bias, rg_conv_w=rg_conv_w, rg_conv_b=rg_conv_b, rg_wa=rg_wa, rg_ba=rg_ba,
        rg_wx=rg_wx, rg_bx=rg_bx, rg_lambda=rg_lambda, w_gate=w_gate, b_gate=b_gate,
        w_proj_hy=w_proj_hy, w_proj_attn=w_proj_attn, w_proj_rg=w_proj_rg, w_out=w_out,
        ffn2_norm=ffn2_norm, ffn2_wg=ffn2_wg, ffn2_wu=ffn2_wu, ffn2_wd=ffn2_wd,
        final_norm=final_norm)
    return _forward(x, p)
```

```python
import functools
import math

import numpy as np
import jax
import jax.numpy as jnp
from jax import lax
from jax.experimental import pallas as pl
from jax.experimental.pallas import tpu as pltpu

F32 = jnp.float32
BF16 = jnp.bfloat16

D_MODEL = 1024
D_FF = 2816
DEPTH = 4
RMS_EPS = 1e-6

HY_WIDTH = 512
HY_ORDER = 2
HY_EMB = 33
HY_BANDS = 16
HY_HIDDEN = 64
HY_COLS = 3 * HY_WIDTH
HY_DECAY_TARGET = 1e-2
HY_FAST_DECAY = 0.3
HY_SLOW_DECAY = 1.5
HY_MOD_SHIFT = 0.05
HY_NORM_EPS = 1e-6

ATTN_GROUPS = ((128, 1), (512, 4), (2048, 16))
N_GROUPS = 3
HEADS = 8
HEAD_DIM = 64
ATTN_WIDTH = HEADS * HEAD_DIM
QKV_COLS = 3 * N_GROUPS * ATTN_WIDTH
BAND = 64
N_BUCKETS = 32
BUCKET_MAX_EXACT = 8
BUCKET_MAX_DIST = 1024
NEG_INF = -1e30
LOG2E = math.log2(math.e)
LN2 = math.log(2.0)

RG_WIDTH = 512
RG_BLOCKS = 8
RG_BLOCK = 64
RG_CONV = 4
RG_C = 8.0
RG_COLS = 2 * RG_WIDTH

N_BRANCH = 3

LANES = 128
VMEM_LIMIT_BYTES = 56 * 1024 * 1024

TOKEN_TILE = 512
Q_TILE = 128
K_WIN = Q_TILE + 2 * BAND
ATTN_UNROLL = 4
HY_CBLK = 256
HY_P = 512
RG_TT = 128
HALO_T = 8


def _const_spec(shape):
    nd = len(shape)
    return pl.BlockSpec(shape, lambda *_: (0,) * nd, pipeline_mode=pl.Buffered(1))


def _rms(x, g):
    ms = jnp.mean(x * x, axis=-1, keepdims=True)
    return x * lax.rsqrt(ms + RMS_EPS) * g


def _sigmoid(x):
    return 1.0 / (1.0 + jnp.exp(-x))


def _lane_slabs(x):
    return [x[:, s * LANES:(s + 1) * LANES] for s in range(x.shape[1] // LANES)]


def _ffn_body(h_ref, g_ref, wgu_ref, wd_ref, *rest):
    gout_ref, o_ref = rest if len(rest) == 2 else (None, rest[0])
    x = h_ref[...]
    xb = _rms(x, g_ref[...]).astype(BF16)
    gu = jnp.dot(xb, wgu_ref[...], preferred_element_type=F32)
    gate = gu[:, :D_FF]
    up = gu[:, D_FF:]
    act = (gate * _sigmoid(gate)) * up
    out = x + 0.5 * jnp.dot(act.astype(BF16), wd_ref[...], preferred_element_type=F32)
    o_ref[...] = out if gout_ref is None else _rms(out, gout_ref[...])


def _ffn(h, g, wgu, wd, g_out=None):
    n_tok = h.shape[0]
    extra = [] if g_out is None else [g_out]
    return pl.pallas_call(
        _ffn_body,
        out_shape=jax.ShapeDtypeStruct(h.shape, F32),
        grid=(n_tok // TOKEN_TILE,),
        in_specs=[
            pl.BlockSpec((TOKEN_TILE, D_MODEL), lambda i: (i, 0)),
            _const_spec((1, D_MODEL)),
            _const_spec((D_MODEL, 2 * D_FF)),
            _const_spec((D_FF, D_MODEL)),
        ] + [_const_spec((1, D_MODEL))] * len(extra),
        out_specs=pl.BlockSpec((TOKEN_TILE, D_MODEL), lambda i: (i, 0)),
        compiler_params=pltpu.CompilerParams(
            dimension_semantics=("arbitrary",), vmem_limit_bytes=VMEM_LIMIT_BYTES),
        name="ffn",
    )(h, g, wgu, wd, *extra)


def _inproj_body(h_ref, g_ref, why_ref, wq0_ref, wq1_ref, wq2_ref, wrg_ref,
                 hy_ref, q0_ref, q1_ref, q2_ref, rg_ref, xs_ref):
    xn = _rms(h_ref[0], g_ref[...])
    xb = xn.astype(BF16)
    hy_ref[0] = jnp.dot(xb, why_ref[...], preferred_element_type=F32).astype(BF16)
    q0_ref[0, 0] = jnp.dot(xb, wq0_ref[...], preferred_element_type=F32).astype(BF16)
    rg_ref[0] = jnp.dot(xb, wrg_ref[...], preferred_element_type=F32).astype(BF16)

    for s, slab in enumerate(_lane_slabs(xn)):
        xs_ref[s] = slab

    def by_residue(d):
        n = TOKEN_TILE // d
        cols = [jnp.concatenate([xs_ref.at[s][pl.ds(r, n, stride=d), :] for r in range(d)], axis=0)
                for s in range(D_MODEL // LANES)]
        return jnp.concatenate(cols, axis=1).astype(BF16)

    for q_ref, w_ref, (_, d) in ((q1_ref, wq1_ref, ATTN_GROUPS[1]), (q2_ref, wq2_ref, ATTN_GROUPS[2])):
        out = jnp.dot(by_residue(d), w_ref[...], preferred_element_type=F32).astype(BF16)
        q_ref[0] = out.reshape(d, TOKEN_TILE // d, 3 * ATTN_WIDTH)


def _inproj(h3, g, w_hy, w_q, w_rg):
    nb, seq, _ = h3.shape
    dils = [d for _, d in ATTN_GROUPS]
    qshape = lambda d: jax.ShapeDtypeStruct((nb, d, seq // d, 3 * ATTN_WIDTH), BF16)
    qspec = lambda d: pl.BlockSpec((1, d, TOKEN_TILE // d, 3 * ATTN_WIDTH), lambda b, i: (b, 0, i, 0))
    return pl.pallas_call(
        _inproj_body,
        out_shape=(
            jax.ShapeDtypeStruct((nb, seq, HY_COLS), BF16),
            qshape(dils[0]), qshape(dils[1]), qshape(dils[2]),
            jax.ShapeDtypeStruct((nb, seq, RG_COLS), BF16),
        ),
        grid=(nb, seq // TOKEN_TILE),
        in_specs=[
            pl.BlockSpec((1, TOKEN_TILE, D_MODEL), lambda b, i: (b, i, 0)),
            _const_spec((1, D_MODEL)),
            _const_spec((D_MODEL, HY_COLS)),
            _const_spec((D_MODEL, 3 * ATTN_WIDTH)),
            _const_spec((D_MODEL, 3 * ATTN_WIDTH)),
            _const_spec((D_MODEL, 3 * ATTN_WIDTH)),
            _const_spec((D_MODEL, RG_COLS)),
        ],
        out_specs=(
            pl.BlockSpec((1, TOKEN_TILE, HY_COLS), lambda b, i: (b, i, 0)),
            qspec(dils[0]), qspec(dils[1]), qspec(dils[2]),
            pl.BlockSpec((1, TOKEN_TILE, RG_COLS), lambda b, i: (b, i, 0)),
        ),
        scratch_shapes=[pltpu.VMEM((D_MODEL // LANES, TOKEN_TILE, LANES), F32)],
        compiler_params=pltpu.CompilerParams(
            dimension_semantics=("arbitrary", "arbitrary"), vmem_limit_bytes=VMEM_LIMIT_BYTES),
        name="inproj",
    )(h3, g, w_hy, w_q[0], w_q[1], w_q[2], w_rg)


def _dft_mats(half):
    n = jnp.arange(half, dtype=jnp.int32)
    kn = (n[:, None] * n[None, :]) % (2 * half)
    ang = kn.astype(F32) * (math.pi / half)
    return jnp.cos(ang).astype(BF16), (-jnp.sin(ang)).astype(BF16)


def _hy_positions(seq):
    t = jnp.linspace(0.0, 1.0, seq, dtype=F32)[:, None]
    tr = jnp.arange(seq, dtype=F32)[:, None]
    wpos = 2.0 * math.pi * tr / seq
    fb = jnp.linspace(1e-4, HY_BANDS - 1, HY_BANDS, dtype=F32)[None, :]
    z = jnp.concatenate([t, jnp.cos(fb * wpos), -jnp.sin(fb * wpos)], axis=-1)
    return jnp.pad(z, ((0, 0), (0, LANES - HY_EMB)))


def _hy_filter_body(z_ref, w1_ref, b1_ref, w2_ref, b2_ref, w3_ref, b3_ref, fr_ref,
                    wf_ref, wb_ref, dl_ref, fre_ref, fim_ref,
                    g_ref, gn_ref):
    hi = lax.Precision.HIGHEST
    seq = z_ref.shape[0]
    fr = fr_ref[...]
    hdn = jnp.sin(fr * (jnp.dot(z_ref[...], w1_ref[...], precision=hi,
                                preferred_element_type=F32) + b1_ref[...]))
    hdn = jnp.sin(fr * (jnp.dot(hdn, w2_ref[...], precision=hi,
                                preferred_element_type=F32) + b2_ref[...]))
    hdn = jnp.sin(fr * (jnp.dot(hdn, w3_ref[...], precision=hi,
                                preferred_element_type=F32) + b3_ref[...]))
    row = lax.broadcasted_iota(jnp.int32, (seq, HY_WIDTH), 0)
    t = row.astype(F32) * (1.0 / (seq - 1))
    decay = jnp.exp(-t * dl_ref[...]) + HY_MOD_SHIFT
    kf = jnp.dot(hdn, wf_ref[0], precision=hi, preferred_element_type=F32) * decay
    kb = jnp.dot(hdn, wb_ref[0], precision=hi, preferred_element_type=F32) * decay
    kb = jnp.where(row == 0, 0.0, kb)
    norm = (jnp.sum(jnp.abs(kf), axis=0, keepdims=True)
            + jnp.sum(jnp.abs(kb), axis=0, keepdims=True) + HY_NORM_EPS)
    kf = kf / norm
    kb = kb / norm

    nseg = seq // HY_P
    krow = lax.broadcasted_iota(jnp.int32, (HY_P, HY_WIDTH), 0)
    sk = (1 - 2 * (krow & 1)).astype(F32)
    ck = jnp.where(krow == 0, 0.5 / HY_P, sk * (1.0 / HY_P))
    zero_row = jnp.zeros((1, HY_WIDTH), F32)

    def dft(x):
        xb = x.astype(BF16)
        return (jnp.dot(fre_ref[...], xb, preferred_element_type=F32),
                jnp.dot(fim_ref[...], xb, preferred_element_type=F32),
                jnp.sum(x * sk, axis=0, keepdims=True))

    spectra = {}
    for e in range(nseg):
        spectra[e] = dft(kf[e * HY_P:(e + 1) * HY_P])
        lead = kb[e * HY_P:(e + 1) * HY_P]
        bre, bim, bny = dft(jnp.where(krow == 0, 0.0, lead))
        first = kb[(e + 1) * HY_P:(e + 1) * HY_P + 1] if e + 1 < nseg else zero_row
        spectra[-(e + 1)] = (first + sk * bre, -(sk * bim), first + bny)

    for delta in range(-(nseg - 1), nseg):
        are, aim, any_ = spectra[delta - 1]
        bre, bim, bny = spectra[delta]
        g_ref[0, delta + nseg - 1, 0] = (are + sk * bre) * ck
        g_ref[0, delta + nseg - 1, 1] = (aim + sk * bim) * ck
        gn_ref[0, delta + nseg - 1:delta + nseg, :] = (any_ + bny) * (0.5 / HY_P)
    gn_ref[0, 2 * nseg - 1:2 * nseg, :] = zero_row


def _hy_filter(z, w1, b1, w2, b2, w3, b3, freq, wout, deltas, fre, fim):
    seq = z.shape[0]
    wf = wout[:, :HY_ORDER * HY_WIDTH].reshape(HY_HIDDEN, HY_ORDER, HY_WIDTH).transpose(1, 0, 2)
    wb = wout[:, HY_ORDER * HY_WIDTH:].reshape(HY_HIDDEN, HY_ORDER, HY_WIDTH).transpose(1, 0, 2)
    w1p = jnp.pad(w1, ((0, LANES - HY_EMB), (0, 0)))
    row = lambda a: a.reshape(1, -1)
    nseg = seq // HY_P
    return pl.pallas_call(
        _hy_filter_body,
        out_shape=(
            jax.ShapeDtypeStruct((HY_ORDER, 2 * nseg - 1, 2, HY_P, HY_WIDTH), F32),
            jax.ShapeDtypeStruct((HY_ORDER, 2 * nseg, HY_WIDTH), F32),
        ),
        grid=(HY_ORDER,),
        in_specs=[
            _const_spec((seq, LANES)),
            _const_spec((LANES, HY_HIDDEN)), _const_spec((1, HY_HIDDEN)),
            _const_spec((HY_HIDDEN, HY_HIDDEN)), _const_spec((1, HY_HIDDEN)),
            _const_spec((HY_HIDDEN, HY_HIDDEN)), _const_spec((1, HY_HIDDEN)),
            _const_spec((1, HY_HIDDEN)),
            pl.BlockSpec((1, HY_HIDDEN, HY_WIDTH), lambda o: (o, 0, 0)),
            pl.BlockSpec((1, HY_HIDDEN, HY_WIDTH), lambda o: (o, 0, 0)),
            _const_spec((1, HY_WIDTH)),
            _const_spec((HY_P, HY_P)), _const_spec((HY_P, HY_P)),
        ],
        out_specs=(
            pl.BlockSpec((1, 2 * nseg - 1, 2, HY_P, HY_WIDTH), lambda o: (o, 0, 0, 0, 0)),
            pl.BlockSpec((1, 2 * nseg, HY_WIDTH), lambda o: (o, 0, 0)),
        ),
        compiler_params=pltpu.CompilerParams(
            dimension_semantics=("arbitrary",), vmem_limit_bytes=VMEM_LIMIT_BYTES),
        name="hy_filter",
    )(z, w1p, row(b1), w2, row(b2), w3, row(b3), row(freq), wf, wb, row(deltas), fre, fim)


def _hy_conv_body(v_ref, x1_ref, x2_ref, wv_ref, w1_ref, w2_ref, bv_ref, b1_ref, b2_ref,
                  fre_ref, fim_ref, g_ref, gn_ref, skip_ref, o_ref,
                  z_ref, g1_ref, g2_ref, ure_ref, uim_ref):
    seq = v_ref.shape[1]
    nseg = seq // HY_P
    row = lax.broadcasted_iota(jnp.int32, (seq, HY_CBLK), 0)
    sgn = (1 - 2 * (lax.broadcasted_iota(jnp.int32, (HY_P, HY_CBLK), 0) & 1)).astype(F32)

    def short_conv(x_ref, w_ref, b_ref):
        x = x_ref[0].astype(F32)
        w = w_ref[...]
        prev = jnp.where(row == 0, 0.0, pltpu.roll(x, 1, 0))
        nxt = jnp.where(row == seq - 1, 0.0, pltpu.roll(x, seq - 1, 0))
        return w[0:1] * prev + w[1:2] * x + w[2:3] * nxt + b_ref[...]

    z_ref[...] = short_conv(v_ref, wv_ref, bv_ref)
    g1_ref[...] = short_conv(x1_ref, w1_ref, b1_ref)
    g2_ref[...] = short_conv(x2_ref, w2_ref, b2_ref)
    skip = skip_ref[...]
    blk = lambda i: slice(i * HY_P, (i + 1) * HY_P)

    for o, gate_ref in enumerate((g1_ref, g2_ref)):
        nyq = []
        for j in range(nseg):
            x = z_ref[blk(j), :]
            xb = x.astype(BF16)
            ure_ref[j] = jnp.dot(fre_ref[...], xb, preferred_element_type=F32)
            uim_ref[j] = jnp.dot(fim_ref[...], xb, preferred_element_type=F32)
            nyq.append(jnp.sum(x * sgn, axis=0, keepdims=True))
        for i in range(nseg):
            zre = zim = zny = None
            for j in range(nseg):
                d = i - j + nseg - 1
                gre, gim = g_ref[o, d, 0], g_ref[o, d, 1]
                ure, uim = ure_ref[j], uim_ref[j]
                tre = gre * ure - gim * uim
                tim = gre * uim + gim * ure
                tny = gn_ref[o, d:d + 1, :] * nyq[j]
                zre, zim, zny = (tre, tim, tny) if zre is None else (zre + tre, zim + tim, zny + tny)
            y = (jnp.dot(fre_ref[...], zre.astype(BF16), preferred_element_type=F32)
                 + jnp.dot(fim_ref[...], zim.astype(BF16), preferred_element_type=F32)
                 + sgn * zny)
            z = gate_ref[blk(i), :] * (y + skip[o:o + 1] * z_ref[blk(i), :])
            if o + 1 < HY_ORDER:
                z_ref[blk(i), :] = z
            else:
                o_ref[0, blk(i), :] = z.astype(BF16)


def _hy_conv(u_hy, conv_w, conv_b, fre, fim, g, gn, skip):
    nb, seq, _ = u_hy.shape
    ncb = HY_WIDTH // HY_CBLK
    nseg = seq // HY_P
    data = lambda part: pl.BlockSpec((1, seq, HY_CBLK), lambda c, b: (b, 0, part * ncb + c))
    wspec = lambda part: pl.BlockSpec((3, HY_CBLK), lambda c, b: (0, part * ncb + c))
    bspec = lambda part: pl.BlockSpec((1, HY_CBLK), lambda c, b: (0, part * ncb + c))
    return pl.pallas_call(
        _hy_conv_body,
        out_shape=jax.ShapeDtypeStruct((nb, seq, HY_WIDTH), BF16),
        grid=(ncb, nb),
        in_specs=[
            data(0), data(1), data(2),
            wspec(0), wspec(1), wspec(2),
            bspec(0), bspec(1), bspec(2),
            _const_spec((HY_P, HY_P)), _const_spec((HY_P, HY_P)),
            pl.BlockSpec((HY_ORDER, 2 * nseg - 1, 2, HY_P, HY_CBLK), lambda c, b: (0, 0, 0, 0, c),
                         pipeline_mode=pl.Buffered(1)),
            pl.BlockSpec((HY_ORDER, 2 * nseg, HY_CBLK), lambda c, b: (0, 0, c)),
            pl.BlockSpec((HY_ORDER, HY_CBLK), lambda c, b: (0, c)),
        ],
        out_specs=pl.BlockSpec((1, seq, HY_CBLK), lambda c, b: (b, 0, c)),
        scratch_shapes=[
            pltpu.VMEM((seq, HY_CBLK), F32), pltpu.VMEM((seq, HY_CBLK), F32),
            pltpu.VMEM((seq, HY_CBLK), F32),
            pltpu.VMEM((nseg, HY_P, HY_CBLK), F32), pltpu.VMEM((nseg, HY_P, HY_CBLK), F32),
        ],
        compiler_params=pltpu.CompilerParams(
            dimension_semantics=("arbitrary", "arbitrary"), vmem_limit_bytes=VMEM_LIMIT_BYTES),
        name="hy_conv",
    )(u_hy, u_hy, u_hy, conv_w, conv_w, conv_w, conv_b, conv_b, conv_b,
      fre, fim, g, gn, skip)


def _bucket_matrix(dilation):
    qi = np.arange(Q_TILE, dtype=np.int64)[:, None]
    kj = np.arange(K_WIN, dtype=np.int64)[None, :]
    delta = kj - BAND - qi
    rel = delta * dilation
    half = N_BUCKETS // 2
    n = np.abs(rel)
    nf = np.maximum(n, 1).astype(np.float32)
    large = BUCKET_MAX_EXACT + (
        np.log(nf / np.float32(BUCKET_MAX_EXACT)) / np.float32(math.log(BUCKET_MAX_DIST / BUCKET_MAX_EXACT))
        * np.float32(half - BUCKET_MAX_EXACT)).astype(np.int32)
    large = np.minimum(large, half - 1)
    bucket = np.where(rel > 0, half, 0) + np.where(n < BUCKET_MAX_EXACT, n, large)
    return np.where(np.abs(delta) <= BAND, bucket, -1).astype(np.int32)


def _attn_body(group, tbl_ref, bkt_ref, q_ref, k_ref, v_ref, o_ref, l_ref,
               bias_ref, kpad_ref, vpad_ref):
    _, dil, ls, _ = q_ref.shape
    tiles_per_res = ls // Q_TILE

    kcol = lax.broadcasted_iota(jnp.int32, (Q_TILE, K_WIN), 1)

    @pl.when(pl.program_id(0) == 0)
    def _():
        bkt = bkt_ref[...]
        for h in range(HEADS):
            acc = jnp.full((Q_TILE, K_WIN), NEG_INF, F32)
            for bk in range(N_BUCKETS):
                acc = jnp.where(bkt == bk, tbl_ref[bk, group * HEADS + h] * LOG2E, acc)
            no_head = jnp.where(kcol < BAND, NEG_INF, acc)
            bias_ref[0, h] = acc
            bias_ref[1, h] = no_head
            bias_ref[2, h] = jnp.where(kcol >= Q_TILE + BAND, NEG_INF, acc)
            bias_ref[3, h] = jnp.where(kcol >= Q_TILE + BAND, NEG_INF, no_head)
        zeros = jnp.zeros((dil, BAND, ATTN_WIDTH), BF16)
        kpad_ref[:, 0:BAND] = zeros
        kpad_ref[:, BAND + ls:BAND + ls + BAND] = zeros
        vpad_ref[:, 0:BAND] = zeros
        vpad_ref[:, BAND + ls:BAND + ls + BAND] = zeros

    kpad_ref[:, BAND:BAND + ls] = k_ref[0]
    vpad_ref[:, BAND:BAND + ls] = v_ref[0]

    lane = lax.broadcasted_iota(jnp.int32, (Q_TILE, LANES), 1)
    low_half = lane < HEAD_DIM

    def tile(t, carry):
        r = t // tiles_per_res
        jt = t % tiles_per_res
        j0 = pl.multiple_of(jt * Q_TILE, Q_TILE)
        edge = jnp.where(jt == 0, 1, 0) + jnp.where(jt == tiles_per_res - 1, 2, 0)
        lse_tile = jnp.zeros((Q_TILE, LANES), F32)
        for hp in range(HEADS // 2):
            cols = slice(hp * LANES, (hp + 1) * LANES)
            q2 = q_ref[0, r, pl.ds(j0, Q_TILE), cols]
            kw = kpad_ref[r, pl.ds(j0, K_WIN), cols]
            vw = vpad_ref[r, pl.ds(j0, K_WIN), cols]
            outs = []
            for half in range(2):
                h = 2 * hp + half
                qh = jnp.where(low_half if half == 0 else ~low_half, q2, 0.0).astype(BF16)
                s = lax.dot_general(qh, kw, (((1,), (1,)), ((), ())), preferred_element_type=F32)
                s = s + bias_ref[edge, h]
                m = jnp.max(s, axis=-1, keepdims=True)
                p = jnp.exp2(s - m)
                den = jnp.sum(p, axis=-1, keepdims=True)
                pv = jnp.dot(p.astype(BF16), vw, preferred_element_type=F32)
                outs.append(pv)
                lse_tile = jnp.where(lane == h, m, jnp.where(lane == HEADS + h, den, lse_tile))
            o_ref[0, r, pl.ds(j0, Q_TILE), cols] = jnp.where(low_half, outs[0], outs[1]).astype(BF16)
        l_ref[0, r, pl.ds(j0, Q_TILE), :] = lse_tile
        return carry

    lax.fori_loop(0, dil * tiles_per_res, tile, 0, unroll=ATTN_UNROLL)


def _attn_group(qkv_g, tbl, group, dilation):
    nb, _, ls, _ = qkv_g.shape
    part = lambda which: pl.BlockSpec((1, dilation, ls, ATTN_WIDTH), lambda b: (b, 0, 0, which))
    bkt = jnp.asarray(_bucket_matrix(dilation))
    return pl.pallas_call(
        functools.partial(_attn_body, group),
        out_shape=(
            jax.ShapeDtypeStruct((nb, dilation, ls, ATTN_WIDTH), BF16),
            jax.ShapeDtypeStruct((nb, dilation, ls, LANES), F32),
        ),
        grid=(nb,),
        in_specs=[
            pl.BlockSpec(memory_space=pltpu.SMEM),
            _const_spec((Q_TILE, K_WIN)),
            part(0), part(1), part(2),
        ],
        out_specs=(
            pl.BlockSpec((1, dilation, ls, ATTN_WIDTH), lambda b: (b, 0, 0, 0)),
            pl.BlockSpec((1, dilation, ls, LANES), lambda b: (b, 0, 0, 0)),
        ),
        scratch_shapes=[
            pltpu.VMEM((4, HEADS, Q_TILE, K_WIN), F32),
            pltpu.VMEM((dilation, ls + 2 * BAND, ATTN_WIDTH), BF16),
            pltpu.VMEM((dilation, ls + 2 * BAND, ATTN_WIDTH), BF16),
        ],
        compiler_params=pltpu.CompilerParams(
            dimension_semantics=("arbitrary",), vmem_limit_bytes=VMEM_LIMIT_BYTES),
        name=f"attn_g{group}",
    )(tbl, bkt, qkv_g, qkv_g, qkv_g)


def _rglru_body(nt, cw_ref, cb_ref, wf_ref, wb_ref, bf_ref, bb_ref, lam_ref,
                xf_ref, pf_ref, nf_ref, xb_ref, pb_ref, nb_ref,
                hf_ref, hb_ref, xw_ref, a_ref, u_ref, hs_ref, cf_ref, cbk_ref):
    i = pl.program_id(0)
    nbat, tt, width = xf_ref.shape
    rows = tt * nbat
    n_slab = width // LANES
    halo_rows = HALO_T * nbat

    @pl.when(i == 0)
    def _():
        cf_ref[...] = jnp.zeros_like(cf_ref)
        cbk_ref[...] = jnp.zeros_like(cbk_ref)

    def to_time_major(src_ref, n_t, row0, keep):
        for b in range(nbat):
            x = src_ref[b].astype(F32)
            x = x if keep is None else jnp.where(keep, x, 0.0)
            for s, slab in enumerate(_lane_slabs(x)):
                xw_ref.at[s][pl.ds(row0 + b, n_t, stride=nbat), :] = slab

    def gates(x_ref, p_ref, n_ref, tile_idx, w_ref, b_ref, lam):
        to_time_major(p_ref, HALO_T, 0, tile_idx > 0)
        to_time_major(x_ref, tt, halo_rows, None)
        to_time_major(n_ref, HALO_T, halo_rows + rows, tile_idx < nt - 1)
        cw = cw_ref[...]

        def tap(k):
            start = halo_rows + (k - 2) * nbat
            return jnp.concatenate([xw_ref[s, start:start + rows, :] for s in range(n_slab)], axis=1)

        xc = cw[0:1] * tap(0) + cw[1:2] * tap(1) + cw[2:3] * tap(2) + cw[3:4] * tap(3) + cb_ref[...]
        g = jnp.dot(xc.astype(BF16), w_ref[...], preferred_element_type=F32) + b_ref[...]
        r = _sigmoid(g[:, :width])
        gi = _sigmoid(g[:, width:])
        softplus = jnp.maximum(-lam, 0.0) + jnp.log(1.0 + jnp.exp(-jnp.abs(lam)))
        log_a = (-RG_C * softplus) * r
        a = jnp.exp(log_a)
        a_ref[...] = a
        u_ref[...] = jnp.sqrt(1.0 - a * a) * (gi * xc)

    def step(t, h):
        r0 = pl.multiple_of(t * nbat, nbat)
        h = a_ref[pl.ds(r0, nbat), :] * h + u_ref[pl.ds(r0, nbat), :]
        for s, slab in enumerate(_lane_slabs(h)):
            hs_ref[s, pl.ds(r0, nbat), :] = slab
        return h

    def to_batch_major(dst_ref):
        for b in range(nbat):
            dst_ref[b] = jnp.concatenate(
                [hs_ref.at[s][pl.ds(b, tt, stride=nbat), :] for s in range(n_slab)], axis=1).astype(BF16)

    gates(xf_ref, pf_ref, nf_ref, i, wf_ref, bf_ref, lam_ref[0:1])
    cf_ref[...] = lax.fori_loop(0, tt, step, cf_ref[...], unroll=8)
    to_batch_major(hf_ref)

    gates(xb_ref, pb_ref, nb_ref, nt - 1 - i, wb_ref, bb_ref, lam_ref[1:2])
    cbk_ref[...] = lax.fori_loop(0, tt, lambda s, h: step(tt - 1 - s, h), cbk_ref[...], unroll=8)
    to_batch_major(hb_ref)


def _rglru(rg, conv_w, conv_b, w_f, w_b, b_f, b_b, lam):
    nbat, seq, _ = rg.shape
    nt = seq // RG_TT
    hpt = RG_TT // HALO_T
    nh = seq // HALO_T
    cur = lambda f: pl.BlockSpec((nbat, RG_TT, RG_WIDTH), lambda i: (0, f(i), 0))
    prv = lambda f: pl.BlockSpec(
        (nbat, HALO_T, RG_WIDTH), lambda i: (0, jnp.maximum(f(i) * hpt - 1, 0), 0))
    nxt = lambda f: pl.BlockSpec(
        (nbat, HALO_T, RG_WIDTH), lambda i: (0, jnp.minimum((f(i) + 1) * hpt, nh - 1), 0))
    fw = lambda i: i
    bw = lambda i: nt - 1 - i
    n_slab = RG_WIDTH // LANES
    return pl.pallas_call(
        functools.partial(_rglru_body, nt),
        out_shape=(
            jax.ShapeDtypeStruct((nbat, seq, RG_WIDTH), BF16),
            jax.ShapeDtypeStruct((nbat, seq, RG_WIDTH), BF16),
        ),
        grid=(nt,),
        in_specs=[
            _const_spec((RG_CONV, RG_WIDTH)), _const_spec((1, RG_WIDTH)),
            _const_spec((RG_WIDTH, 2 * RG_WIDTH)), _const_spec((RG_WIDTH, 2 * RG_WIDTH)),
            _const_spec((1, 2 * RG_WIDTH)), _const_spec((1, 2 * RG_WIDTH)),
            _const_spec((2, RG_WIDTH)),
            cur(fw), prv(fw), nxt(fw), cur(bw), prv(bw), nxt(bw),
        ],
        out_specs=(cur(fw), cur(bw)),
        scratch_shapes=[
            pltpu.VMEM((n_slab, (RG_TT + 2 * HALO_T) * nbat, LANES), F32),
            pltpu.VMEM((RG_TT * nbat, RG_WIDTH), F32),
            pltpu.VMEM((RG_TT * nbat, RG_WIDTH), F32),
            pltpu.VMEM((n_slab, RG_TT * nbat, LANES), F32),
            pltpu.VMEM((nbat, RG_WIDTH), F32),
            pltpu.VMEM((nbat, RG_WIDTH), F32),
        ],
        compiler_params=pltpu.CompilerParams(
            dimension_semantics=("arbitrary",), vmem_limit_bytes=VMEM_LIMIT_BYTES),
        name="rglru",
    )(conv_w, conv_b, w_f, w_b, b_f, b_b, lam, rg, rg, rg, rg, rg, rg)


def _block_diag(w):
    eye = jnp.eye(RG_BLOCKS, dtype=w.dtype)
    return jnp.einsum("hij,hk->hikj", w, eye).reshape(RG_WIDTH, RG_WIDTH)


def _merge_body(h_ref, g_ref, wg_ref, bg_ref, ya_ref, o0_ref, o1_ref, o2_ref,
                l0_ref, l1_ref, l2_ref, hf_ref, hb_ref, gate_ref, exp_ref,
                wphy_ref, wpat_ref, wprg_ref, wout_ref, out_ref, os_ref, ls_ref):
    x = h_ref[0]
    xb = _rms(x, g_ref[...]).astype(BF16)
    gates = _sigmoid(jnp.dot(xb, wg_ref[...], preferred_element_type=F32) + bg_ref[...])

    def token_order(o_ref, l_ref):
        d, n = o_ref.shape[1], o_ref.shape[2]
        if d == 1:
            return o_ref[0, 0].astype(F32), l_ref[0, 0]
        for r in range(d):
            for s, slab in enumerate(_lane_slabs(o_ref[0, r].astype(F32))):
                os_ref.at[s][pl.ds(r, n, stride=d), :] = slab
            ls_ref[pl.ds(r, n, stride=d), :] = l_ref[0, r]
        o = jnp.concatenate([os_ref[s] for s in range(ATTN_WIDTH // LANES)], axis=1)
        return o, ls_ref[...]

    o0, l0 = token_order(o0_ref, l0_ref)
    o1, l1 = token_order(o1_ref, l1_ref)
    o2, l2 = token_order(o2_ref, l2_ref)

    head_lane = lax.broadcasted_iota(jnp.int32, l0.shape, 1) < HEADS
    d0, d1, d2 = (pltpu.roll(l, LANES - HEADS, 1) for l in (l0, l1, l2))
    m = jnp.maximum(jnp.maximum(l0, l1), l2)
    e0, e1, e2 = jnp.exp2(l0 - m), jnp.exp2(l1 - m), jnp.exp2(l2 - m)
    inv = 1.0 / jnp.where(head_lane, e0 * d0 + e1 * d1 + e2 * d2, 1.0)

    def widen(w):
        hi = w.astype(BF16)
        lo = (w - hi.astype(F32)).astype(BF16)
        return (jnp.dot(hi, exp_ref[...], preferred_element_type=F32)
                + jnp.dot(lo, exp_ref[...], preferred_element_type=F32))

    y_b = widen(e0 * inv) * o0 + widen(e1 * inv) * o1 + widen(e2 * inv) * o2

    gt = gate_ref[0].astype(F32)
    gelu = 0.5 * gt * (1.0 + jnp.tanh(math.sqrt(2.0 / math.pi) * (gt + 0.044715 * (gt * gt * gt))))
    y_c = (hf_ref[0].astype(F32) + hb_ref[0].astype(F32)) * gelu

    merged = (gates[:, :D_MODEL]
              * jnp.dot(ya_ref[0], wphy_ref[...], preferred_element_type=F32)
              + gates[:, D_MODEL:2 * D_MODEL]
              * jnp.dot(y_b.astype(BF16), wpat_ref[...], preferred_element_type=F32)
              + gates[:, 2 * D_MODEL:]
              * jnp.dot(y_c.astype(BF16), wprg_ref[...], preferred_element_type=F32))
    out_ref[0] = x + jnp.dot(merged.astype(BF16), wout_ref[...], preferred_element_type=F32)


def _merge(h3, g, w_gate, b_gate, y_a, attn, hf, hb, rg, expand, wp_hy, wp_attn, wp_rg, w_out):
    nb, seq, _ = h3.shape
    tok = lambda width, col=0: pl.BlockSpec((1, TOKEN_TILE, width), lambda b, i: (b, i, col))
    res = lambda d, width: pl.BlockSpec((1, d, TOKEN_TILE // d, width), lambda b, i: (b, 0, i, 0))
    (o0, l0), (o1, l1), (o2, l2) = attn
    dils = [d for _, d in ATTN_GROUPS]
    return pl.pallas_call(
        _merge_body,
        out_shape=jax.ShapeDtypeStruct(h3.shape, F32),
        grid=(nb, seq // TOKEN_TILE),
        in_specs=[
            tok(D_MODEL), _const_spec((1, D_MODEL)),
            _const_spec((D_MODEL, N_BRANCH * D_MODEL)), _const_spec((1, N_BRANCH * D_MODEL)),
            tok(HY_WIDTH),
            res(dils[0], ATTN_WIDTH), res(dils[1], ATTN_WIDTH), res(dils[2], ATTN_WIDTH),
            res(dils[0], LANES), res(dils[1], LANES), res(dils[2], LANES),
            tok(RG_WIDTH), tok(RG_WIDTH), tok(RG_WIDTH, 1),
            _const_spec((LANES, ATTN_WIDTH)),
            _const_spec((HY_WIDTH, D_MODEL)), _const_spec((ATTN_WIDTH, D_MODEL)),
            _const_spec((RG_WIDTH, D_MODEL)), _const_spec((D_MODEL, D_MODEL)),
        ],
        out_specs=tok(D_MODEL),
        scratch_shapes=[
            pltpu.VMEM((ATTN_WIDTH // LANES, TOKEN_TILE, LANES), F32),
            pltpu.VMEM((TOKEN_TILE, LANES), F32),
        ],
        compiler_params=pltpu.CompilerParams(
            dimension_semantics=("arbitrary", "arbitrary"), vmem_limit_bytes=VMEM_LIMIT_BYTES),
        name="merge",
    )(h3, g, w_gate, b_gate, y_a, o0, o1, o2, l0, l1, l2, hf, hb, rg, expand,
      wp_hy, wp_attn, wp_rg, w_out)


def _mixer(h3, l, p, consts):
    fre, fim, zpos, deltas, expand = consts
    w_in = p["w_in"][l].astype(BF16)
    w_qkv = p["w_in"][l][:, HY_COLS:HY_COLS + QKV_COLS].reshape(D_MODEL, 3, N_GROUPS, ATTN_WIDTH)
    qkv_scale = jnp.array([HEAD_DIM ** -0.5 * LOG2E, 1.0, 1.0], F32).reshape(1, 3, 1, 1)
    w_qkv = (w_qkv * qkv_scale).astype(BF16)
    w_q = [w_qkv[:, :, g, :].reshape(D_MODEL, 3 * ATTN_WIDTH) for g in range(N_GROUPS)]
    hy, q0, q1, q2, rg = _inproj(
        h3, p["mix_norm"][l].reshape(1, -1), w_in[:, :HY_COLS], w_q, w_in[:, HY_COLS + QKV_COLS:])

    g_spec, g_nyq = _hy_filter(
        zpos, p["hy_w1"][l], p["hy_b1"][l], p["hy_w2"][l], p["hy_b2"][l], p["hy_w3"][l],
        p["hy_b3"][l], p["hy_freq"][l], p["hy_wout"][l], deltas, fre, fim)
    y_a = _hy_conv(hy, p["hy_conv_w"][l], p["hy_conv_b"][l].reshape(1, -1),
                   fre, fim, g_spec, g_nyq, p["hy_skip"][l])

    attn = [_attn_group(q, p["rel_bias"], g, dil)
            for g, (q, (_, dil)) in enumerate(zip((q0, q1, q2), ATTN_GROUPS))]

    rg_w = lambda d: jnp.concatenate(
        [_block_diag(p["rg_wa"][l, d]), _block_diag(p["rg_wx"][l, d])], axis=1).astype(BF16)
    rg_b = lambda d: jnp.concatenate([p["rg_ba"][l, d], p["rg_bx"][l, d]]).reshape(1, -1)
    hf, hb = _rglru(rg, p["rg_conv_w"][l], p["rg_conv_b"][l].reshape(1, -1),
                    rg_w(0), rg_w(1), rg_b(0), rg_b(1), p["rg_lambda"][l])

    return _merge(
        h3, p["mix_norm"][l].reshape(1, -1), p["w_gate"][l].astype(BF16),
        p["b_gate"][l].reshape(1, -1), y_a, attn, hf, hb, rg, expand,
        p["w_proj_hy"][l].astype(BF16), p["w_proj_attn"][l].astype(BF16),
        p["w_proj_rg"][l].astype(BF16), p["w_out"][l].astype(BF16))


def _forward(x, p):
    nb, seq, _ = x.shape
    fre, fim = _dft_mats(HY_P)
    deltas = jnp.abs(jnp.linspace(math.log(HY_DECAY_TARGET) / HY_FAST_DECAY,
                                  math.log(HY_DECAY_TARGET) / HY_SLOW_DECAY, HY_WIDTH, dtype=F32))
    head_of_lane = jnp.arange(ATTN_WIDTH, dtype=jnp.int32) // HEAD_DIM
    expand = (jnp.arange(LANES, dtype=jnp.int32)[:, None] == head_of_lane[None, :]).astype(BF16)
    consts = (fre, fim, _hy_positions(seq), deltas, expand)

    def ffn(h3, norm, wg, wu, wd, l, g_out=None):
        wgu = jnp.concatenate([wg[l], wu[l]], axis=1).astype(BF16)
        out = _ffn(h3.reshape(nb * seq, D_MODEL), norm[l].reshape(1, -1), wgu, wd[l].astype(BF16), g_out)
        return out.reshape(nb, seq, D_MODEL)

    h = x
    for l in range(DEPTH):
        h = ffn(h, p["ffn1_norm"], p["ffn1_wg"], p["ffn1_wu"], p["ffn1_wd"], l)
        h = _mixer(h, l, p, consts)
        g_out = p["final_norm"].reshape(1, -1) if l == DEPTH - 1 else None
        h = ffn(h, p["ffn2_norm"], p["ffn2_wg"], p["ffn2_wu"], p["ffn2_wd"], l, g_out)
    return h


def kernel(x, ffn1_norm, ffn1_wg, ffn1_wu, ffn1_wd, mix_norm, w_in, hy_conv_w, hy_conv_b, hy_w1, hy_b1, hy_w2, hy_b2, hy_w3, hy_b3, hy_freq, hy_wout, hy_skip, rel_bias, rg_conv_w, rg_conv_b, rg_wa, rg_ba, rg_wx, rg_bx, rg_lambda, w_gate, b_gate, w_proj_hy, w_proj_attn, w_proj_rg, w_out, ffn2_norm, ffn2_wg, ffn2_wu, ffn2_wd, final_norm):
    p = dict(
        ffn1_norm=ffn1_norm, ffn1_wg=ffn1_wg, ffn1_wu=ffn1_wu, ffn1_wd=ffn1_wd, mix_norm=mix_norm,
        w_in=w_in, hy_conv_w=hy_conv_w, hy_conv_b=hy_conv_b, hy_w1=hy_w1, hy_b1=hy_b1, hy_w2=hy_w2,
        hy_b2=hy_b2, hy_w3=hy_w3, hy_b3=hy_b3, hy_freq=hy_freq, hy_wout=hy_wout, hy_skip=hy_skip,
        rel_bias=rel_bias, rg_conv_w=rg_conv_w, rg_conv_b=rg_conv_b, rg_wa=rg_wa, rg_ba=rg_ba,
        rg_wx=rg_wx, rg_bx=rg_bx, rg_lambda=rg_lambda, w_gate=w_gate, b_gate=b_gate,
        w_proj_hy=w_proj_hy, w_proj_attn=w_proj_attn, w_proj_rg=w_proj_rg, w_out=w_out,
        ffn2_norm=ffn2_norm, ffn2_wg=ffn2_wg, ffn2_wu=ffn2_wu, ffn2_wd=ffn2_wd,
        final_norm=final_norm)
    return _forward(x, p)
```

```python
import functools
import math

import numpy as np
import jax
import jax.numpy as jnp
from jax import lax
from jax.experimental import pallas as pl
from jax.experimental.pallas import tpu as pltpu

F32 = jnp.float32
BF16 = jnp.bfloat16

D_MODEL = 1024
D_FF = 2816
DEPTH = 4
RMS_EPS = 1e-6

HY_WIDTH = 512
HY_ORDER = 2
HY_EMB = 33
HY_BANDS = 16
HY_HIDDEN = 64
HY_COLS = 3 * HY_WIDTH
HY_DECAY_TARGET = 1e-2
HY_FAST_DECAY = 0.3
HY_SLOW_DECAY = 1.5
HY_MOD_SHIFT = 0.05
HY_NORM_EPS = 1e-6

ATTN_GROUPS = ((128, 1), (512, 4), (2048, 16))
N_GROUPS = 3
HEADS = 8
HEAD_DIM = 64
ATTN_WIDTH = HEADS * HEAD_DIM
QKV_COLS = 3 * N_GROUPS * ATTN_WIDTH
BAND = 64
N_BUCKETS = 32
BUCKET_MAX_EXACT = 8
BUCKET_MAX_DIST = 1024
NEG_INF = -1e30
LOG2E = math.log2(math.e)
LN2 = math.log(2.0)
Q_SCALE = HEAD_DIM ** -0.5 * LOG2E

RG_WIDTH = 512
RG_BLOCKS = 8
RG_BLOCK = 64
RG_CONV = 4
RG_C = 8.0
RG_COLS = 2 * RG_WIDTH

N_BRANCH = 3

LANES = 128
VMEM_LIMIT_BYTES = 56 * 1024 * 1024

TOKEN_TILE = 512
Q_TILE = 128
K_WIN = Q_TILE + 2 * BAND
ATTN_UNROLL = 4
HY_CBLK = 256
HY_P = 512
RG_TT = 128
HALO_T = 8


def _const_spec(shape):
    nd = len(shape)
    return pl.BlockSpec(shape, lambda *_: (0,) * nd, pipeline_mode=pl.Buffered(1))


def _layer_spec(l, rows, cols, col_block=0):
    return pl.BlockSpec((1, rows, cols), lambda *_: (l, 0, col_block), pipeline_mode=pl.Buffered(1))


def _rms(x, g):
    ms = jnp.mean(x * x, axis=-1, keepdims=True)
    return x * lax.rsqrt(ms + RMS_EPS) * g


def _sigmoid(x):
    return 1.0 / (1.0 + jnp.exp(-x))


def _lane_slabs(x):
    return [x[:, s * LANES:(s + 1) * LANES] for s in range(x.shape[1] // LANES)]


def _ffn_body(h_ref, g_ref, wg_ref, wu_ref, wd_ref, *rest):
    gout_ref, o_ref = rest if len(rest) == 2 else (None, rest[0])
    x = h_ref[...]
    xb = _rms(x, g_ref[...]).astype(BF16)
    gate = jnp.dot(xb, wg_ref[0], preferred_element_type=F32)
    up = jnp.dot(xb, wu_ref[0], preferred_element_type=F32)
    act = (gate * _sigmoid(gate)) * up
    out = x + 0.5 * jnp.dot(act.astype(BF16), wd_ref[0], preferred_element_type=F32)
    o_ref[...] = out if gout_ref is None else _rms(out, gout_ref[...])


def _ffn(h, g, wg, wu, wd, l, g_out=None):
    n_tok = h.shape[0]
    extra = [] if g_out is None else [g_out]
    return pl.pallas_call(
        _ffn_body,
        out_shape=jax.ShapeDtypeStruct(h.shape, F32),
        grid=(n_tok // TOKEN_TILE,),
        in_specs=[
            pl.BlockSpec((TOKEN_TILE, D_MODEL), lambda i: (i, 0)),
            _const_spec((1, D_MODEL)),
            _layer_spec(l, D_MODEL, D_FF),
            _layer_spec(l, D_MODEL, D_FF),
            _layer_spec(l, D_FF, D_MODEL),
        ] + [_const_spec((1, D_MODEL))] * len(extra),
        out_specs=pl.BlockSpec((TOKEN_TILE, D_MODEL), lambda i: (i, 0)),
        compiler_params=pltpu.CompilerParams(
            dimension_semantics=("arbitrary",), vmem_limit_bytes=VMEM_LIMIT_BYTES),
        name="ffn",
    )(h, g, wg, wu, wd, *extra)


def _inproj_body(h_ref, g_ref, why_ref, *rest):
    w_qkv = rest[:3 * N_GROUPS]
    wrg_ref, hy_ref, q0_ref, q1_ref, q2_ref, rg_ref, xs_ref = rest[3 * N_GROUPS:]
    xn = _rms(h_ref[0], g_ref[...])
    xb = xn.astype(BF16)
    hy_ref[0] = jnp.dot(xb, why_ref[0], preferred_element_type=F32).astype(BF16)
    rg_ref[0] = jnp.dot(xb, wrg_ref[0], preferred_element_type=F32).astype(BF16)

    for s, slab in enumerate(_lane_slabs(xn)):
        xs_ref[s] = slab

    def by_residue(d):
        if d == 1:
            return xb
        n = TOKEN_TILE // d
        cols = [jnp.concatenate([xs_ref.at[s][pl.ds(r, n, stride=d), :] for r in range(d)], axis=0)
                for s in range(D_MODEL // LANES)]
        return jnp.concatenate(cols, axis=1).astype(BF16)

    for g, (q_ref, (_, d)) in enumerate(zip((q0_ref, q1_ref, q2_ref), ATTN_GROUPS)):
        lhs = by_residue(d)
        for part in range(3):
            out = jnp.dot(lhs, w_qkv[3 * g + part][0], preferred_element_type=F32)
            if part == 0:
                out = out * Q_SCALE
            q_ref[0, :, :, part * ATTN_WIDTH:(part + 1) * ATTN_WIDTH] = (
                out.astype(BF16).reshape(d, TOKEN_TILE // d, ATTN_WIDTH))


def _inproj(h3, g, w_in, l):
    nb, seq, _ = h3.shape
    dils = [d for _, d in ATTN_GROUPS]
    qshape = lambda d: jax.ShapeDtypeStruct((nb, d, seq // d, 3 * ATTN_WIDTH), BF16)
    qspec = lambda d: pl.BlockSpec((1, d, TOKEN_TILE // d, 3 * ATTN_WIDTH), lambda b, i: (b, 0, i, 0))
    qkv_block0 = HY_COLS // ATTN_WIDTH
    w_qkv_specs = [_layer_spec(l, D_MODEL, ATTN_WIDTH, qkv_block0 + part * N_GROUPS + grp)
                   for grp in range(N_GROUPS) for part in range(3)]
    return pl.pallas_call(
        _inproj_body,
        out_shape=(
            jax.ShapeDtypeStruct((nb, seq, HY_COLS), BF16),
            qshape(dils[0]), qshape(dils[1]), qshape(dils[2]),
            jax.ShapeDtypeStruct((nb, seq, RG_COLS), BF16),
        ),
        grid=(nb, seq // TOKEN_TILE),
        in_specs=[
            pl.BlockSpec((1, TOKEN_TILE, D_MODEL), lambda b, i: (b, i, 0)),
            _const_spec((1, D_MODEL)),
            _layer_spec(l, D_MODEL, HY_COLS, 0),
            *w_qkv_specs,
            _layer_spec(l, D_MODEL, RG_COLS, (HY_COLS + QKV_COLS) // RG_COLS),
        ],
        out_specs=(
            pl.BlockSpec((1, TOKEN_TILE, HY_COLS), lambda b, i: (b, i, 0)),
            qspec(dils[0]), qspec(dils[1]), qspec(dils[2]),
            pl.BlockSpec((1, TOKEN_TILE, RG_COLS), lambda b, i: (b, i, 0)),
        ),
        scratch_shapes=[pltpu.VMEM((D_MODEL // LANES, TOKEN_TILE, LANES), F32)],
        compiler_params=pltpu.CompilerParams(
            dimension_semantics=("arbitrary", "arbitrary"), vmem_limit_bytes=VMEM_LIMIT_BYTES),
        name="inproj",
    )(h3, g, w_in, *([w_in] * (3 * N_GROUPS)), w_in)


def _dft_mats(half):
    n = jnp.arange(half, dtype=jnp.int32)
    kn = (n[:, None] * n[None, :]) % (2 * half)
    ang = kn.astype(F32) * (math.pi / half)
    return jnp.cos(ang).astype(BF16), (-jnp.sin(ang)).astype(BF16)


def _hy_positions(seq):
    t = jnp.linspace(0.0, 1.0, seq, dtype=F32)[:, None]
    tr = jnp.arange(seq, dtype=F32)[:, None]
    wpos = 2.0 * math.pi * tr / seq
    fb = jnp.linspace(1e-4, HY_BANDS - 1, HY_BANDS, dtype=F32)[None, :]
    z = jnp.concatenate([t, jnp.cos(fb * wpos), -jnp.sin(fb * wpos)], axis=-1)
    return jnp.pad(z, ((0, 0), (0, LANES - HY_EMB)))


def _hy_filter_body(z_ref, w1_ref, b1_ref, w2_ref, b2_ref, w3_ref, b3_ref, fr_ref,
                    wf_ref, wb_ref, dl_ref, fre_ref, fim_ref,
                    g_ref, gn_ref):
    hi = lax.Precision.HIGHEST
    seq = z_ref.shape[0]
    fr = fr_ref[...]
    hdn = jnp.sin(fr * (jnp.dot(z_ref[...], w1_ref[...], precision=hi,
                                preferred_element_type=F32) + b1_ref[...]))
    hdn = jnp.sin(fr * (jnp.dot(hdn, w2_ref[...], precision=hi,
                                preferred_element_type=F32) + b2_ref[...]))
    hdn = jnp.sin(fr * (jnp.dot(hdn, w3_ref[...], precision=hi,
                                preferred_element_type=F32) + b3_ref[...]))
    row = lax.broadcasted_iota(jnp.int32, (seq, HY_WIDTH), 0)
    t = row.astype(F32) * (1.0 / (seq - 1))
    decay = jnp.exp(-t * dl_ref[...]) + HY_MOD_SHIFT
    kf = jnp.dot(hdn, wf_ref[0], precision=hi, preferred_element_type=F32) * decay
    kb = jnp.dot(hdn, wb_ref[0], precision=hi, preferred_element_type=F32) * decay
    kb = jnp.where(row == 0, 0.0, kb)
    norm = (jnp.sum(jnp.abs(kf), axis=0, keepdims=True)
            + jnp.sum(jnp.abs(kb), axis=0, keepdims=True) + HY_NORM_EPS)
    kf = kf / norm
    kb = kb / norm

    nseg = seq // HY_P
    krow = lax.broadcasted_iota(jnp.int32, (HY_P, HY_WIDTH), 0)
    sk = (1 - 2 * (krow & 1)).astype(F32)
    ck = jnp.where(krow == 0, 0.5 / HY_P, sk * (1.0 / HY_P))
    zero_row = jnp.zeros((1, HY_WIDTH), F32)

    def dft(x):
        xb = x.astype(BF16)
        return (jnp.dot(fre_ref[...], xb, preferred_element_type=F32),
                jnp.dot(fim_ref[...], xb, preferred_element_type=F32),
                jnp.sum(x * sk, axis=0, keepdims=True))

    spectra = {}
    for e in range(nseg):
        spectra[e] = dft(kf[e * HY_P:(e + 1) * HY_P])
        lead = kb[e * HY_P:(e + 1) * HY_P]
        bre, bim, bny = dft(jnp.where(krow == 0, 0.0, lead))
        first = kb[(e + 1) * HY_P:(e + 1) * HY_P + 1] if e + 1 < nseg else zero_row
        spectra[-(e + 1)] = (first + sk * bre, -(sk * bim), first + bny)

    for delta in range(-(nseg - 1), nseg):
        are, aim, any_ = spectra[delta - 1]
        bre, bim, bny = spectra[delta]
        g_ref[0, delta + nseg - 1, 0] = (are + sk * bre) * ck
        g_ref[0, delta + nseg - 1, 1] = (aim + sk * bim) * ck
        gn_ref[0, delta + nseg - 1:delta + nseg, :] = (any_ + bny) * (0.5 / HY_P)
    gn_ref[0, 2 * nseg - 1:2 * nseg, :] = zero_row


def _hy_filter(z, w1, b1, w2, b2, w3, b3, freq, wout, deltas, fre, fim):
    seq = z.shape[0]
    wf = wout[:, :HY_ORDER * HY_WIDTH].reshape(HY_HIDDEN, HY_ORDER, HY_WIDTH).transpose(1, 0, 2)
    wb = wout[:, HY_ORDER * HY_WIDTH:].reshape(HY_HIDDEN, HY_ORDER, HY_WIDTH).transpose(1, 0, 2)
    w1p = jnp.pad(w1, ((0, LANES - HY_EMB), (0, 0)))
    row = lambda a: a.reshape(1, -1)
    nseg = seq // HY_P
    return pl.pallas_call(
        _hy_filter_body,
        out_shape=(
            jax.ShapeDtypeStruct((HY_ORDER, 2 * nseg - 1, 2, HY_P, HY_WIDTH), F32),
            jax.ShapeDtypeStruct((HY_ORDER, 2 * nseg, HY_WIDTH), F32),
        ),
        grid=(HY_ORDER,),
        in_specs=[
            _const_spec((seq, LANES)),
            _const_spec((LANES, HY_HIDDEN)), _const_spec((1, HY_HIDDEN)),
            _const_spec((HY_HIDDEN, HY_HIDDEN)), _const_spec((1, HY_HIDDEN)),
            _const_spec((HY_HIDDEN, HY_HIDDEN)), _const_spec((1, HY_HIDDEN)),
            _const_spec((1, HY_HIDDEN)),
            pl.BlockSpec((1, HY_HIDDEN, HY_WIDTH), lambda o: (o, 0, 0)),
            pl.BlockSpec((1, HY_HIDDEN, HY_WIDTH), lambda o: (o, 0, 0)),
            _const_spec((1, HY_WIDTH)),
            _const_spec((HY_P, HY_P)), _const_spec((HY_P, HY_P)),
        ],
        out_specs=(
            pl.BlockSpec((1, 2 * nseg - 1, 2, HY_P, HY_WIDTH), lambda o: (o, 0, 0, 0, 0)),
            pl.BlockSpec((1, 2 * nseg, HY_WIDTH), lambda o: (o, 0, 0)),
        ),
        compiler_params=pltpu.CompilerParams(
            dimension_semantics=("arbitrary",), vmem_limit_bytes=VMEM_LIMIT_BYTES),
        name="hy_filter",
    )(z, w1p, row(b1), w2, row(b2), w3, row(b3), row(freq), wf, wb, row(deltas), fre, fim)


def _hy_conv_body(v_ref, x1_ref, x2_ref, wv_ref, w1_ref, w2_ref, bv_ref, b1_ref, b2_ref,
                  fre_ref, fim_ref, g_ref, gn_ref, skip_ref, o_ref,
                  z_ref, g1_ref, g2_ref, ure_ref, uim_ref):
    seq = v_ref.shape[1]
    nseg = seq // HY_P
    row = lax.broadcasted_iota(jnp.int32, (seq, HY_CBLK), 0)
    sgn = (1 - 2 * (lax.broadcasted_iota(jnp.int32, (HY_P, HY_CBLK), 0) & 1)).astype(F32)

    def short_conv(x_ref, w_ref, b_ref):
        x = x_ref[0].astype(F32)
        w = w_ref[...]
        prev = jnp.where(row == 0, 0.0, pltpu.roll(x, 1, 0))
        nxt = jnp.where(row == seq - 1, 0.0, pltpu.roll(x, seq - 1, 0))
        return w[0:1] * prev + w[1:2] * x + w[2:3] * nxt + b_ref[...]

    z_ref[...] = short_conv(v_ref, wv_ref, bv_ref)
    g1_ref[...] = short_conv(x1_ref, w1_ref, b1_ref)
    g2_ref[...] = short_conv(x2_ref, w2_ref, b2_ref)
    skip = skip_ref[...]
    blk = lambda i: slice(i * HY_P, (i + 1) * HY_P)

    for o, gate_ref in enumerate((g1_ref, g2_ref)):
        nyq = []
        for j in range(nseg):
            x = z_ref[blk(j), :]
            xb = x.astype(BF16)
            ure_ref[j] = jnp.dot(fre_ref[...], xb, preferred_element_type=F32)
            uim_ref[j] = jnp.dot(fim_ref[...], xb, preferred_element_type=F32)
            nyq.append(jnp.sum(x * sgn, axis=0, keepdims=True))
        for i in range(nseg):
            zre = zim = zny = None
            for j in range(nseg):
                d = i - j + nseg - 1
                gre, gim = g_ref[o, d, 0], g_ref[o, d, 1]
                ure, uim = ure_ref[j], uim_ref[j]
                tre = gre * ure - gim * uim
                tim = gre * uim + gim * ure
                tny = gn_ref[o, d:d + 1, :] * nyq[j]
                zre, zim, zny = (tre, tim, tny) if zre is None else (zre + tre, zim + tim, zny + tny)
            y = (jnp.dot(fre_ref[...], zre.astype(BF16), preferred_element_type=F32)
                 + jnp.dot(fim_ref[...], zim.astype(BF16), preferred_element_type=F32)
                 + sgn * zny)
            z = gate_ref[blk(i), :] * (y + skip[o:o + 1] * z_ref[blk(i), :])
            if o + 1 < HY_ORDER:
                z_ref[blk(i), :] = z
            else:
                o_ref[0, blk(i), :] = z.astype(BF16)


def _hy_conv(u_hy, conv_w, conv_b, fre, fim, g, gn, skip):
    nb, seq, _ = u_hy.shape
    ncb = HY_WIDTH // HY_CBLK
    nseg = seq // HY_P
    data = lambda part: pl.BlockSpec((1, seq, HY_CBLK), lambda c, b: (b, 0, part * ncb + c))
    wspec = lambda part: pl.BlockSpec((3, HY_CBLK), lambda c, b: (0, part * ncb + c))
    bspec = lambda part: pl.BlockSpec((1, HY_CBLK), lambda c, b: (0, part * ncb + c))
    return pl.pallas_call(
        _hy_conv_body,
        out_shape=jax.ShapeDtypeStruct((nb, seq, HY_WIDTH), BF16),
        grid=(ncb, nb),
        in_specs=[
            data(0), data(1), data(2),
            wspec(0), wspec(1), wspec(2),
            bspec(0), bspec(1), bspec(2),
            _const_spec((HY_P, HY_P)), _const_spec((HY_P, HY_P)),
            pl.BlockSpec((HY_ORDER, 2 * nseg - 1, 2, HY_P, HY_CBLK), lambda c, b: (0, 0, 0, 0, c),
                         pipeline_mode=pl.Buffered(1)),
            pl.BlockSpec((HY_ORDER, 2 * nseg, HY_CBLK), lambda c, b: (0, 0, c)),
            pl.BlockSpec((HY_ORDER, HY_CBLK), lambda c, b: (0, c)),
        ],
        out_specs=pl.BlockSpec((1, seq, HY_CBLK), lambda c, b: (b, 0, c)),
        scratch_shapes=[
            pltpu.VMEM((seq, HY_CBLK), F32), pltpu.VMEM((seq, HY_CBLK), F32),
            pltpu.VMEM((seq, HY_CBLK), F32),
            pltpu.VMEM((nseg, HY_P, HY_CBLK), F32), pltpu.VMEM((nseg, HY_P, HY_CBLK), F32),
        ],
        compiler_params=pltpu.CompilerParams(
            dimension_semantics=("arbitrary", "arbitrary"), vmem_limit_bytes=VMEM_LIMIT_BYTES),
        name="hy_conv",
    )(u_hy, u_hy, u_hy, conv_w, conv_w, conv_w, conv_b, conv_b, conv_b,
      fre, fim, g, gn, skip)


def _bucket_matrix(dilation):
    qi = np.arange(Q_TILE, dtype=np.int64)[:, None]
    kj = np.arange(K_WIN, dtype=np.int64)[None, :]
    delta = kj - BAND - qi
    rel = delta * dilation
    half = N_BUCKETS // 2
    n = np.abs(rel)
    nf = np.maximum(n, 1).astype(np.float32)
    large = BUCKET_MAX_EXACT + (
        np.log(nf / np.float32(BUCKET_MAX_EXACT)) / np.float32(math.log(BUCKET_MAX_DIST / BUCKET_MAX_EXACT))
        * np.float32(half - BUCKET_MAX_EXACT)).astype(np.int32)
    large = np.minimum(large, half - 1)
    bucket = np.where(rel > 0, half, 0) + np.where(n < BUCKET_MAX_EXACT, n, large)
    return np.where(np.abs(delta) <= BAND, bucket, -1).astype(np.int32)


def _attn_body(group, tbl_ref, bkt_ref, q_ref, k_ref, v_ref, o_ref, l_ref,
               bias_ref, kpad_ref, vpad_ref):
    _, dil, ls, _ = q_ref.shape
    tiles_per_res = ls // Q_TILE

    kcol = lax.broadcasted_iota(jnp.int32, (Q_TILE, K_WIN), 1)

    @pl.when(pl.program_id(0) == 0)
    def _():
        bkt = bkt_ref[...]
        for h in range(HEADS):
            acc = jnp.full((Q_TILE, K_WIN), NEG_INF, F32)
            for bk in range(N_BUCKETS):
                acc = jnp.where(bkt == bk, tbl_ref[bk, group * HEADS + h] * LOG2E, acc)
            no_head = jnp.where(kcol < BAND, NEG_INF, acc)
            bias_ref[0, h] = acc
            bias_ref[1, h] = no_head
            bias_ref[2, h] = jnp.where(kcol >= Q_TILE + BAND, NEG_INF, acc)
            bias_ref[3, h] = jnp.where(kcol >= Q_TILE + BAND, NEG_INF, no_head)
        zeros = jnp.zeros((dil, BAND, ATTN_WIDTH), BF16)
        kpad_ref[:, 0:BAND] = zeros
        kpad_ref[:, BAND + ls:BAND + ls + BAND] = zeros
        vpad_ref[:, 0:BAND] = zeros
        vpad_ref[:, BAND + ls:BAND + ls + BAND] = zeros

    kpad_ref[:, BAND:BAND + ls] = k_ref[0]
    vpad_ref[:, BAND:BAND + ls] = v_ref[0]

    lane = lax.broadcasted_iota(jnp.int32, (Q_TILE, LANES), 1)
    low_half = lane < HEAD_DIM

    def tile(t, carry):
        r = t // tiles_per_res
        jt = t % tiles_per_res
        j0 = pl.multiple_of(jt * Q_TILE, Q_TILE)
        edge = jnp.where(jt == 0, 1, 0) + jnp.where(jt == tiles_per_res - 1, 2, 0)
        lse_tile = jnp.zeros((Q_TILE, LANES), F32)
        for hp in range(HEADS // 2):
            cols = slice(hp * LANES, (hp + 1) * LANES)
            q2 = q_ref[0, r, pl.ds(j0, Q_TILE), cols]
            kw = kpad_ref[r, pl.ds(j0, K_WIN), cols]
            vw = vpad_ref[r, pl.ds(j0, K_WIN), cols]
            outs = []
            for half in range(2):
                h = 2 * hp + half
                qh = jnp.where(low_half if half == 0 else ~low_half, q2, 0.0).astype(BF16)
                s = lax.dot_general(qh, kw, (((1,), (1,)), ((), ())), preferred_element_type=F32)
                s = s + bias_ref[edge, h]
                m = jnp.max(s, axis=-1, keepdims=True)
                p = jnp.exp2(s - m)
                den = jnp.sum(p, axis=-1, keepdims=True)
                pv = jnp.dot(p.astype(BF16), vw, preferred_element_type=F32)
                outs.append(pv)
                lse_tile = jnp.where(lane == h, m, jnp.where(lane == HEADS + h, den, lse_tile))
            o_ref[0, r, pl.ds(j0, Q_TILE), cols] = jnp.where(low_half, outs[0], outs[1]).astype(BF16)
        l_ref[0, r, pl.ds(j0, Q_TILE), :] = lse_tile
        return carry

    lax.fori_loop(0, dil * tiles_per_res, tile, 0, unroll=ATTN_UNROLL)


def _attn_group(qkv_g, tbl, group, dilation):
    nb, _, ls, _ = qkv_g.shape
    part = lambda which: pl.BlockSpec((1, dilation, ls, ATTN_WIDTH), lambda b: (b, 0, 0, which))
    bkt = jnp.asarray(_bucket_matrix(dilation))
    return pl.pallas_call(
        functools.partial(_attn_body, group),
        out_shape=(
            jax.ShapeDtypeStruct((nb, dilation, ls, ATTN_WIDTH), BF16),
            jax.ShapeDtypeStruct((nb, dilation, ls, LANES), F32),
        ),
        grid=(nb,),
        in_specs=[
            pl.BlockSpec(memory_space=pltpu.SMEM),
            _const_spec((Q_TILE, K_WIN)),
            part(0), part(1), part(2),
        ],
        out_specs=(
            pl.BlockSpec((1, dilation, ls, ATTN_WIDTH), lambda b: (b, 0, 0, 0)),
            pl.BlockSpec((1, dilation, ls, LANES), lambda b: (b, 0, 0, 0)),
        ),
        scratch_shapes=[
            pltpu.VMEM((4, HEADS, Q_TILE, K_WIN), F32),
            pltpu.VMEM((dilation, ls + 2 * BAND, ATTN_WIDTH), BF16),
            pltpu.VMEM((dilation, ls + 2 * BAND, ATTN_WIDTH), BF16),
        ],
        compiler_params=pltpu.CompilerParams(
            dimension_semantics=("arbitrary",), vmem_limit_bytes=VMEM_LIMIT_BYTES),
        name=f"attn_g{group}",
    )(tbl, bkt, qkv_g, qkv_g, qkv_g)


def _rglru_body(nt, cw_ref, cb_ref, wf_ref, wb_ref, bf_ref, bb_ref, lam_ref,
                xf_ref, pf_ref, nf_ref, xb_ref, pb_ref, nb_ref,
                hf_ref, hb_ref, xw_ref, a_ref, u_ref, hs_ref, cf_ref, cbk_ref):
    i = pl.program_id(0)
    nbat, tt, width = xf_ref.shape
    rows = tt * nbat
    n_slab = width // LANES
    halo_rows = HALO_T * nbat

    @pl.when(i == 0)
    def _():
        cf_ref[...] = jnp.zeros_like(cf_ref)
        cbk_ref[...] = jnp.zeros_like(cbk_ref)

    def to_time_major(src_ref, n_t, row0, keep):
        for b in range(nbat):
            x = src_ref[b].astype(F32)
            x = x if keep is None else jnp.where(keep, x, 0.0)
            for s, slab in enumerate(_lane_slabs(x)):
                xw_ref.at[s][pl.ds(row0 + b, n_t, stride=nbat), :] = slab

    def gates(x_ref, p_ref, n_ref, tile_idx, w_ref, b_ref, lam):
        to_time_major(p_ref, HALO_T, 0, tile_idx > 0)
        to_time_major(x_ref, tt, halo_rows, None)
        to_time_major(n_ref, HALO_T, halo_rows + rows, tile_idx < nt - 1)
        cw = cw_ref[...]

        def tap(k):
            start = halo_rows + (k - 2) * nbat
            return jnp.concatenate([xw_ref[s, start:start + rows, :] for s in range(n_slab)], axis=1)

        xc = cw[0:1] * tap(0) + cw[1:2] * tap(1) + cw[2:3] * tap(2) + cw[3:4] * tap(3) + cb_ref[...]
        g = jnp.dot(xc.astype(BF16), w_ref[...], preferred_element_type=F32) + b_ref[...]
        r = _sigmoid(g[:, :width])
        gi = _sigmoid(g[:, width:])
        softplus = jnp.maximum(-lam, 0.0) + jnp.log(1.0 + jnp.exp(-jnp.abs(lam)))
        log_a = (-RG_C * softplus) * r
        a = jnp.exp(log_a)
        a_ref[...] = a
        one_m_a2 = 1.0 - a * a
        root = jnp.where(one_m_a2 > 0.0, one_m_a2 * lax.rsqrt(one_m_a2), 0.0)
        u_ref[...] = root * (gi * xc)

    def step(t, h):
        r0 = pl.multiple_of(t * nbat, nbat)
        h = a_ref[pl.ds(r0, nbat), :] * h + u_ref[pl.ds(r0, nbat), :]
        for s, slab in enumerate(_lane_slabs(h)):
            hs_ref[s, pl.ds(r0, nbat), :] = slab
        return h

    def to_batch_major(dst_ref):
        for b in range(nbat):
            dst_ref[b] = jnp.concatenate(
                [hs_ref.at[s][pl.ds(b, tt, stride=nbat), :] for s in range(n_slab)], axis=1).astype(BF16)

    gates(xf_ref, pf_ref, nf_ref, i, wf_ref, bf_ref, lam_ref[0:1])
    cf_ref[...] = lax.fori_loop(0, tt, step, cf_ref[...], unroll=8)
    to_batch_major(hf_ref)

    gates(xb_ref, pb_ref, nb_ref, nt - 1 - i, wb_ref, bb_ref, lam_ref[1:2])
    cbk_ref[...] = lax.fori_loop(0, tt, lambda s, h: step(tt - 1 - s, h), cbk_ref[...], unroll=8)
    to_batch_major(hb_ref)


def _rglru(rg, conv_w, conv_b, w_f, w_b, b_f, b_b, lam):
    nbat, seq, _ = rg.shape
    nt = seq // RG_TT
    hpt = RG_TT // HALO_T
    nh = seq // HALO_T
    cur = lambda f: pl.BlockSpec((nbat, RG_TT, RG_WIDTH), lambda i: (0, f(i), 0))
    prv = lambda f: pl.BlockSpec(
        (nbat, HALO_T, RG_WIDTH), lambda i: (0, jnp.maximum(f(i) * hpt - 1, 0), 0))
    nxt = lambda f: pl.BlockSpec(
        (nbat, HALO_T, RG_WIDTH), lambda i: (0, jnp.minimum((f(i) + 1) * hpt, nh - 1), 0))
    fw = lambda i: i
    bw = lambda i: nt - 1 - i
    n_slab = RG_WIDTH // LANES
    return pl.pallas_call(
        functools.partial(_rglru_body, nt),
        out_shape=(
            jax.ShapeDtypeStruct((nbat, seq, RG_WIDTH), BF16),
            jax.ShapeDtypeStruct((nbat, seq, RG_WIDTH), BF16),
        ),
        grid=(nt,),
        in_specs=[
            _const_spec((RG_CONV, RG_WIDTH)), _const_spec((1, RG_WIDTH)),
            _const_spec((RG_WIDTH, 2 * RG_WIDTH)), _const_spec((RG_WIDTH, 2 * RG_WIDTH)),
            _const_spec((1, 2 * RG_WIDTH)), _const_spec((1, 2 * RG_WIDTH)),
            _const_spec((2, RG_WIDTH)),
            cur(fw), prv(fw), nxt(fw), cur(bw), prv(bw), nxt(bw),
        ],
        out_specs=(cur(fw), cur(bw)),
        scratch_shapes=[
            pltpu.VMEM((n_slab, (RG_TT + 2 * HALO_T) * nbat, LANES), F32),
            pltpu.VMEM((RG_TT * nbat, RG_WIDTH), F32),
            pltpu.VMEM((RG_TT * nbat, RG_WIDTH), F32),
            pltpu.VMEM((n_slab, RG_TT * nbat, LANES), F32),
            pltpu.VMEM((nbat, RG_WIDTH), F32),
            pltpu.VMEM((nbat, RG_WIDTH), F32),
        ],
        compiler_params=pltpu.CompilerParams(
            dimension_semantics=("arbitrary",), vmem_limit_bytes=VMEM_LIMIT_BYTES),
        name="rglru",
    )(conv_w, conv_b, w_f, w_b, b_f, b_b, lam, rg, rg, rg, rg, rg, rg)


def _block_diag(w):
    eye = jnp.eye(RG_BLOCKS, dtype=w.dtype)
    return jnp.einsum("hij,hk->hikj", w, eye).reshape(RG_WIDTH, RG_WIDTH)


def _merge_body(h_ref, g_ref, wg_ref, bg_ref, ya_ref, o0_ref, o1_ref, o2_ref,
                l0_ref, l1_ref, l2_ref, hf_ref, hb_ref, gate_ref, exp_ref,
                wphy_ref, wpat_ref, wprg_ref, wout_ref, out_ref, os_ref, ls_ref):
    x = h_ref[0]
    xb = _rms(x, g_ref[...]).astype(BF16)
    gates = _sigmoid(jnp.dot(xb, wg_ref[0], preferred_element_type=F32) + bg_ref[...])

    def token_order(o_ref, l_ref):
        d, n = o_ref.shape[1], o_ref.shape[2]
        if d == 1:
            return o_ref[0, 0].astype(F32), l_ref[0, 0]
        for r in range(d):
            for s, slab in enumerate(_lane_slabs(o_ref[0, r].astype(F32))):
                os_ref.at[s][pl.ds(r, n, stride=d), :] = slab
            ls_ref[pl.ds(r, n, stride=d), :] = l_ref[0, r]
        o = jnp.concatenate([os_ref[s] for s in range(ATTN_WIDTH // LANES)], axis=1)
        return o, ls_ref[...]

    o0, l0 = token_order(o0_ref, l0_ref)
    o1, l1 = token_order(o1_ref, l1_ref)
    o2, l2 = token_order(o2_ref, l2_ref)

    head_lane = lax.broadcasted_iota(jnp.int32, l0.shape, 1) < HEADS
    d0, d1, d2 = (pltpu.roll(l, LANES - HEADS, 1) for l in (l0, l1, l2))
    m = jnp.maximum(jnp.maximum(l0, l1), l2)
    e0, e1, e2 = jnp.exp2(l0 - m), jnp.exp2(l1 - m), jnp.exp2(l2 - m)
    inv = 1.0 / jnp.where(head_lane, e0 * d0 + e1 * d1 + e2 * d2, 1.0)

    def widen(w):
        return jnp.dot(w.astype(BF16), exp_ref[...], preferred_element_type=F32)

    y_b = widen(e0 * inv) * o0 + widen(e1 * inv) * o1 + widen(e2 * inv) * o2

    gt = gate_ref[0].astype(F32)
    gelu = 0.5 * gt * (1.0 + jnp.tanh(math.sqrt(2.0 / math.pi) * (gt + 0.044715 * (gt * gt * gt))))
    y_c = (hf_ref[0].astype(F32) + hb_ref[0].astype(F32)) * gelu

    merged = (gates[:, :D_MODEL]
              * jnp.dot(ya_ref[0], wphy_ref[0], preferred_element_type=F32)
              + gates[:, D_MODEL:2 * D_MODEL]
              * jnp.dot(y_b.astype(BF16), wpat_ref[0], preferred_element_type=F32)
              + gates[:, 2 * D_MODEL:]
              * jnp.dot(y_c.astype(BF16), wprg_ref[0], preferred_element_type=F32))
    out_ref[0] = x + jnp.dot(merged.astype(BF16), wout_ref[0], preferred_element_type=F32)


def _merge(h3, g, w_gate, b_gate, y_a, attn, hf, hb, rg, expand, wp_hy, wp_attn, wp_rg, w_out, l):
    nb, seq, _ = h3.shape
    tok = lambda width, col=0: pl.BlockSpec((1, TOKEN_TILE, width), lambda b, i: (b, i, col))
    res = lambda d, width: pl.BlockSpec((1, d, TOKEN_TILE // d, width), lambda b, i: (b, 0, i, 0))
    (o0, l0), (o1, l1), (o2, l2) = attn
    dils = [d for _, d in ATTN_GROUPS]
    return pl.pallas_call(
        _merge_body,
        out_shape=jax.ShapeDtypeStruct(h3.shape, F32),
        grid=(nb, seq // TOKEN_TILE),
        in_specs=[
            tok(D_MODEL), _const_spec((1, D_MODEL)),
            _layer_spec(l, D_MODEL, N_BRANCH * D_MODEL), _const_spec((1, N_BRANCH * D_MODEL)),
            tok(HY_WIDTH),
            res(dils[0], ATTN_WIDTH), res(dils[1], ATTN_WIDTH), res(dils[2], ATTN_WIDTH),
            res(dils[0], LANES), res(dils[1], LANES), res(dils[2], LANES),
            tok(RG_WIDTH), tok(RG_WIDTH), tok(RG_WIDTH, 1),
            _const_spec((LANES, ATTN_WIDTH)),
            _layer_spec(l, HY_WIDTH, D_MODEL), _layer_spec(l, ATTN_WIDTH, D_MODEL),
            _layer_spec(l, RG_WIDTH, D_MODEL), _layer_spec(l, D_MODEL, D_MODEL),
        ],
        out_specs=tok(D_MODEL),
        scratch_shapes=[
            pltpu.VMEM((ATTN_WIDTH // LANES, TOKEN_TILE, LANES), F32),
            pltpu.VMEM((TOKEN_TILE, LANES), F32),
        ],
        compiler_params=pltpu.CompilerParams(
            dimension_semantics=("arbitrary", "arbitrary"), vmem_limit_bytes=VMEM_LIMIT_BYTES),
        name="merge",
    )(h3, g, w_gate, b_gate, y_a, o0, o1, o2, l0, l1, l2, hf, hb, rg, expand,
      wp_hy, wp_attn, wp_rg, w_out)


def _mixer(h3, l, p, wb, consts):
    fre, fim, zpos, deltas, expand = consts
    hy, q0, q1, q2, rg = _inproj(h3, p["mix_norm"][l].reshape(1, -1), wb["w_in"], l)

    g_spec, g_nyq = _hy_filter(
        zpos, p["hy_w1"][l], p["hy_b1"][l], p["hy_w2"][l], p["hy_b2"][l], p["hy_w3"][l],
        p["hy_b3"][l], p["hy_freq"][l], p["hy_wout"][l], deltas, fre, fim)
    y_a = _hy_conv(hy, p["hy_conv_w"][l], p["hy_conv_b"][l].reshape(1, -1),
                   fre, fim, g_spec, g_nyq, p["hy_skip"][l])

    attn = [_attn_group(q, p["rel_bias"], g, dil)
            for g, (q, (_, dil)) in enumerate(zip((q0, q1, q2), ATTN_GROUPS))]

    rg_w = lambda d: jnp.concatenate(
        [_block_diag(p["rg_wa"][l, d]), _block_diag(p["rg_wx"][l, d])], axis=1).astype(BF16)
    rg_b = lambda d: jnp.concatenate([p["rg_ba"][l, d], p["rg_bx"][l, d]]).reshape(1, -1)
    hf, hb = _rglru(rg, p["rg_conv_w"][l], p["rg_conv_b"][l].reshape(1, -1),
                    rg_w(0), rg_w(1), rg_b(0), rg_b(1), p["rg_lambda"][l])

    return _merge(
        h3, p["mix_norm"][l].reshape(1, -1), wb["w_gate"], p["b_gate"][l].reshape(1, -1),
        y_a, attn, hf, hb, rg, expand,
        wb["w_proj_hy"], wb["w_proj_attn"], wb["w_proj_rg"], wb["w_out"], l)


def _forward(x, p):
    nb, seq, _ = x.shape
    fre, fim = _dft_mats(HY_P)
    deltas = jnp.abs(jnp.linspace(math.log(HY_DECAY_TARGET) / HY_FAST_DECAY,
                                  math.log(HY_DECAY_TARGET) / HY_SLOW_DECAY, HY_WIDTH, dtype=F32))
    head_of_lane = jnp.arange(ATTN_WIDTH, dtype=jnp.int32) // HEAD_DIM
    expand = (jnp.arange(LANES, dtype=jnp.int32)[:, None] == head_of_lane[None, :]).astype(BF16)
    consts = (fre, fim, _hy_positions(seq), deltas, expand)

    wb = {k: p[k].astype(BF16) for k in (
        "ffn1_wg", "ffn1_wu", "ffn1_wd", "ffn2_wg", "ffn2_wu", "ffn2_wd",
        "w_in", "w_gate", "w_proj_hy", "w_proj_attn", "w_proj_rg", "w_out")}

    def ffn(h3, which, l, g_out=None):
        out = _ffn(h3.reshape(nb * seq, D_MODEL), p[which + "_norm"][l].reshape(1, -1),
                   wb[which + "_wg"], wb[which + "_wu"], wb[which + "_wd"], l, g_out)
        return out.reshape(nb, seq, D_MODEL)

    h = x
    for l in range(DEPTH):
        h = ffn(h, "ffn1", l)
        h = _mixer(h, l, p, wb, consts)
        h = ffn(h, "ffn2", l, p["final_norm"].reshape(1, -1) if l == DEPTH - 1 else None)
    return h


def kernel(x, ffn1_norm, ffn1_wg, ffn1_wu, ffn1_wd, mix_norm, w_in, hy_conv_w, hy_conv_b, hy_w1, hy_b1, hy_w2, hy_b2, hy_w3, hy_b3, hy_freq, hy_wout, hy_skip, rel_bias, rg_conv_w, rg_conv_b, rg_wa, rg_ba, rg_wx, rg_bx, rg_lambda, w_gate, b_gate, w_proj_hy, w_proj_attn, w_proj_rg, w_out, ffn2_norm, ffn2_wg, ffn2_wu, ffn2_wd, final_norm):
    p = dict(
        ffn1_norm=ffn1_norm, ffn1_wg=ffn1_wg, ffn1_wu=ffn1_wu, ffn1_wd=ffn1_wd, mix_norm=mix_norm,
        w_in=w_in, hy_conv_w=hy_conv_w, hy_conv_b=hy_conv_b, hy_w1=hy_w1, hy_b1=hy_b1, hy_w2=hy_w2,
        hy_b2=hy_b2, hy_w3=hy_w3, hy_b3=hy_b3, hy_freq=hy_freq, hy_wout=hy_wout, hy_skip=hy_skip,
        rel_bias=rel_bias, rg_conv_w=rg_conv_w, rg_conv_b=rg_conv_b, rg_wa=rg_wa, rg_ba=rg_ba,
        rg_wx=rg_wx, rg_bx=rg_bx, rg_lambda=rg_lambda, w_gate=w_gate, b_gate=b_gate,
        w_proj_hy=w_proj_hy, w_proj_attn=w_proj_attn, w_proj_rg=w_proj_rg, w_out=w_out,
        ffn2_norm=ffn2_norm, ffn2_wg=ffn2_wg, ffn2_wu=ffn2_wu, ffn2_wd=ffn2_wd,
        final_norm=final_norm)
    return _forward(x, p)
```

```python
import functools
import math

import numpy as np
import jax
import jax.numpy as jnp
from jax import lax
from jax.experimental import pallas as pl
from jax.experimental.pallas import tpu as pltpu

F32 = jnp.float32
BF16 = jnp.bfloat16

D_MODEL = 1024
D_FF = 2816
DEPTH = 4
RMS_EPS = 1e-6

HY_WIDTH = 512
HY_ORDER = 2
HY_EMB = 33
HY_BANDS = 16
HY_HIDDEN = 64
HY_COLS = 3 * HY_WIDTH
HY_DECAY_TARGET = 1e-2
HY_FAST_DECAY = 0.3
HY_SLOW_DECAY = 1.5
HY_MOD_SHIFT = 0.05
HY_NORM_EPS = 1e-6

ATTN_GROUPS = ((128, 1), (512, 4), (2048, 16))
N_GROUPS = 3
HEADS = 8
HEAD_DIM = 64
ATTN_WIDTH = HEADS * HEAD_DIM
QKV_COLS = 3 * N_GROUPS * ATTN_WIDTH
BAND = 64
N_BUCKETS = 32
BUCKET_MAX_EXACT = 8
BUCKET_MAX_DIST = 1024
NEG_INF = -1e30
LOG2E = math.log2(math.e)
LN2 = math.log(2.0)
Q_SCALE = HEAD_DIM ** -0.5 * LOG2E

RG_WIDTH = 512
RG_BLOCKS = 8
RG_BLOCK = 64
RG_CONV = 4
RG_C = 8.0
RG_COLS = 2 * RG_WIDTH

N_BRANCH = 3

LANES = 128
SUBLANES = 8
VMEM_LIMIT_BYTES = 56 * 1024 * 1024

TOKEN_TILE = 512
Q_TILE = 128
K_WIN = Q_TILE + 2 * BAND
ATTN_UNROLL = 4
HY_CBLK = 256
HY_P = 512
RG_TT = 128
HALO_T = 8


def _const_spec(shape):
    nd = len(shape)
    return pl.BlockSpec(shape, lambda *_: (0,) * nd, pipeline_mode=pl.Buffered(1))


def _layer_spec(l, rows, cols, col_block=0):
    return pl.BlockSpec((1, rows, cols), lambda *_: (l, 0, col_block), pipeline_mode=pl.Buffered(1))


def _rms(x, g):
    ms = jnp.mean(x * x, axis=-1, keepdims=True)
    return x * lax.rsqrt(ms + RMS_EPS) * g


def _sigmoid(x):
    return 0.5 * jnp.tanh(0.5 * x) + 0.5


def _lane_slabs(x):
    return [x[:, s * LANES:(s + 1) * LANES] for s in range(x.shape[1] // LANES)]


def _ffn_body(h_ref, g_ref, wg_ref, wu_ref, wd_ref, *rest):
    gout_ref, o_ref = rest if len(rest) == 2 else (None, rest[0])
    x = h_ref[...]
    xb = _rms(x, g_ref[...]).astype(BF16)
    gate = jnp.dot(xb, wg_ref[0], preferred_element_type=F32)
    up = jnp.dot(xb, wu_ref[0], preferred_element_type=F32)
    act = (gate * _sigmoid(gate)) * up
    out = x + 0.5 * jnp.dot(act.astype(BF16), wd_ref[0], preferred_element_type=F32)
    o_ref[...] = out if gout_ref is None else _rms(out, gout_ref[...])


def _ffn(h, g, wg, wu, wd, l, g_out=None):
    n_tok = h.shape[0]
    extra = [] if g_out is None else [g_out]
    return pl.pallas_call(
        _ffn_body,
        out_shape=jax.ShapeDtypeStruct(h.shape, F32),
        grid=(n_tok // TOKEN_TILE,),
        in_specs=[
            pl.BlockSpec((TOKEN_TILE, D_MODEL), lambda i: (i, 0)),
            _const_spec((1, D_MODEL)),
            _layer_spec(l, D_MODEL, D_FF),
            _layer_spec(l, D_MODEL, D_FF),
            _layer_spec(l, D_FF, D_MODEL),
        ] + [_const_spec((1, D_MODEL))] * len(extra),
        out_specs=pl.BlockSpec((TOKEN_TILE, D_MODEL), lambda i: (i, 0)),
        compiler_params=pltpu.CompilerParams(
            dimension_semantics=("arbitrary",), vmem_limit_bytes=VMEM_LIMIT_BYTES),
        name="ffn",
    )(h, g, wg, wu, wd, *extra)


def _inproj_body(h_ref, g_ref, why_ref, *rest):
    w_qkv = rest[:3 * N_GROUPS]
    wrg_ref, hy_ref, q0_ref, q1_ref, q2_ref, rg_ref, xs_ref = rest[3 * N_GROUPS:]
    xn = _rms(h_ref[0], g_ref[...])
    xb = xn.astype(BF16)
    hy_ref[0] = jnp.dot(xb, why_ref[0], preferred_element_type=F32).astype(BF16)
    rg_ref[0] = jnp.dot(xb, wrg_ref[0], preferred_element_type=F32).astype(BF16)

    for s, slab in enumerate(_lane_slabs(xn)):
        xs_ref[s] = slab

    def by_residue(d):
        if d == 1:
            return xb
        n = TOKEN_TILE // d
        cols = [jnp.concatenate([xs_ref.at[s][pl.ds(r, n, stride=d), :] for r in range(d)], axis=0)
                for s in range(D_MODEL // LANES)]
        return jnp.concatenate(cols, axis=1).astype(BF16)

    for g, (q_ref, (_, d)) in enumerate(zip((q0_ref, q1_ref, q2_ref), ATTN_GROUPS)):
        lhs = by_residue(d)
        for part in range(3):
            out = jnp.dot(lhs, w_qkv[3 * g + part][0], preferred_element_type=F32)
            if part == 0:
                out = out * Q_SCALE
            q_ref[0, :, :, part * ATTN_WIDTH:(part + 1) * ATTN_WIDTH] = (
                out.astype(BF16).reshape(d, TOKEN_TILE // d, ATTN_WIDTH))


def _inproj(h3, g, w_in, l):
    nb, seq, _ = h3.shape
    dils = [d for _, d in ATTN_GROUPS]
    qshape = lambda d: jax.ShapeDtypeStruct((nb, d, seq // d, 3 * ATTN_WIDTH), BF16)
    qspec = lambda d: pl.BlockSpec((1, d, TOKEN_TILE // d, 3 * ATTN_WIDTH), lambda b, i: (b, 0, i, 0))
    qkv_block0 = HY_COLS // ATTN_WIDTH
    w_qkv_specs = [_layer_spec(l, D_MODEL, ATTN_WIDTH, qkv_block0 + part * N_GROUPS + grp)
                   for grp in range(N_GROUPS) for part in range(3)]
    return pl.pallas_call(
        _inproj_body,
        out_shape=(
            jax.ShapeDtypeStruct((nb, seq, HY_COLS), BF16),
            qshape(dils[0]), qshape(dils[1]), qshape(dils[2]),
            jax.ShapeDtypeStruct((nb, seq, RG_COLS), BF16),
        ),
        grid=(nb, seq // TOKEN_TILE),
        in_specs=[
            pl.BlockSpec((1, TOKEN_TILE, D_MODEL), lambda b, i: (b, i, 0)),
            _const_spec((1, D_MODEL)),
            _layer_spec(l, D_MODEL, HY_COLS, 0),
            *w_qkv_specs,
            _layer_spec(l, D_MODEL, RG_COLS, (HY_COLS + QKV_COLS) // RG_COLS),
        ],
        out_specs=(
            pl.BlockSpec((1, TOKEN_TILE, HY_COLS), lambda b, i: (b, i, 0)),
            qspec(dils[0]), qspec(dils[1]), qspec(dils[2]),
            pl.BlockSpec((1, TOKEN_TILE, RG_COLS), lambda b, i: (b, i, 0)),
        ),
        scratch_shapes=[pltpu.VMEM((D_MODEL // LANES, TOKEN_TILE, LANES), F32)],
        compiler_params=pltpu.CompilerParams(
            dimension_semantics=("arbitrary", "arbitrary"), vmem_limit_bytes=VMEM_LIMIT_BYTES),
        name="inproj",
    )(h3, g, w_in, *([w_in] * (3 * N_GROUPS)), w_in)


def _dft_mats(half):
    n = jnp.arange(half, dtype=jnp.int32)
    kn = (n[:, None] * n[None, :]) % (2 * half)
    ang = kn.astype(F32) * (math.pi / half)
    return jnp.cos(ang).astype(BF16), (-jnp.sin(ang)).astype(BF16)


def _hy_positions(seq):
    t = jnp.linspace(0.0, 1.0, seq, dtype=F32)[:, None]
    tr = jnp.arange(seq, dtype=F32)[:, None]
    wpos = 2.0 * math.pi * tr / seq
    fb = jnp.linspace(1e-4, HY_BANDS - 1, HY_BANDS, dtype=F32)[None, :]
    z = jnp.concatenate([t, jnp.cos(fb * wpos), -jnp.sin(fb * wpos)], axis=-1)
    return jnp.pad(z, ((0, 0), (0, LANES - HY_EMB)))


def _hy_filter_body(z_ref, w1_ref, b1_ref, w2_ref, b2_ref, w3_ref, b3_ref, fr_ref,
                    wf_ref, wb_ref, dl_ref, fre_ref, fim_ref,
                    g_ref, gn_ref, hdn_ref):
    hi = lax.Precision.HIGHEST
    seq = z_ref.shape[0]
    width = wf_ref.shape[-1]

    @pl.when((pl.program_id(0) == 0) & (pl.program_id(1) == 0))
    def _():
        fr = fr_ref[...]
        hdn = jnp.sin(fr * (jnp.dot(z_ref[...], w1_ref[...], precision=hi,
                                    preferred_element_type=F32) + b1_ref[...]))
        hdn = jnp.sin(fr * (jnp.dot(hdn, w2_ref[...], precision=hi,
                                    preferred_element_type=F32) + b2_ref[...]))
        hdn_ref[...] = jnp.sin(fr * (jnp.dot(hdn, w3_ref[...], precision=hi,
                                             preferred_element_type=F32) + b3_ref[...]))

    hdn = hdn_ref[...]
    row = lax.broadcasted_iota(jnp.int32, (seq, width), 0)
    t = row.astype(F32) * (1.0 / (seq - 1))
    decay = jnp.exp(-t * dl_ref[...]) + HY_MOD_SHIFT
    kf = jnp.dot(hdn, wf_ref[0], precision=hi, preferred_element_type=F32) * decay
    kb = jnp.dot(hdn, wb_ref[0], precision=hi, preferred_element_type=F32) * decay
    kb = jnp.where(row == 0, 0.0, kb)
    norm = (jnp.sum(jnp.abs(kf), axis=0, keepdims=True)
            + jnp.sum(jnp.abs(kb), axis=0, keepdims=True) + HY_NORM_EPS)
    kf = kf / norm
    kb = kb / norm

    nseg = seq // HY_P
    krow = lax.broadcasted_iota(jnp.int32, (HY_P, width), 0)
    sk = (1 - 2 * (krow & 1)).astype(F32)
    ck = jnp.where(krow == 0, 0.5 / HY_P, sk * (1.0 / HY_P))
    zero_row = jnp.zeros((1, width), F32)

    def dft(x):
        xb = x.astype(BF16)
        return (jnp.dot(fre_ref[...], xb, preferred_element_type=F32),
                jnp.dot(fim_ref[...], xb, preferred_element_type=F32),
                jnp.sum(x * sk, axis=0, keepdims=True))

    spectra = {}
    for e in range(nseg):
        spectra[e] = dft(kf[e * HY_P:(e + 1) * HY_P])
        lead = kb[e * HY_P:(e + 1) * HY_P]
        bre, bim, bny = dft(jnp.where(krow == 0, 0.0, lead))
        first = kb[(e + 1) * HY_P:(e + 1) * HY_P + 1] if e + 1 < nseg else zero_row
        spectra[-(e + 1)] = (first + sk * bre, -(sk * bim), first + bny)

    for delta in range(-(nseg - 1), nseg):
        are, aim, any_ = spectra[delta - 1]
        bre, bim, bny = spectra[delta]
        gre = (are + sk * bre) * ck
        gim = (aim + sk * bim) * ck
        g_ref[0, delta + nseg - 1, 0] = gre
        g_ref[0, delta + nseg - 1, 1] = gim - gre
        g_ref[0, delta + nseg - 1, 2] = gre + gim
        gn_ref[0, delta + nseg - 1:delta + nseg, :] = (any_ + bny) * (0.5 / HY_P)
    gn_ref[0, 2 * nseg - 1:2 * nseg, :] = zero_row


def _hy_filter(z, w1, b1, w2, b2, w3, b3, freq, wout, deltas, fre, fim):
    seq = z.shape[0]
    wf = wout[:, :HY_ORDER * HY_WIDTH].reshape(HY_HIDDEN, HY_ORDER, HY_WIDTH).transpose(1, 0, 2)
    wb = wout[:, HY_ORDER * HY_WIDTH:].reshape(HY_HIDDEN, HY_ORDER, HY_WIDTH).transpose(1, 0, 2)
    w1p = jnp.pad(w1, ((0, LANES - HY_EMB), (0, 0)))
    row = lambda a: a.reshape(1, -1)
    nseg = seq // HY_P
    return pl.pallas_call(
        _hy_filter_body,
        out_shape=(
            jax.ShapeDtypeStruct((HY_ORDER, 2 * nseg - 1, 3, HY_P, HY_WIDTH), F32),
            jax.ShapeDtypeStruct((HY_ORDER, 2 * nseg, HY_WIDTH), F32),
        ),
        grid=(HY_ORDER, HY_WIDTH // HY_CBLK),
        in_specs=[
            _const_spec((seq, LANES)),
            _const_spec((LANES, HY_HIDDEN)), _const_spec((1, HY_HIDDEN)),
            _const_spec((HY_HIDDEN, HY_HIDDEN)), _const_spec((1, HY_HIDDEN)),
            _const_spec((HY_HIDDEN, HY_HIDDEN)), _const_spec((1, HY_HIDDEN)),
            _const_spec((1, HY_HIDDEN)),
            pl.BlockSpec((1, HY_HIDDEN, HY_CBLK), lambda o, c: (o, 0, c)),
            pl.BlockSpec((1, HY_HIDDEN, HY_CBLK), lambda o, c: (o, 0, c)),
            pl.BlockSpec((1, HY_CBLK), lambda o, c: (0, c)),
            _const_spec((HY_P, HY_P)), _const_spec((HY_P, HY_P)),
        ],
        out_specs=(
            pl.BlockSpec((1, 2 * nseg - 1, 3, HY_P, HY_CBLK), lambda o, c: (o, 0, 0, 0, c)),
            pl.BlockSpec((1, 2 * nseg, HY_CBLK), lambda o, c: (o, 0, c)),
        ),
        scratch_shapes=[pltpu.VMEM((seq, HY_HIDDEN), F32)],
        compiler_params=pltpu.CompilerParams(
            dimension_semantics=("arbitrary", "arbitrary"), vmem_limit_bytes=VMEM_LIMIT_BYTES),
        name="hy_filter",
    )(z, w1p, row(b1), w2, row(b2), w3, row(b3), row(freq), wf, wb, row(deltas), fre, fim)


def _hy_conv_body(v_ref, x1_ref, x2_ref, wv_ref, w1_ref, w2_ref, bv_ref, b1_ref, b2_ref,
                  fre_ref, fim_ref, g_ref, gn_ref, skip_ref, o_ref,
                  z_ref, g1_ref, g2_ref, ure_ref, uim_ref, usum_ref):
    seq = v_ref.shape[1]
    nseg = seq // HY_P
    sgn = (1 - 2 * (lax.broadcasted_iota(jnp.int32, (HY_P, HY_CBLK), 0) & 1)).astype(F32)

    edge = lax.broadcasted_iota(jnp.int32, (SUBLANES, HY_CBLK), 0)

    def short_conv(x_ref, w_ref, b_ref, dst_ref):
        x = x_ref[0].astype(F32)
        w = w_ref[...]
        dst_ref[...] = (w[0:1] * pltpu.roll(x, 1, 0) + w[1:2] * x
                        + w[2:3] * pltpu.roll(x, seq - 1, 0) + b_ref[...])
        dst_ref[0:SUBLANES, :] = dst_ref[0:SUBLANES, :] - jnp.where(
            edge == 0, w[0:1] * x[seq - 1:seq], 0.0)
        dst_ref[seq - SUBLANES:seq, :] = dst_ref[seq - SUBLANES:seq, :] - jnp.where(
            edge == SUBLANES - 1, w[2:3] * x[0:1], 0.0)

    short_conv(v_ref, wv_ref, bv_ref, z_ref)
    short_conv(x1_ref, w1_ref, b1_ref, g1_ref)
    short_conv(x2_ref, w2_ref, b2_ref, g2_ref)
    skip = skip_ref[...]
    blk = lambda i: slice(i * HY_P, (i + 1) * HY_P)

    for o, gate_ref in enumerate((g1_ref, g2_ref)):
        nyq = []
        for j in range(nseg):
            x = z_ref[blk(j), :]
            xb = x.astype(BF16)
            ure = jnp.dot(fre_ref[...], xb, preferred_element_type=F32)
            uim = jnp.dot(fim_ref[...], xb, preferred_element_type=F32)
            ure_ref[j] = ure
            uim_ref[j] = uim
            usum_ref[j] = ure + uim
            nyq.append(jnp.sum(x * sgn, axis=0, keepdims=True))
        for i in range(nseg):
            k1 = k2 = k3 = zny = None
            for j in range(nseg):
                d = i - j + nseg - 1
                t1 = g_ref[o, d, 0] * usum_ref[j]
                t2 = g_ref[o, d, 1] * ure_ref[j]
                t3 = g_ref[o, d, 2] * uim_ref[j]
                tny = gn_ref[o, d:d + 1, :] * nyq[j]
                k1, k2, k3, zny = (t1, t2, t3, tny) if k1 is None else (k1 + t1, k2 + t2, k3 + t3, zny + tny)
            y = (jnp.dot(fre_ref[...], (k1 - k3).astype(BF16), preferred_element_type=F32)
                 + jnp.dot(fim_ref[...], (k1 + k2).astype(BF16), preferred_element_type=F32)
                 + sgn * zny)
            z = gate_ref[blk(i), :] * (y + skip[o:o + 1] * z_ref[blk(i), :])
            if o + 1 < HY_ORDER:
                z_ref[blk(i), :] = z
            else:
                o_ref[0, blk(i), :] = z.astype(BF16)


def _hy_conv(u_hy, conv_w, conv_b, fre, fim, g, gn, skip):
    nb, seq, _ = u_hy.shape
    ncb = HY_WIDTH // HY_CBLK
    nseg = seq // HY_P
    data = lambda part: pl.BlockSpec((1, seq, HY_CBLK), lambda c, b: (b, 0, part * ncb + c))
    wspec = lambda part: pl.BlockSpec((3, HY_CBLK), lambda c, b: (0, part * ncb + c))
    bspec = lambda part: pl.BlockSpec((1, HY_CBLK), lambda c, b: (0, part * ncb + c))
    return pl.pallas_call(
        _hy_conv_body,
        out_shape=jax.ShapeDtypeStruct((nb, seq, HY_WIDTH), BF16),
        grid=(ncb, nb),
        in_specs=[
            data(0), data(1), data(2),
            wspec(0), wspec(1), wspec(2),
            bspec(0), bspec(1), bspec(2),
            _const_spec((HY_P, HY_P)), _const_spec((HY_P, HY_P)),
            pl.BlockSpec((HY_ORDER, 2 * nseg - 1, 3, HY_P, HY_CBLK), lambda c, b: (0, 0, 0, 0, c),
                         pipeline_mode=pl.Buffered(1)),
            pl.BlockSpec((HY_ORDER, 2 * nseg, HY_CBLK), lambda c, b: (0, 0, c)),
            pl.BlockSpec((HY_ORDER, HY_CBLK), lambda c, b: (0, c)),
        ],
        out_specs=pl.BlockSpec((1, seq, HY_CBLK), lambda c, b: (b, 0, c)),
        scratch_shapes=[
            pltpu.VMEM((seq, HY_CBLK), F32), pltpu.VMEM((seq, HY_CBLK), F32),
            pltpu.VMEM((seq, HY_CBLK), F32),
            pltpu.VMEM((nseg, HY_P, HY_CBLK), F32), pltpu.VMEM((nseg, HY_P, HY_CBLK), F32),
            pltpu.VMEM((nseg, HY_P, HY_CBLK), F32),
        ],
        compiler_params=pltpu.CompilerParams(
            dimension_semantics=("arbitrary", "arbitrary"), vmem_limit_bytes=VMEM_LIMIT_BYTES),
        name="hy_conv",
    )(u_hy, u_hy, u_hy, conv_w, conv_w, conv_w, conv_b, conv_b, conv_b,
      fre, fim, g, gn, skip)


def _bucket_matrix(dilation):
    qi = np.arange(Q_TILE, dtype=np.int64)[:, None]
    kj = np.arange(K_WIN, dtype=np.int64)[None, :]
    delta = kj - BAND - qi
    rel = delta * dilation
    half = N_BUCKETS // 2
    n = np.abs(rel)
    nf = np.maximum(n, 1).astype(np.float32)
    large = BUCKET_MAX_EXACT + (
        np.log(nf / np.float32(BUCKET_MAX_EXACT)) / np.float32(math.log(BUCKET_MAX_DIST / BUCKET_MAX_EXACT))
        * np.float32(half - BUCKET_MAX_EXACT)).astype(np.int32)
    large = np.minimum(large, half - 1)
    bucket = np.where(rel > 0, half, 0) + np.where(n < BUCKET_MAX_EXACT, n, large)
    return np.where(np.abs(delta) <= BAND, bucket, -1).astype(np.int32)


def _attn_body(group, tbl_ref, bkt_ref, q_ref, k_ref, v_ref, o_ref, l_ref,
               bias_ref, kpad_ref, vpad_ref):
    _, dil, ls, _ = q_ref.shape
    tiles_per_res = ls // Q_TILE

    kcol = lax.broadcasted_iota(jnp.int32, (Q_TILE, K_WIN), 1)

    @pl.when(pl.program_id(0) == 0)
    def _():
        bkt = bkt_ref[...]
        for h in range(HEADS):
            acc = jnp.full((Q_TILE, K_WIN), NEG_INF, F32)
            for bk in range(N_BUCKETS):
                acc = jnp.where(bkt == bk, tbl_ref[bk, group * HEADS + h] * LOG2E, acc)
            no_head = jnp.where(kcol < BAND, NEG_INF, acc)
            bias_ref[0, h] = acc
            bias_ref[1, h] = no_head
            bias_ref[2, h] = jnp.where(kcol >= Q_TILE + BAND, NEG_INF, acc)
            bias_ref[3, h] = jnp.where(kcol >= Q_TILE + BAND, NEG_INF, no_head)
        zeros = jnp.zeros((dil, BAND, ATTN_WIDTH), BF16)
        kpad_ref[:, 0:BAND] = zeros
        kpad_ref[:, BAND + ls:BAND + ls + BAND] = zeros
        vpad_ref[:, 0:BAND] = zeros
        vpad_ref[:, BAND + ls:BAND + ls + BAND] = zeros

    kpad_ref[:, BAND:BAND + ls] = k_ref[0]
    vpad_ref[:, BAND:BAND + ls] = v_ref[0]

    lane = lax.broadcasted_iota(jnp.int32, (Q_TILE, LANES), 1)
    low_half = lane < HEAD_DIM

    def tile(t, carry):
        r = t // tiles_per_res
        jt = t % tiles_per_res
        j0 = pl.multiple_of(jt * Q_TILE, Q_TILE)
        edge = jnp.where(jt == 0, 1, 0) + jnp.where(jt == tiles_per_res - 1, 2, 0)
        lse_tile = jnp.zeros((Q_TILE, LANES), F32)
        for hp in range(HEADS // 2):
            cols = slice(hp * LANES, (hp + 1) * LANES)
            q2 = q_ref[0, r, pl.ds(j0, Q_TILE), cols]
            kw = kpad_ref[r, pl.ds(j0, K_WIN), cols]
            vw = vpad_ref[r, pl.ds(j0, K_WIN), cols]
            q_pair = jnp.concatenate(
                [jnp.where(low_half, q2, 0.0), jnp.where(low_half, 0.0, q2)], axis=0).astype(BF16)
            s_pair = lax.dot_general(q_pair, kw, (((1,), (1,)), ((), ())), preferred_element_type=F32)
            probs = []
            for half in range(2):
                h = 2 * hp + half
                s = s_pair[half * Q_TILE:(half + 1) * Q_TILE] + bias_ref[edge, h]
                m = jnp.max(s, axis=-1, keepdims=True)
                p = jnp.exp2(s - m)
                den = jnp.sum(p, axis=-1, keepdims=True)
                probs.append(p.astype(BF16))
                lse_tile = jnp.where(lane == h, m, jnp.where(lane == HEADS + h, den, lse_tile))
            pv = jnp.dot(jnp.concatenate(probs, axis=0), vw, preferred_element_type=F32)
            o_ref[0, r, pl.ds(j0, Q_TILE), cols] = jnp.where(
                low_half, pv[:Q_TILE], pv[Q_TILE:]).astype(BF16)
        l_ref[0, r, pl.ds(j0, Q_TILE), :] = lse_tile
        return carry

    lax.fori_loop(0, dil * tiles_per_res, tile, 0, unroll=ATTN_UNROLL)


def _attn_group(qkv_g, tbl, group, dilation):
    nb, _, ls, _ = qkv_g.shape
    part = lambda which: pl.BlockSpec((1, dilation, ls, ATTN_WIDTH), lambda b: (b, 0, 0, which))
    bkt = jnp.asarray(_bucket_matrix(dilation))
    return pl.pallas_call(
        functools.partial(_attn_body, group),
        out_shape=(
            jax.ShapeDtypeStruct((nb, dilation, ls, ATTN_WIDTH), BF16),
            jax.ShapeDtypeStruct((nb, dilation, ls, LANES), F32),
        ),
        grid=(nb,),
        in_specs=[
            pl.BlockSpec(memory_space=pltpu.SMEM),
            _const_spec((Q_TILE, K_WIN)),
            part(0), part(1), part(2),
        ],
        out_specs=(
            pl.BlockSpec((1, dilation, ls, ATTN_WIDTH), lambda b: (b, 0, 0, 0)),
            pl.BlockSpec((1, dilation, ls, LANES), lambda b: (b, 0, 0, 0)),
        ),
        scratch_shapes=[
            pltpu.VMEM((4, HEADS, Q_TILE, K_WIN), F32),
            pltpu.VMEM((dilation, ls + 2 * BAND, ATTN_WIDTH), BF16),
            pltpu.VMEM((dilation, ls + 2 * BAND, ATTN_WIDTH), BF16),
        ],
        compiler_params=pltpu.CompilerParams(
            dimension_semantics=("arbitrary",), vmem_limit_bytes=VMEM_LIMIT_BYTES),
        name=f"attn_g{group}",
    )(tbl, bkt, qkv_g, qkv_g, qkv_g)


def _rglru_body(nt, cw_ref, cb_ref, wf_ref, wb_ref, bf_ref, bb_ref, lam_ref,
                xf_ref, pf_ref, nf_ref, xb_ref, pb_ref, nb_ref,
                hf_ref, hb_ref, xw_ref, a_ref, u_ref, hs_ref, cf_ref, cbk_ref):
    i = pl.program_id(0)
    nbat, tt, width = xf_ref.shape
    n_slab = width // LANES
    n_grp = nbat // SUBLANES
    grp_rows = tt * SUBLANES
    halo_rows = HALO_T * SUBLANES

    @pl.when(i == 0)
    def _():
        cf_ref[...] = jnp.zeros_like(cf_ref)
        cbk_ref[...] = jnp.zeros_like(cbk_ref)

    def to_time_major(src_ref, n_t, row0, keep):
        for b in range(nbat):
            grp, b8 = divmod(b, SUBLANES)
            x = src_ref[b].astype(F32)
            x = x if keep is None else jnp.where(keep, x, 0.0)
            for s, slab in enumerate(_lane_slabs(x)):
                xw_ref.at[s * n_grp + grp][pl.ds(row0 + b8, n_t, stride=SUBLANES), :] = slab

    def gates(x_ref, p_ref, n_ref, tile_idx, w_ref, b_ref, lam):
        to_time_major(p_ref, HALO_T, 0, tile_idx > 0)
        to_time_major(x_ref, tt, halo_rows, None)
        to_time_major(n_ref, HALO_T, halo_rows + grp_rows, tile_idx < nt - 1)
        cw = cw_ref[...]

        def tap(k):
            start = halo_rows + (k - 2) * SUBLANES
            return jnp.concatenate(
                [jnp.concatenate([xw_ref[s * n_grp + grp, start:start + grp_rows, :]
                                  for s in range(n_slab)], axis=1) for grp in range(n_grp)], axis=0)

        xc = cw[0:1] * tap(0) + cw[1:2] * tap(1) + cw[2:3] * tap(2) + cw[3:4] * tap(3) + cb_ref[...]
        g = jnp.dot(xc.astype(BF16), w_ref[...], preferred_element_type=F32) + b_ref[...]
        r = _sigmoid(g[:, :width])
        gi = _sigmoid(g[:, width:])
        softplus = jnp.maximum(-lam, 0.0) + jnp.log(1.0 + jnp.exp(-jnp.abs(lam)))
        log_a = (-RG_C * softplus) * r
        a = jnp.exp(log_a)
        a_ref[...] = a
        one_m_a2 = 1.0 - a * a
        root = jnp.where(one_m_a2 > 0.0, one_m_a2 * lax.rsqrt(one_m_a2), 0.0)
        u_ref[...] = root * (gi * xc)

    def step(t, hs):
        r0 = pl.multiple_of(t * SUBLANES, SUBLANES)
        out = []
        for grp, h in enumerate(hs):
            rows = pl.ds(grp * grp_rows + r0, SUBLANES)
            h = a_ref[rows, :] * h + u_ref[rows, :]
            for s, slab in enumerate(_lane_slabs(h)):
                hs_ref[s * n_grp + grp, pl.ds(r0, SUBLANES), :] = slab
            out.append(h)
        return tuple(out)

    def scan(carry_ref, time_of):
        init = tuple(carry_ref[grp * SUBLANES:(grp + 1) * SUBLANES, :] for grp in range(n_grp))
        last = lax.fori_loop(0, tt, lambda k, hs: step(time_of(k), hs), init, unroll=8)
        carry_ref[...] = jnp.concatenate(last, axis=0)

    def to_batch_major(dst_ref):
        for b in range(nbat):
            grp, b8 = divmod(b, SUBLANES)
            dst_ref[b] = jnp.concatenate(
                [hs_ref.at[s * n_grp + grp][pl.ds(b8, tt, stride=SUBLANES), :] for s in range(n_slab)],
                axis=1).astype(BF16)

    gates(xf_ref, pf_ref, nf_ref, i, wf_ref, bf_ref, lam_ref[0:1])
    scan(cf_ref, lambda k: k)
    to_batch_major(hf_ref)

    gates(xb_ref, pb_ref, nb_ref, nt - 1 - i, wb_ref, bb_ref, lam_ref[1:2])
    scan(cbk_ref, lambda k: tt - 1 - k)
    to_batch_major(hb_ref)


def _rglru(rg, conv_w, conv_b, w_f, w_b, b_f, b_b, lam):
    nbat, seq, _ = rg.shape
    nt = seq // RG_TT
    hpt = RG_TT // HALO_T
    nh = seq // HALO_T
    cur = lambda f: pl.BlockSpec((nbat, RG_TT, RG_WIDTH), lambda i: (0, f(i), 0))
    prv = lambda f: pl.BlockSpec(
        (nbat, HALO_T, RG_WIDTH), lambda i: (0, jnp.maximum(f(i) * hpt - 1, 0), 0))
    nxt = lambda f: pl.BlockSpec(
        (nbat, HALO_T, RG_WIDTH), lambda i: (0, jnp.minimum((f(i) + 1) * hpt, nh - 1), 0))
    fw = lambda i: i
    bw = lambda i: nt - 1 - i
    n_slab = RG_WIDTH // LANES
    n_grp = nbat // SUBLANES
    return pl.pallas_call(
        functools.partial(_rglru_body, nt),
        out_shape=(
            jax.ShapeDtypeStruct((nbat, seq, RG_WIDTH), BF16),
            jax.ShapeDtypeStruct((nbat, seq, RG_WIDTH), BF16),
        ),
        grid=(nt,),
        in_specs=[
            _const_spec((RG_CONV, RG_WIDTH)), _const_spec((1, RG_WIDTH)),
            _const_spec((RG_WIDTH, 2 * RG_WIDTH)), _const_spec((RG_WIDTH, 2 * RG_WIDTH)),
            _const_spec((1, 2 * RG_WIDTH)), _const_spec((1, 2 * RG_WIDTH)),
            _const_spec((2, RG_WIDTH)),
            cur(fw), prv(fw), nxt(fw), cur(bw), prv(bw), nxt(bw),
        ],
        out_specs=(cur(fw), cur(bw)),
        scratch_shapes=[
            pltpu.VMEM((n_slab * n_grp, (RG_TT + 2 * HALO_T) * SUBLANES, LANES), F32),
            pltpu.VMEM((RG_TT * nbat, RG_WIDTH), F32),
            pltpu.VMEM((RG_TT * nbat, RG_WIDTH), F32),
            pltpu.VMEM((n_slab * n_grp, RG_TT * SUBLANES, LANES), F32),
            pltpu.VMEM((nbat, RG_WIDTH), F32),
            pltpu.VMEM((nbat, RG_WIDTH), F32),
        ],
        compiler_params=pltpu.CompilerParams(
            dimension_semantics=("arbitrary",), vmem_limit_bytes=VMEM_LIMIT_BYTES),
        name="rglru",
    )(conv_w, conv_b, w_f, w_b, b_f, b_b, lam, rg, rg, rg, rg, rg, rg)


def _block_diag(w):
    eye = jnp.eye(RG_BLOCKS, dtype=w.dtype)
    return jnp.einsum("hij,hk->hikj", w, eye).reshape(RG_WIDTH, RG_WIDTH)


def _merge_body(h_ref, g_ref, wg_ref, bg_ref, ya_ref, o0_ref, o1_ref, o2_ref,
                l0_ref, l1_ref, l2_ref, hf_ref, hb_ref, gate_ref, exp_ref,
                wphy_ref, wpat_ref, wprg_ref, wout_ref, out_ref, os_ref, ls_ref):
    x = h_ref[0]
    xb = _rms(x, g_ref[...]).astype(BF16)
    gates = _sigmoid(jnp.dot(xb, wg_ref[0], preferred_element_type=F32) + bg_ref[...])

    def token_order(o_ref, l_ref):
        d, n = o_ref.shape[1], o_ref.shape[2]
        if d == 1:
            return o_ref[0, 0].astype(F32), l_ref[0, 0]
        for r in range(d):
            for s, slab in enumerate(_lane_slabs(o_ref[0, r].astype(F32))):
                os_ref.at[s][pl.ds(r, n, stride=d), :] = slab
            ls_ref[pl.ds(r, n, stride=d), :] = l_ref[0, r]
        o = jnp.concatenate([os_ref[s] for s in range(ATTN_WIDTH // LANES)], axis=1)
        return o, ls_ref[...]

    o0, l0 = token_order(o0_ref, l0_ref)
    o1, l1 = token_order(o1_ref, l1_ref)
    o2, l2 = token_order(o2_ref, l2_ref)

    head_lane = lax.broadcasted_iota(jnp.int32, l0.shape, 1) < HEADS
    d0, d1, d2 = (pltpu.roll(l, LANES - HEADS, 1) for l in (l0, l1, l2))
    m = jnp.maximum(jnp.maximum(l0, l1), l2)
    e0, e1, e2 = jnp.exp2(l0 - m), jnp.exp2(l1 - m), jnp.exp2(l2 - m)
    inv = 1.0 / jnp.where(head_lane, e0 * d0 + e1 * d1 + e2 * d2, 1.0)

    def widen(w):
        return jnp.dot(w.astype(BF16), exp_ref[...], preferred_element_type=F32)

    y_b = widen(e0 * inv) * o0 + widen(e1 * inv) * o1 + widen(e2 * inv) * o2

    gt = gate_ref[0].astype(F32)
    gelu = 0.5 * gt * (1.0 + jnp.tanh(math.sqrt(2.0 / math.pi) * (gt + 0.044715 * (gt * gt * gt))))
    y_c = (hf_ref[0].astype(F32) + hb_ref[0].astype(F32)) * gelu

    merged = (gates[:, :D_MODEL]
              * jnp.dot(ya_ref[0], wphy_ref[0], preferred_element_type=F32)
              + gates[:, D_MODEL:2 * D_MODEL]
              * jnp.dot(y_b.astype(BF16), wpat_ref[0], preferred_element_type=F32)
              + gates[:, 2 * D_MODEL:]
              * jnp.dot(y_c.astype(BF16), wprg_ref[0], preferred_element_type=F32))
    out_ref[0] = x + jnp.dot(merged.astype(BF16), wout_ref[0], preferred_element_type=F32)


def _merge(h3, g, w_gate, b_gate, y_a, attn, hf, hb, rg, expand, wp_hy, wp_attn, wp_rg, w_out, l):
    nb, seq, _ = h3.shape
    tok = lambda width, col=0: pl.BlockSpec((1, TOKEN_TILE, width), lambda b, i: (b, i, col))
    res = lambda d, width: pl.BlockSpec((1, d, TOKEN_TILE // d, width), lambda b, i: (b, 0, i, 0))
    (o0, l0), (o1, l1), (o2, l2) = attn
    dils = [d for _, d in ATTN_GROUPS]
    return pl.pallas_call(
        _merge_body,
        out_shape=jax.ShapeDtypeStruct(h3.shape, F32),
        grid=(nb, seq // TOKEN_TILE),
        in_specs=[
            tok(D_MODEL), _const_spec((1, D_MODEL)),
            _layer_spec(l, D_MODEL, N_BRANCH * D_MODEL), _const_spec((1, N_BRANCH * D_MODEL)),
            tok(HY_WIDTH),
            res(dils[0], ATTN_WIDTH), res(dils[1], ATTN_WIDTH), res(dils[2], ATTN_WIDTH),
            res(dils[0], LANES), res(dils[1], LANES), res(dils[2], LANES),
            tok(RG_WIDTH), tok(RG_WIDTH), tok(RG_WIDTH, 1),
            _const_spec((LANES, ATTN_WIDTH)),
            _layer_spec(l, HY_WIDTH, D_MODEL), _layer_spec(l, ATTN_WIDTH, D_MODEL),
            _layer_spec(l, RG_WIDTH, D_MODEL), _layer_spec(l, D_MODEL, D_MODEL),
        ],
        out_specs=tok(D_MODEL),
        scratch_shapes=[
            pltpu.VMEM((ATTN_WIDTH // LANES, TOKEN_TILE, LANES), F32),
            pltpu.VMEM((TOKEN_TILE, LANES), F32),
        ],
        compiler_params=pltpu.CompilerParams(
            dimension_semantics=("arbitrary", "arbitrary"), vmem_limit_bytes=VMEM_LIMIT_BYTES),
        name="merge",
    )(h3, g, w_gate, b_gate, y_a, o0, o1, o2, l0, l1, l2, hf, hb, rg, expand,
      wp_hy, wp_attn, wp_rg, w_out)


def _mixer(h3, l, p, wb, consts):
    fre, fim, zpos, deltas, expand = consts
    hy, q0, q1, q2, rg = _inproj(h3, p["mix_norm"][l].reshape(1, -1), wb["w_in"], l)

    g_spec, g_nyq = _hy_filter(
        zpos, p["hy_w1"][l], p["hy_b1"][l], p["hy_w2"][l], p["hy_b2"][l], p["hy_w3"][l],
        p["hy_b3"][l], p["hy_freq"][l], p["hy_wout"][l], deltas, fre, fim)
    y_a = _hy_conv(hy, p["hy_conv_w"][l], p["hy_conv_b"][l].reshape(1, -1),
                   fre, fim, g_spec, g_nyq, p["hy_skip"][l])

    attn = [_attn_group(q, p["rel_bias"], g, dil)
            for g, (q, (_, dil)) in enumerate(zip((q0, q1, q2), ATTN_GROUPS))]

    rg_w = lambda d: jnp.concatenate(
        [_block_diag(p["rg_wa"][l, d]), _block_diag(p["rg_wx"][l, d])], axis=1).astype(BF16)
    rg_b = lambda d: jnp.concatenate([p["rg_ba"][l, d], p["rg_bx"][l, d]]).reshape(1, -1)
    hf, hb = _rglru(rg, p["rg_conv_w"][l], p["rg_conv_b"][l].reshape(1, -1),
                    rg_w(0), rg_w(1), rg_b(0), rg_b(1), p["rg_lambda"][l])

    return _merge(
        h3, p["mix_norm"][l].reshape(1, -1), wb["w_gate"], p["b_gate"][l].reshape(1, -1),
        y_a, attn, hf, hb, rg, expand,
        wb["w_proj_hy"], wb["w_proj_attn"], wb["w_proj_rg"], wb["w_out"], l)


def _forward(x, p):
    nb, seq, _ = x.shape
    fre, fim = _dft_mats(HY_P)
    deltas = jnp.abs(jnp.linspace(math.log(HY_DECAY_TARGET) / HY_FAST_DECAY,
                                  math.log(HY_DECAY_TARGET) / HY_SLOW_DECAY, HY_WIDTH, dtype=F32))
    head_of_lane = jnp.arange(ATTN_WIDTH, dtype=jnp.int32) // HEAD_DIM
    expand = (jnp.arange(LANES, dtype=jnp.int32)[:, None] == head_of_lane[None, :]).astype(BF16)
    consts = (fre, fim, _hy_positions(seq), deltas, expand)

    wb = {k: p[k].astype(BF16) for k in (
        "ffn1_wg", "ffn1_wu", "ffn1_wd", "ffn2_wg", "ffn2_wu", "ffn2_wd",
        "w_in", "w_gate", "w_proj_hy", "w_proj_attn", "w_proj_rg", "w_out")}

    def ffn(h3, which, l, g_out=None):
        out = _ffn(h3.reshape(nb * seq, D_MODEL), p[which + "_norm"][l].reshape(1, -1),
                   wb[which + "_wg"], wb[which + "_wu"], wb[which + "_wd"], l, g_out)
        return out.reshape(nb, seq, D_MODEL)

    h = x
    for l in range(DEPTH):
        h = ffn(h, "ffn1", l)
        h = _mixer(h, l, p, wb, consts)
        h = ffn(h, "ffn2", l, p["final_norm"].reshape(1, -1) if l == DEPTH - 1 else None)
    return h


def kernel(x, ffn1_norm, ffn1_wg, ffn1_wu, ffn1_wd, mix_norm, w_in, hy_conv_w, hy_conv_b, hy_w1, hy_b1, hy_w2, hy_b2, hy_w3, hy_b3, hy_freq, hy_wout, hy_skip, rel_bias, rg_conv_w, rg_conv_b, rg_wa, rg_ba, rg_wx, rg_bx, rg_lambda, w_gate, b_gate, w_proj_hy, w_proj_attn, w_proj_rg, w_out, ffn2_norm, ffn2_wg, ffn2_wu, ffn2_wd, final_norm):
    p = dict(
        ffn1_norm=ffn1_norm, ffn1_wg=ffn1_wg, ffn1_wu=ffn1_wu, ffn1_wd=ffn1_wd, mix_norm=mix_norm,
        w_in=w_in, hy_conv_w=hy_conv_w, hy_conv_b=hy_conv_b, hy_w1=hy_w1, hy_b1=hy_b1, hy_w2=hy_w2,
        hy_b2=hy_b2, hy_w3=hy_w3, hy_b3=hy_b3, hy_freq=hy_freq, hy_wout=hy_wout, hy_skip=hy_skip,
        rel_bias=rel_bias, rg_conv_w=rg_conv_w, rg_conv_b=rg_conv_b, rg_wa=rg_wa, rg_ba=rg_ba,
        rg_wx=rg_wx, rg_bx=rg_bx, rg_lambda=rg_lambda, w_gate=w_gate, b_gate=b_gate,
        w_proj_hy=w_proj_hy, w_proj_attn=w_proj_attn, w_proj_rg=w_proj_rg, w_out=w_out,
        ffn2_norm=ffn2_norm, ffn2_wg=ffn2_wg, ffn2_wu=ffn2_wu, ffn2_wd=ffn2_wd,
        final_norm=final_norm)
    return _forward(x, p)
```

```python
import functools
import math

import numpy as np
import jax
import jax.numpy as jnp
from jax import lax
from jax.experimental import pallas as pl
from jax.experimental.pallas import tpu as pltpu

F32 = jnp.float32
BF16 = jnp.bfloat16

D_MODEL = 1024
D_FF = 2816
DEPTH = 4
RMS_EPS = 1e-6

HY_WIDTH = 512
HY_ORDER = 2
HY_EMB = 33
HY_BANDS = 16
HY_HIDDEN = 64
HY_COLS = 3 * HY_WIDTH
HY_DECAY_TARGET = 1e-2
HY_FAST_DECAY = 0.3
HY_SLOW_DECAY = 1.5
HY_MOD_SHIFT = 0.05
HY_NORM_EPS = 1e-6

ATTN_GROUPS = ((128, 1), (512, 4), (2048, 16))
N_GROUPS = 3
HEADS = 8
HEAD_DIM = 64
ATTN_WIDTH = HEADS * HEAD_DIM
QKV_COLS = 3 * N_GROUPS * ATTN_WIDTH
BAND = 64
N_BUCKETS = 32
BUCKET_MAX_EXACT = 8
BUCKET_MAX_DIST = 1024
NEG_INF = -1e30
LOG2E = math.log2(math.e)
LN2 = math.log(2.0)
Q_SCALE = HEAD_DIM ** -0.5 * LOG2E

RG_WIDTH = 512
RG_BLOCKS = 8
RG_BLOCK = 64
RG_CONV = 4
RG_C = 8.0
RG_COLS = 2 * RG_WIDTH

N_BRANCH = 3

LANES = 128
SUBLANES = 8
VMEM_LIMIT_BYTES = 56 * 1024 * 1024

TOKEN_TILE = 512
Q_TILE = 128
K_WIN = Q_TILE + 2 * BAND
ATTN_UNROLL = 16
HY_CBLK = 256
HY_P = 512
RG_TT = 128
HALO_T = 8


def _const_spec(shape):
    nd = len(shape)
    return pl.BlockSpec(shape, lambda *_: (0,) * nd, pipeline_mode=pl.Buffered(1))


def _layer_spec(l, rows, cols, col_block=0):
    return pl.BlockSpec((1, rows, cols), lambda *_: (l, 0, col_block), pipeline_mode=pl.Buffered(1))


def _rms(x, g):
    ms = jnp.mean(x * x, axis=-1, keepdims=True)
    return x * lax.rsqrt(ms + RMS_EPS) * g


def _sigmoid(x):
    return 0.5 * jnp.tanh(0.5 * x) + 0.5


def _lane_slabs(x):
    return [x[:, s * LANES:(s + 1) * LANES] for s in range(x.shape[1] // LANES)]


def _ffn_body(h_ref, g_ref, wg_ref, wu_ref, wd_ref, *rest):
    gout_ref, o_ref = rest if len(rest) == 2 else (None, rest[0])
    x = h_ref[...]
    xb = _rms(x, g_ref[...]).astype(BF16)
    gate = jnp.dot(xb, wg_ref[0], preferred_element_type=F32)
    up = jnp.dot(xb, wu_ref[0], preferred_element_type=F32)
    act = (gate * _sigmoid(gate)) * up
    out = x + 0.5 * jnp.dot(act.astype(BF16), wd_ref[0], preferred_element_type=F32)
    o_ref[...] = out if gout_ref is None else _rms(out, gout_ref[...])


def _ffn(h, g, wg, wu, wd, l, g_out=None):
    n_tok = h.shape[0]
    extra = [] if g_out is None else [g_out]
    return pl.pallas_call(
        _ffn_body,
        out_shape=jax.ShapeDtypeStruct(h.shape, F32),
        grid=(n_tok // TOKEN_TILE,),
        in_specs=[
            pl.BlockSpec((TOKEN_TILE, D_MODEL), lambda i: (i, 0)),
            _const_spec((1, D_MODEL)),
            _layer_spec(l, D_MODEL, D_FF),
            _layer_spec(l, D_MODEL, D_FF),
            _layer_spec(l, D_FF, D_MODEL),
        ] + [_const_spec((1, D_MODEL))] * len(extra),
        out_specs=pl.BlockSpec((TOKEN_TILE, D_MODEL), lambda i: (i, 0)),
        compiler_params=pltpu.CompilerParams(
            dimension_semantics=("arbitrary",), vmem_limit_bytes=VMEM_LIMIT_BYTES),
        name="ffn",
    )(h, g, wg, wu, wd, *extra)


def _inproj_body(h_ref, g_ref, why_ref, *rest):
    w_qkv = rest[:3 * N_GROUPS]
    wrg_ref, hy_ref, q0_ref, q1_ref, q2_ref, rg_ref, xs_ref = rest[3 * N_GROUPS:]
    xn = _rms(h_ref[0], g_ref[...])
    xb = xn.astype(BF16)
    hy_ref[0] = jnp.dot(xb, why_ref[0], preferred_element_type=F32).astype(BF16)
    rg_ref[0] = jnp.dot(xb, wrg_ref[0], preferred_element_type=F32).astype(BF16)

    for s, slab in enumerate(_lane_slabs(xn)):
        xs_ref[s] = slab

    def by_residue(d):
        if d == 1:
            return xb
        n = TOKEN_TILE // d
        cols = [jnp.concatenate([xs_ref.at[s][pl.ds(r, n, stride=d), :] for r in range(d)], axis=0)
                for s in range(D_MODEL // LANES)]
        return jnp.concatenate(cols, axis=1).astype(BF16)

    for g, (q_ref, (_, d)) in enumerate(zip((q0_ref, q1_ref, q2_ref), ATTN_GROUPS)):
        lhs = by_residue(d)
        for part in range(3):
            out = jnp.dot(lhs, w_qkv[3 * g + part][0], preferred_element_type=F32)
            if part == 0:
                out = out * Q_SCALE
            q_ref[0, :, :, part * ATTN_WIDTH:(part + 1) * ATTN_WIDTH] = (
                out.astype(BF16).reshape(d, TOKEN_TILE // d, ATTN_WIDTH))


def _inproj(h3, g, w_in, l):
    nb, seq, _ = h3.shape
    dils = [d for _, d in ATTN_GROUPS]
    qshape = lambda d: jax.ShapeDtypeStruct((nb, d, seq // d, 3 * ATTN_WIDTH), BF16)
    qspec = lambda d: pl.BlockSpec((1, d, TOKEN_TILE // d, 3 * ATTN_WIDTH), lambda b, i: (b, 0, i, 0))
    qkv_block0 = HY_COLS // ATTN_WIDTH
    w_qkv_specs = [_layer_spec(l, D_MODEL, ATTN_WIDTH, qkv_block0 + part * N_GROUPS + grp)
                   for grp in range(N_GROUPS) for part in range(3)]
    return pl.pallas_call(
        _inproj_body,
        out_shape=(
            jax.ShapeDtypeStruct((nb, seq, HY_COLS), BF16),
            qshape(dils[0]), qshape(dils[1]), qshape(dils[2]),
            jax.ShapeDtypeStruct((nb, seq, RG_COLS), BF16),
        ),
        grid=(nb, seq // TOKEN_TILE),
        in_specs=[
            pl.BlockSpec((1, TOKEN_TILE, D_MODEL), lambda b, i: (b, i, 0)),
            _const_spec((1, D_MODEL)),
            _layer_spec(l, D_MODEL, HY_COLS, 0),
            *w_qkv_specs,
            _layer_spec(l, D_MODEL, RG_COLS, (HY_COLS + QKV_COLS) // RG_COLS),
        ],
        out_specs=(
            pl.BlockSpec((1, TOKEN_TILE, HY_COLS), lambda b, i: (b, i, 0)),
            qspec(dils[0]), qspec(dils[1]), qspec(dils[2]),
            pl.BlockSpec((1, TOKEN_TILE, RG_COLS), lambda b, i: (b, i, 0)),
        ),
        scratch_shapes=[pltpu.VMEM((D_MODEL // LANES, TOKEN_TILE, LANES), F32)],
        compiler_params=pltpu.CompilerParams(
            dimension_semantics=("arbitrary", "arbitrary"), vmem_limit_bytes=VMEM_LIMIT_BYTES),
        name="inproj",
    )(h3, g, w_in, *([w_in] * (3 * N_GROUPS)), w_in)


def _dft_mats(half):
    n = jnp.arange(half, dtype=jnp.int32)
    kn = (n[:, None] * n[None, :]) % (2 * half)
    ang = kn.astype(F32) * (math.pi / half)
    return jnp.cos(ang).astype(BF16), (-jnp.sin(ang)).astype(BF16)


def _hy_positions(seq):
    t = jnp.linspace(0.0, 1.0, seq, dtype=F32)[:, None]
    tr = jnp.arange(seq, dtype=F32)[:, None]
    wpos = 2.0 * math.pi * tr / seq
    fb = jnp.linspace(1e-4, HY_BANDS - 1, HY_BANDS, dtype=F32)[None, :]
    z = jnp.concatenate([t, jnp.cos(fb * wpos), -jnp.sin(fb * wpos)], axis=-1)
    return jnp.pad(z, ((0, 0), (0, LANES - HY_EMB)))


def _hy_filter_body(z_ref, w1_ref, b1_ref, w2_ref, b2_ref, w3_ref, b3_ref, fr_ref,
                    wf_ref, wb_ref, dl_ref, fre_ref, fim_ref,
                    g_ref, gn_ref, hdn_ref):
    hi = lax.Precision.HIGHEST
    seq = z_ref.shape[0]
    width = wf_ref.shape[-1]

    @pl.when((pl.program_id(0) == 0) & (pl.program_id(1) == 0))
    def _():
        fr = fr_ref[...]
        hdn = jnp.sin(fr * (jnp.dot(z_ref[...], w1_ref[...], precision=hi,
                                    preferred_element_type=F32) + b1_ref[...]))
        hdn = jnp.sin(fr * (jnp.dot(hdn, w2_ref[...], precision=hi,
                                    preferred_element_type=F32) + b2_ref[...]))
        hdn_ref[...] = jnp.sin(fr * (jnp.dot(hdn, w3_ref[...], precision=hi,
                                             preferred_element_type=F32) + b3_ref[...]))

    hdn = hdn_ref[...]
    row = lax.broadcasted_iota(jnp.int32, (seq, width), 0)
    t = row.astype(F32) * (1.0 / (seq - 1))
    decay = jnp.exp(-t * dl_ref[...]) + HY_MOD_SHIFT
    kf = jnp.dot(hdn, wf_ref[0], precision=hi, preferred_element_type=F32) * decay
    kb = jnp.dot(hdn, wb_ref[0], precision=hi, preferred_element_type=F32) * decay
    kb = jnp.where(row == 0, 0.0, kb)
    norm = (jnp.sum(jnp.abs(kf), axis=0, keepdims=True)
            + jnp.sum(jnp.abs(kb), axis=0, keepdims=True) + HY_NORM_EPS)
    kf = kf / norm
    kb = kb / norm

    nseg = seq // HY_P
    krow = lax.broadcasted_iota(jnp.int32, (HY_P, width), 0)
    sk = (1 - 2 * (krow & 1)).astype(F32)
    ck = jnp.where(krow == 0, 0.5 / HY_P, sk * (1.0 / HY_P))
    zero_row = jnp.zeros((1, width), F32)

    def dft(x):
        xb = x.astype(BF16)
        return (jnp.dot(fre_ref[...], xb, preferred_element_type=F32),
                jnp.dot(fim_ref[...], xb, preferred_element_type=F32),
                jnp.sum(x * sk, axis=0, keepdims=True))

    spectra = {}
    for e in range(nseg):
        spectra[e] = dft(kf[e * HY_P:(e + 1) * HY_P])
        lead = kb[e * HY_P:(e + 1) * HY_P]
        bre, bim, bny = dft(jnp.where(krow == 0, 0.0, lead))
        first = kb[(e + 1) * HY_P:(e + 1) * HY_P + 1] if e + 1 < nseg else zero_row
        spectra[-(e + 1)] = (first + sk * bre, -(sk * bim), first + bny)

    for delta in range(-(nseg - 1), nseg):
        are, aim, any_ = spectra[delta - 1]
        bre, bim, bny = spectra[delta]
        gre = (are + sk * bre) * ck
        gim = (aim + sk * bim) * ck
        g_ref[0, delta + nseg - 1, 0] = gre
        g_ref[0, delta + nseg - 1, 1] = gim - gre
        g_ref[0, delta + nseg - 1, 2] = gre + gim
        gn_ref[0, delta + nseg - 1:delta + nseg, :] = (any_ + bny) * (0.5 / HY_P)
    gn_ref[0, 2 * nseg - 1:2 * nseg, :] = zero_row


def _hy_filter(z, w1, b1, w2, b2, w3, b3, freq, wout, deltas, fre, fim):
    seq = z.shape[0]
    wf = wout[:, :HY_ORDER * HY_WIDTH].reshape(HY_HIDDEN, HY_ORDER, HY_WIDTH).transpose(1, 0, 2)
    wb = wout[:, HY_ORDER * HY_WIDTH:].reshape(HY_HIDDEN, HY_ORDER, HY_WIDTH).transpose(1, 0, 2)
    w1p = jnp.pad(w1, ((0, LANES - HY_EMB), (0, 0)))
    row = lambda a: a.reshape(1, -1)
    nseg = seq // HY_P
    return pl.pallas_call(
        _hy_filter_body,
        out_shape=(
            jax.ShapeDtypeStruct((HY_ORDER, 2 * nseg - 1, 3, HY_P, HY_WIDTH), F32),
            jax.ShapeDtypeStruct((HY_ORDER, 2 * nseg, HY_WIDTH), F32),
        ),
        grid=(HY_ORDER, HY_WIDTH // HY_CBLK),
        in_specs=[
            _const_spec((seq, LANES)),
            _const_spec((LANES, HY_HIDDEN)), _const_spec((1, HY_HIDDEN)),
            _const_spec((HY_HIDDEN, HY_HIDDEN)), _const_spec((1, HY_HIDDEN)),
            _const_spec((HY_HIDDEN, HY_HIDDEN)), _const_spec((1, HY_HIDDEN)),
            _const_spec((1, HY_HIDDEN)),
            pl.BlockSpec((1, HY_HIDDEN, HY_CBLK), lambda o, c: (o, 0, c)),
            pl.BlockSpec((1, HY_HIDDEN, HY_CBLK), lambda o, c: (o, 0, c)),
            pl.BlockSpec((1, HY_CBLK), lambda o, c: (0, c)),
            _const_spec((HY_P, HY_P)), _const_spec((HY_P, HY_P)),
        ],
        out_specs=(
            pl.BlockSpec((1, 2 * nseg - 1, 3, HY_P, HY_CBLK), lambda o, c: (o, 0, 0, 0, c)),
            pl.BlockSpec((1, 2 * nseg, HY_CBLK), lambda o, c: (o, 0, c)),
        ),
        scratch_shapes=[pltpu.VMEM((seq, HY_HIDDEN), F32)],
        compiler_params=pltpu.CompilerParams(
            dimension_semantics=("arbitrary", "arbitrary"), vmem_limit_bytes=VMEM_LIMIT_BYTES),
        name="hy_filter",
    )(z, w1p, row(b1), w2, row(b2), w3, row(b3), row(freq), wf, wb, row(deltas), fre, fim)


def _hy_conv_body(v_ref, x1_ref, x2_ref, wv_ref, w1_ref, w2_ref, bv_ref, b1_ref, b2_ref,
                  fre_ref, fim_ref, g_ref, gn_ref, skip_ref, o_ref,
                  z_ref, g1_ref, g2_ref, ure_ref, uim_ref, usum_ref):
    seq = v_ref.shape[1]
    nseg = seq // HY_P
    sgn = (1 - 2 * (lax.broadcasted_iota(jnp.int32, (HY_P, HY_CBLK), 0) & 1)).astype(F32)

    edge = lax.broadcasted_iota(jnp.int32, (SUBLANES, HY_CBLK), 0)

    def short_conv(x_ref, w_ref, b_ref, dst_ref):
        x = x_ref[0].astype(F32)
        w = w_ref[...]
        dst_ref[...] = (w[0:1] * pltpu.roll(x, 1, 0) + w[1:2] * x
                        + w[2:3] * pltpu.roll(x, seq - 1, 0) + b_ref[...])
        dst_ref[0:SUBLANES, :] = dst_ref[0:SUBLANES, :] - jnp.where(
            edge == 0, w[0:1] * x[seq - 1:seq], 0.0)
        dst_ref[seq - SUBLANES:seq, :] = dst_ref[seq - SUBLANES:seq, :] - jnp.where(
            edge == SUBLANES - 1, w[2:3] * x[0:1], 0.0)

    short_conv(v_ref, wv_ref, bv_ref, z_ref)
    short_conv(x1_ref, w1_ref, b1_ref, g1_ref)
    short_conv(x2_ref, w2_ref, b2_ref, g2_ref)
    skip = skip_ref[...]
    blk = lambda i: slice(i * HY_P, (i + 1) * HY_P)

    for o, gate_ref in enumerate((g1_ref, g2_ref)):
        nyq = []
        for j in range(nseg):
            x = z_ref[blk(j), :]
            xb = x.astype(BF16)
            ure = jnp.dot(fre_ref[...], xb, preferred_element_type=F32)
            uim = jnp.dot(fim_ref[...], xb, preferred_element_type=F32)
            ure_ref[j] = ure
            uim_ref[j] = uim
            usum_ref[j] = ure + uim
            nyq.append(jnp.sum(x * sgn, axis=0, keepdims=True))
        for i in range(nseg):
            k1 = k2 = k3 = zny = None
            for j in range(nseg):
                d = i - j + nseg - 1
                t1 = g_ref[o, d, 0] * usum_ref[j]
                t2 = g_ref[o, d, 1] * ure_ref[j]
                t3 = g_ref[o, d, 2] * uim_ref[j]
                tny = gn_ref[o, d:d + 1, :] * nyq[j]
                k1, k2, k3, zny = (t1, t2, t3, tny) if k1 is None else (k1 + t1, k2 + t2, k3 + t3, zny + tny)
            y = (jnp.dot(fre_ref[...], (k1 - k3).astype(BF16), preferred_element_type=F32)
                 + jnp.dot(fim_ref[...], (k1 + k2).astype(BF16), preferred_element_type=F32)
                 + sgn * zny)
            z = gate_ref[blk(i), :] * (y + skip[o:o + 1] * z_ref[blk(i), :])
            if o + 1 < HY_ORDER:
                z_ref[blk(i), :] = z
            else:
                o_ref[0, blk(i), :] = z.astype(BF16)


def _hy_conv(u_hy, conv_w, conv_b, fre, fim, g, gn, skip):
    nb, seq, _ = u_hy.shape
    ncb = HY_WIDTH // HY_CBLK
    nseg = seq // HY_P
    data = lambda part: pl.BlockSpec((1, seq, HY_CBLK), lambda c, b: (b, 0, part * ncb + c))
    wspec = lambda part: pl.BlockSpec((3, HY_CBLK), lambda c, b: (0, part * ncb + c))
    bspec = lambda part: pl.BlockSpec((1, HY_CBLK), lambda c, b: (0, part * ncb + c))
    return pl.pallas_call(
        _hy_conv_body,
        out_shape=jax.ShapeDtypeStruct((nb, seq, HY_WIDTH), BF16),
        grid=(ncb, nb),
        in_specs=[
            data(0), data(1), data(2),
            wspec(0), wspec(1), wspec(2),
            bspec(0), bspec(1), bspec(2),
            _const_spec((HY_P, HY_P)), _const_spec((HY_P, HY_P)),
            pl.BlockSpec((HY_ORDER, 2 * nseg - 1, 3, HY_P, HY_CBLK), lambda c, b: (0, 0, 0, 0, c),
                         pipeline_mode=pl.Buffered(1)),
            pl.BlockSpec((HY_ORDER, 2 * nseg, HY_CBLK), lambda c, b: (0, 0, c)),
            pl.BlockSpec((HY_ORDER, HY_CBLK), lambda c, b: (0, c)),
        ],
        out_specs=pl.BlockSpec((1, seq, HY_CBLK), lambda c, b: (b, 0, c)),
        scratch_shapes=[
            pltpu.VMEM((seq, HY_CBLK), F32), pltpu.VMEM((seq, HY_CBLK), F32),
            pltpu.VMEM((seq, HY_CBLK), F32),
            pltpu.VMEM((nseg, HY_P, HY_CBLK), F32), pltpu.VMEM((nseg, HY_P, HY_CBLK), F32),
            pltpu.VMEM((nseg, HY_P, HY_CBLK), F32),
        ],
        compiler_params=pltpu.CompilerParams(
            dimension_semantics=("arbitrary", "arbitrary"), vmem_limit_bytes=VMEM_LIMIT_BYTES),
        name="hy_conv",
    )(u_hy, u_hy, u_hy, conv_w, conv_w, conv_w, conv_b, conv_b, conv_b,
      fre, fim, g, gn, skip)


def _bucket_matrix(dilation):
    qi = np.arange(Q_TILE, dtype=np.int64)[:, None]
    kj = np.arange(K_WIN, dtype=np.int64)[None, :]
    delta = kj - BAND - qi
    rel = delta * dilation
    half = N_BUCKETS // 2
    n = np.abs(rel)
    nf = np.maximum(n, 1).astype(np.float32)
    large = BUCKET_MAX_EXACT + (
        np.log(nf / np.float32(BUCKET_MAX_EXACT)) / np.float32(math.log(BUCKET_MAX_DIST / BUCKET_MAX_EXACT))
        * np.float32(half - BUCKET_MAX_EXACT)).astype(np.int32)
    large = np.minimum(large, half - 1)
    bucket = np.where(rel > 0, half, 0) + np.where(n < BUCKET_MAX_EXACT, n, large)
    return np.where(np.abs(delta) <= BAND, bucket, -1).astype(np.int32)


def _attn_body(group, tbl_ref, bkt_ref, q_ref, k_ref, v_ref, o_ref, l_ref,
               bias_ref, kpad_ref, vpad_ref):
    _, dil, ls, _ = q_ref.shape
    tiles_per_res = ls // Q_TILE

    kcol = lax.broadcasted_iota(jnp.int32, (Q_TILE, K_WIN), 1)

    @pl.when(pl.program_id(0) == 0)
    def _():
        bkt = bkt_ref[...]
        for h in range(HEADS):
            acc = jnp.full((Q_TILE, K_WIN), NEG_INF, F32)
            for bk in range(N_BUCKETS):
                acc = jnp.where(bkt == bk, tbl_ref[bk, group * HEADS + h] * LOG2E, acc)
            no_head = jnp.where(kcol < BAND, NEG_INF, acc)
            bias_ref[0, h] = acc
            bias_ref[1, h] = no_head
            bias_ref[2, h] = jnp.where(kcol >= Q_TILE + BAND, NEG_INF, acc)
            bias_ref[3, h] = jnp.where(kcol >= Q_TILE + BAND, NEG_INF, no_head)
        zeros = jnp.zeros((dil, BAND, ATTN_WIDTH), BF16)
        kpad_ref[:, 0:BAND] = zeros
        kpad_ref[:, BAND + ls:BAND + ls + BAND] = zeros
        vpad_ref[:, 0:BAND] = zeros
        vpad_ref[:, BAND + ls:BAND + ls + BAND] = zeros

    kpad_ref[:, BAND:BAND + ls] = k_ref[0]
    vpad_ref[:, BAND:BAND + ls] = v_ref[0]

    lane = lax.broadcasted_iota(jnp.int32, (Q_TILE, LANES), 1)
    low_half = lane < HEAD_DIM

    def tile(t, carry):
        r = t // tiles_per_res
        jt = t % tiles_per_res
        j0 = pl.multiple_of(jt * Q_TILE, Q_TILE)
        edge = jnp.where(jt == 0, 1, 0) + jnp.where(jt == tiles_per_res - 1, 2, 0)
        lse_tile = jnp.zeros((Q_TILE, LANES), F32)
        for hp in range(HEADS // 2):
            cols = slice(hp * LANES, (hp + 1) * LANES)
            q2 = q_ref[0, r, pl.ds(j0, Q_TILE), cols]
            kw = kpad_ref[r, pl.ds(j0, K_WIN), cols]
            vw = vpad_ref[r, pl.ds(j0, K_WIN), cols]
            q_pair = jnp.concatenate(
                [jnp.where(low_half, q2, 0.0), jnp.where(low_half, 0.0, q2)], axis=0).astype(BF16)
            s_pair = lax.dot_general(q_pair, kw, (((1,), (1,)), ((), ())), preferred_element_type=F32)
            probs = []
            for half in range(2):
                h = 2 * hp + half
                s = s_pair[half * Q_TILE:(half + 1) * Q_TILE] + bias_ref[edge, h]
                m = jnp.max(s, axis=-1, keepdims=True)
                p = jnp.exp2(s - m)
                den = jnp.sum(p, axis=-1, keepdims=True)
                probs.append(p.astype(BF16))
                lse_tile = jnp.where(lane == h, m, jnp.where(lane == HEADS + h, den, lse_tile))
            pv = jnp.dot(jnp.concatenate(probs, axis=0), vw, preferred_element_type=F32)
            o_ref[0, r, pl.ds(j0, Q_TILE), cols] = jnp.where(
                low_half, pv[:Q_TILE], pv[Q_TILE:]).astype(BF16)
        l_ref[0, r, pl.ds(j0, Q_TILE), :] = lse_tile
        return carry

    lax.fori_loop(0, dil * tiles_per_res, tile, 0, unroll=ATTN_UNROLL)


def _attn_group(qkv_g, tbl, group, dilation):
    nb, _, ls, _ = qkv_g.shape
    part = lambda which: pl.BlockSpec((1, dilation, ls, ATTN_WIDTH), lambda b: (b, 0, 0, which))
    bkt = jnp.asarray(_bucket_matrix(dilation))
    return pl.pallas_call(
        functools.partial(_attn_body, group),
        out_shape=(
            jax.ShapeDtypeStruct((nb, dilation, ls, ATTN_WIDTH), BF16),
            jax.ShapeDtypeStruct((nb, dilation, ls, LANES), F32),
        ),
        grid=(nb,),
        in_specs=[
            pl.BlockSpec(memory_space=pltpu.SMEM),
            _const_spec((Q_TILE, K_WIN)),
            part(0), part(1), part(2),
        ],
        out_specs=(
            pl.BlockSpec((1, dilation, ls, ATTN_WIDTH), lambda b: (b, 0, 0, 0)),
            pl.BlockSpec((1, dilation, ls, LANES), lambda b: (b, 0, 0, 0)),
        ),
        scratch_shapes=[
            pltpu.VMEM((4, HEADS, Q_TILE, K_WIN), F32),
            pltpu.VMEM((dilation, ls + 2 * BAND, ATTN_WIDTH), BF16),
            pltpu.VMEM((dilation, ls + 2 * BAND, ATTN_WIDTH), BF16),
        ],
        compiler_params=pltpu.CompilerParams(
            dimension_semantics=("arbitrary",), vmem_limit_bytes=VMEM_LIMIT_BYTES),
        name=f"attn_g{group}",
    )(tbl, bkt, qkv_g, qkv_g, qkv_g)


def _rglru_body(nt, cw_ref, cb_ref, wf_ref, wb_ref, bf_ref, bb_ref, lam_ref,
                xf_ref, pf_ref, nf_ref, xb_ref, pb_ref, nb_ref,
                hf_ref, hb_ref, xw_ref, a_ref, u_ref, hs_ref, cf_ref, cbk_ref):
    i = pl.program_id(0)
    nbat, tt, width = xf_ref.shape
    n_slab = width // LANES
    n_grp = nbat // SUBLANES
    grp_rows = tt * SUBLANES
    halo_rows = HALO_T * SUBLANES

    @pl.when(i == 0)
    def _():
        cf_ref[...] = jnp.zeros_like(cf_ref)
        cbk_ref[...] = jnp.zeros_like(cbk_ref)

    def to_time_major(src_ref, n_t, row0, keep):
        for b in range(nbat):
            grp, b8 = divmod(b, SUBLANES)
            x = src_ref[b].astype(F32)
            x = x if keep is None else jnp.where(keep, x, 0.0)
            for s, slab in enumerate(_lane_slabs(x)):
                xw_ref.at[s * n_grp + grp][pl.ds(row0 + b8, n_t, stride=SUBLANES), :] = slab

    def gates(x_ref, p_ref, n_ref, tile_idx, w_ref, b_ref, lam):
        to_time_major(p_ref, HALO_T, 0, tile_idx > 0)
        to_time_major(x_ref, tt, halo_rows, None)
        to_time_major(n_ref, HALO_T, halo_rows + grp_rows, tile_idx < nt - 1)
        cw = cw_ref[...]

        def tap(k):
            start = halo_rows + (k - 2) * SUBLANES
            return jnp.concatenate(
                [jnp.concatenate([xw_ref[s * n_grp + grp, start:start + grp_rows, :]
                                  for s in range(n_slab)], axis=1) for grp in range(n_grp)], axis=0)

        xc = cw[0:1] * tap(0) + cw[1:2] * tap(1) + cw[2:3] * tap(2) + cw[3:4] * tap(3) + cb_ref[...]
        g = jnp.dot(xc.astype(BF16), w_ref[...], preferred_element_type=F32) + b_ref[...]
        r = _sigmoid(g[:, :width])
        gi = _sigmoid(g[:, width:])
        softplus = jnp.maximum(-lam, 0.0) + jnp.log(1.0 + jnp.exp(-jnp.abs(lam)))
        log_a = (-RG_C * softplus) * r
        a = jnp.exp(log_a)
        a_ref[...] = a
        one_m_a2 = 1.0 - a * a
        root = jnp.where(one_m_a2 > 0.0, one_m_a2 * lax.rsqrt(one_m_a2), 0.0)
        u_ref[...] = root * (gi * xc)

    def step(t, hs):
        r0 = pl.multiple_of(t * SUBLANES, SUBLANES)
        out = []
        for grp, h in enumerate(hs):
            rows = pl.ds(grp * grp_rows + r0, SUBLANES)
            h = a_ref[rows, :] * h + u_ref[rows, :]
            for s, slab in enumerate(_lane_slabs(h)):
                hs_ref[s * n_grp + grp, pl.ds(r0, SUBLANES), :] = slab
            out.append(h)
        return tuple(out)

    def scan(carry_ref, time_of):
        init = tuple(carry_ref[grp * SUBLANES:(grp + 1) * SUBLANES, :] for grp in range(n_grp))
        last = lax.fori_loop(0, tt, lambda k, hs: step(time_of(k), hs), init, unroll=8)
        carry_ref[...] = jnp.concatenate(last, axis=0)

    def to_batch_major(dst_ref):
        for b in range(nbat):
            grp, b8 = divmod(b, SUBLANES)
            dst_ref[b] = jnp.concatenate(
                [hs_ref.at[s * n_grp + grp][pl.ds(b8, tt, stride=SUBLANES), :] for s in range(n_slab)],
                axis=1).astype(BF16)

    gates(xf_ref, pf_ref, nf_ref, i, wf_ref, bf_ref, lam_ref[0:1])
    scan(cf_ref, lambda k: k)
    to_batch_major(hf_ref)

    gates(xb_ref, pb_ref, nb_ref, nt - 1 - i, wb_ref, bb_ref, lam_ref[1:2])
    scan(cbk_ref, lambda k: tt - 1 - k)
    to_batch_major(hb_ref)


def _rglru(rg, conv_w, conv_b, w_f, w_b, b_f, b_b, lam):
    nbat, seq, _ = rg.shape
    nt = seq // RG_TT
    hpt = RG_TT // HALO_T
    nh = seq // HALO_T
    cur = lambda f: pl.BlockSpec((nbat, RG_TT, RG_WIDTH), lambda i: (0, f(i), 0))
    prv = lambda f: pl.BlockSpec(
        (nbat, HALO_T, RG_WIDTH), lambda i: (0, jnp.maximum(f(i) * hpt - 1, 0), 0))
    nxt = lambda f: pl.BlockSpec(
        (nbat, HALO_T, RG_WIDTH), lambda i: (0, jnp.minimum((f(i) + 1) * hpt, nh - 1), 0))
    fw = lambda i: i
    bw = lambda i: nt - 1 - i
    n_slab = RG_WIDTH // LANES
    n_grp = nbat // SUBLANES
    return pl.pallas_call(
        functools.partial(_rglru_body, nt),
        out_shape=(
            jax.ShapeDtypeStruct((nbat, seq, RG_WIDTH), BF16),
            jax.ShapeDtypeStruct((nbat, seq, RG_WIDTH), BF16),
        ),
        grid=(nt,),
        in_specs=[
            _const_spec((RG_CONV, RG_WIDTH)), _const_spec((1, RG_WIDTH)),
            _const_spec((RG_WIDTH, 2 * RG_WIDTH)), _const_spec((RG_WIDTH, 2 * RG_WIDTH)),
            _const_spec((1, 2 * RG_WIDTH)), _const_spec((1, 2 * RG_WIDTH)),
            _const_spec((2, RG_WIDTH)),
            cur(fw), prv(fw), nxt(fw), cur(bw), prv(bw), nxt(bw),
        ],
        out_specs=(cur(fw), cur(bw)),
        scratch_shapes=[
            pltpu.VMEM((n_slab * n_grp, (RG_TT + 2 * HALO_T) * SUBLANES, LANES), F32),
            pltpu.VMEM((RG_TT * nbat, RG_WIDTH), F32),
            pltpu.VMEM((RG_TT * nbat, RG_WIDTH), F32),
            pltpu.VMEM((n_slab * n_grp, RG_TT * SUBLANES, LANES), F32),
            pltpu.VMEM((nbat, RG_WIDTH), F32),
            pltpu.VMEM((nbat, RG_WIDTH), F32),
        ],
        compiler_params=pltpu.CompilerParams(
            dimension_semantics=("arbitrary",), vmem_limit_bytes=VMEM_LIMIT_BYTES),
        name="rglru",
    )(conv_w, conv_b, w_f, w_b, b_f, b_b, lam, rg, rg, rg, rg, rg, rg)


def _block_diag(w):
    eye = jnp.eye(RG_BLOCKS, dtype=w.dtype)
    return jnp.einsum("hij,hk->hikj", w, eye).reshape(RG_WIDTH, RG_WIDTH)


def _merge_body(h_ref, g_ref, wg_ref, bg_ref, ya_ref, o0_ref, o1_ref, o2_ref,
                l0_ref, l1_ref, l2_ref, hf_ref, hb_ref, gate_ref, exp_ref,
                wphy_ref, wpat_ref, wprg_ref, wout_ref, out_ref, os_ref, ls_ref):
    x = h_ref[0]
    xb = _rms(x, g_ref[...]).astype(BF16)
    gates = _sigmoid(jnp.dot(xb, wg_ref[0], preferred_element_type=F32) + bg_ref[...])

    def token_order(o_ref, l_ref):
        d, n = o_ref.shape[1], o_ref.shape[2]
        if d == 1:
            return o_ref[0, 0].astype(F32), l_ref[0, 0]
        for r in range(d):
            for s, slab in enumerate(_lane_slabs(o_ref[0, r].astype(F32))):
                os_ref.at[s][pl.ds(r, n, stride=d), :] = slab
            ls_ref[pl.ds(r, n, stride=d), :] = l_ref[0, r]
        o = jnp.concatenate([os_ref[s] for s in range(ATTN_WIDTH // LANES)], axis=1)
        return o, ls_ref[...]

    o0, l0 = token_order(o0_ref, l0_ref)
    o1, l1 = token_order(o1_ref, l1_ref)
    o2, l2 = token_order(o2_ref, l2_ref)

    head_lane = lax.broadcasted_iota(jnp.int32, l0.shape, 1) < HEADS
    d0, d1, d2 = (pltpu.roll(l, LANES - HEADS, 1) for l in (l0, l1, l2))
    m = jnp.maximum(jnp.maximum(l0, l1), l2)
    e0, e1, e2 = jnp.exp2(l0 - m), jnp.exp2(l1 - m), jnp.exp2(l2 - m)
    inv = 1.0 / jnp.where(head_lane, e0 * d0 + e1 * d1 + e2 * d2, 1.0)

    def widen(w):
        return jnp.dot(w.astype(BF16), exp_ref[...], preferred_element_type=F32)

    y_b = widen(e0 * inv) * o0 + widen(e1 * inv) * o1 + widen(e2 * inv) * o2

    gt = gate_ref[0].astype(F32)
    gelu = 0.5 * gt * (1.0 + jnp.tanh(math.sqrt(2.0 / math.pi) * (gt + 0.044715 * (gt * gt * gt))))
    y_c = (hf_ref[0].astype(F32) + hb_ref[0].astype(F32)) * gelu

    merged = (gates[:, :D_MODEL]
              * jnp.dot(ya_ref[0], wphy_ref[0], preferred_element_type=F32)
              + gates[:, D_MODEL:2 * D_MODEL]
              * jnp.dot(y_b.astype(BF16), wpat_ref[0], preferred_element_type=F32)
              + gates[:, 2 * D_MODEL:]
              * jnp.dot(y_c.astype(BF16), wprg_ref[0], preferred_element_type=F32))
    out_ref[0] = x + jnp.dot(merged.astype(BF16), wout_ref[0], preferred_element_type=F32)


def _merge(h3, g, w_gate, b_gate, y_a, attn, hf, hb, rg, expand, wp_hy, wp_attn, wp_rg, w_out, l):
    nb, seq, _ = h3.shape
    tok = lambda width, col=0: pl.BlockSpec((1, TOKEN_TILE, width), lambda b, i: (b, i, col))
    res = lambda d, width: pl.BlockSpec((1, d, TOKEN_TILE // d, width), lambda b, i: (b, 0, i, 0))
    (o0, l0), (o1, l1), (o2, l2) = attn
    dils = [d for _, d in ATTN_GROUPS]
    return pl.pallas_call(
        _merge_body,
        out_shape=jax.ShapeDtypeStruct(h3.shape, F32),
        grid=(nb, seq // TOKEN_TILE),
        in_specs=[
            tok(D_MODEL), _const_spec((1, D_MODEL)),
            _layer_spec(l, D_MODEL, N_BRANCH * D_MODEL), _const_spec((1, N_BRANCH * D_MODEL)),
            tok(HY_WIDTH),
            res(dils[0], ATTN_WIDTH), res(dils[1], ATTN_WIDTH), res(dils[2], ATTN_WIDTH),
            res(dils[0], LANES), res(dils[1], LANES), res(dils[2], LANES),
            tok(RG_WIDTH), tok(RG_WIDTH), tok(RG_WIDTH, 1),
            _const_spec((LANES, ATTN_WIDTH)),
            _layer_spec(l, HY_WIDTH, D_MODEL), _layer_spec(l, ATTN_WIDTH, D_MODEL),
            _layer_spec(l, RG_WIDTH, D_MODEL), _layer_spec(l, D_MODEL, D_MODEL),
        ],
        out_specs=tok(D_MODEL),
        scratch_shapes=[
            pltpu.VMEM((ATTN_WIDTH // LANES, TOKEN_TILE, LANES), F32),
            pltpu.VMEM((TOKEN_TILE, LANES), F32),
        ],
        compiler_params=pltpu.CompilerParams(
            dimension_semantics=("arbitrary", "arbitrary"), vmem_limit_bytes=VMEM_LIMIT_BYTES),
        name="merge",
    )(h3, g, w_gate, b_gate, y_a, o0, o1, o2, l0, l1, l2, hf, hb, rg, expand,
      wp_hy, wp_attn, wp_rg, w_out)


def _mixer(h3, l, p, wb, consts):
    fre, fim, zpos, deltas, expand = consts
    hy, q0, q1, q2, rg = _inproj(h3, p["mix_norm"][l].reshape(1, -1), wb["w_in"], l)

    g_spec, g_nyq = _hy_filter(
        zpos, p["hy_w1"][l], p["hy_b1"][l], p["hy_w2"][l], p["hy_b2"][l], p["hy_w3"][l],
        p["hy_b3"][l], p["hy_freq"][l], p["hy_wout"][l], deltas, fre, fim)
    y_a = _hy_conv(hy, p["hy_conv_w"][l], p["hy_conv_b"][l].reshape(1, -1),
                   fre, fim, g_spec, g_nyq, p["hy_skip"][l])

    attn = [_attn_group(q, p["rel_bias"], g, dil)
            for g, (q, (_, dil)) in enumerate(zip((q0, q1, q2), ATTN_GROUPS))]

    rg_w = lambda d: jnp.concatenate(
        [_block_diag(p["rg_wa"][l, d]), _block_diag(p["rg_wx"][l, d])], axis=1).astype(BF16)
    rg_b = lambda d: jnp.concatenate([p["rg_ba"][l, d], p["rg_bx"][l, d]]).reshape(1, -1)
    hf, hb = _rglru(rg, p["rg_conv_w"][l], p["rg_conv_b"][l].reshape(1, -1),
                    rg_w(0), rg_w(1), rg_b(0), rg_b(1), p["rg_lambda"][l])

    return _merge(
        h3, p["mix_norm"][l].reshape(1, -1), wb["w_gate"], p["b_gate"][l].reshape(1, -1),
        y_a, attn, hf, hb, rg, expand,
        wb["w_proj_hy"], wb["w_proj_attn"], wb["w_proj_rg"], wb["w_out"], l)


def _forward(x, p):
    nb, seq, _ = x.shape
    fre, fim = _dft_mats(HY_P)
    deltas = jnp.abs(jnp.linspace(math.log(HY_DECAY_TARGET) / HY_FAST_DECAY,
                                  math.log(HY_DECAY_TARGET) / HY_SLOW_DECAY, HY_WIDTH, dtype=F32))
    head_of_lane = jnp.arange(ATTN_WIDTH, dtype=jnp.int32) // HEAD_DIM
    expand = (jnp.arange(LANES, dtype=jnp.int32)[:, None] == head_of_lane[None, :]).astype(BF16)
    consts = (fre, fim, _hy_positions(seq), deltas, expand)

    wb = {k: p[k].astype(BF16) for k in (
        "ffn1_wg", "ffn1_wu", "ffn1_wd", "ffn2_wg", "ffn2_wu", "ffn2_wd",
        "w_in", "w_gate", "w_proj_hy", "w_proj_attn", "w_proj_rg", "w_out")}

    def ffn(h3, which, l, g_out=None):
        out = _ffn(h3.reshape(nb * seq, D_MODEL), p[which + "_norm"][l].reshape(1, -1),
                   wb[which + "_wg"], wb[which + "_wu"], wb[which + "_wd"], l, g_out)
        return out.reshape(nb, seq, D_MODEL)

    h = x
    for l in range(DEPTH):
        h = ffn(h, "ffn1", l)
        h = _mixer(h, l, p, wb, consts)
        h = ffn(h, "ffn2", l, p["final_norm"].reshape(1, -1) if l == DEPTH - 1 else None)
    return h


def kernel(x, ffn1_norm, ffn1_wg, ffn1_wu, ffn1_wd, mix_norm, w_in, hy_conv_w, hy_conv_b, hy_w1, hy_b1, hy_w2, hy_b2, hy_w3, hy_b3, hy_freq, hy_wout, hy_skip, rel_bias, rg_conv_w, rg_conv_b, rg_wa, rg_ba, rg_wx, rg_bx, rg_lambda, w_gate, b_gate, w_proj_hy, w_proj_attn, w_proj_rg, w_out, ffn2_norm, ffn2_wg, ffn2_wu, ffn2_wd, final_norm):
    p = dict(
        ffn1_norm=ffn1_norm, ffn1_wg=ffn1_wg, ffn1_wu=ffn1_wu, ffn1_wd=ffn1_wd, mix_norm=mix_norm,
        w_in=w_in, hy_conv_w=hy_conv_w, hy_conv_b=hy_conv_b, hy_w1=hy_w1, hy_b1=hy_b1, hy_w2=hy_w2,
        hy_b2=hy_b2, hy_w3=hy_w3, hy_b3=hy_b3, hy_freq=hy_freq, hy_wout=hy_wout, hy_skip=hy_skip,
        rel_bias=rel_bias, rg_conv_w=rg_conv_w, rg_conv_b=rg_conv_b, rg_wa=rg_wa, rg_ba=rg_ba,
        rg_wx=rg_wx, rg_bx=rg_bx, rg_lambda=rg_lambda, w_gate=w_gate, b_gate=b_gate,
        w_proj_hy=w_proj_hy, w_proj_attn=w_proj_attn, w_proj_rg=w_proj_rg, w_out=w_out,
        ffn2_norm=ffn2_norm, ffn2_wg=ffn2_wg, ffn2_wu=ffn2_wu, ffn2_wd=ffn2_wd,
        final_norm=final_norm)
    return _forward(x, p)
```

```python
import functools
import math

import numpy as np
import jax
import jax.numpy as jnp
from jax import lax
from jax.experimental import pallas as pl
from jax.experimental.pallas import tpu as pltpu

F32 = jnp.float32
BF16 = jnp.bfloat16

D_MODEL = 1024
D_FF = 2816
DEPTH = 4
RMS_EPS = 1e-6

HY_WIDTH = 512
HY_ORDER = 2
HY_EMB = 33
HY_BANDS = 16
HY_HIDDEN = 64
HY_COLS = 3 * HY_WIDTH
HY_DECAY_TARGET = 1e-2
HY_FAST_DECAY = 0.3
HY_SLOW_DECAY = 1.5
HY_MOD_SHIFT = 0.05
HY_NORM_EPS = 1e-6

ATTN_GROUPS = ((128, 1), (512, 4), (2048, 16))
N_GROUPS = 3
HEADS = 8
HEAD_DIM = 64
ATTN_WIDTH = HEADS * HEAD_DIM
QKV_COLS = 3 * N_GROUPS * ATTN_WIDTH
BAND = 64
N_BUCKETS = 32
BUCKET_MAX_EXACT = 8
BUCKET_MAX_DIST = 1024
NEG_INF = -1e30
LOG2E = math.log2(math.e)
Q_SCALE = HEAD_DIM ** -0.5 * LOG2E

RG_WIDTH = 512
RG_BLOCKS = 8
RG_CONV = 4
RG_C = 8.0
RG_COLS = 2 * RG_WIDTH

N_BRANCH = 3

LANES = 128
SUBLANES = 8
VMEM_LIMIT_BYTES = 56 * 1024 * 1024

TOKEN_TILE = 512
Q_TILE = 128
K_WIN = Q_TILE + 2 * BAND
ATTN_UNROLL = 16
N_EDGE_CASES = 4
RG_SCAN_UNROLL = 16
HY_CBLK = 256
HY_P = 512
RG_TT = 64
HALO_T = 8


def _const_spec(shape):
    nd = len(shape)
    return pl.BlockSpec(shape, lambda *_: (0,) * nd, pipeline_mode=pl.Buffered(1))


def _layer_spec(l, rows, cols, col_block=0):
    return pl.BlockSpec((1, rows, cols), lambda *_: (l, 0, col_block), pipeline_mode=pl.Buffered(1))


def _rms(x, g):
    ms = jnp.mean(x * x, axis=-1, keepdims=True)
    return x * lax.rsqrt(ms + RMS_EPS) * g


def _sigmoid(x):
    return 0.5 * jnp.tanh(0.5 * x) + 0.5


def _lane_slabs(x):
    return [x[:, s * LANES:(s + 1) * LANES] for s in range(x.shape[1] // LANES)]


def _ffn_body(h_ref, g_ref, wg_ref, wu_ref, wd_ref, *rest):
    gout_ref, o_ref = rest if len(rest) == 2 else (None, rest[0])
    x = h_ref[...]
    xb = _rms(x, g_ref[...]).astype(BF16)
    gate = jnp.dot(xb, wg_ref[0], preferred_element_type=F32)
    up = jnp.dot(xb, wu_ref[0], preferred_element_type=F32)
    act = (gate * _sigmoid(gate)) * up
    out = x + 0.5 * jnp.dot(act.astype(BF16), wd_ref[0], preferred_element_type=F32)
    o_ref[...] = out if gout_ref is None else _rms(out, gout_ref[...])


def _ffn(h, g, wg, wu, wd, l, g_out=None):
    n_tok = h.shape[0]
    extra = [] if g_out is None else [g_out]
    return pl.pallas_call(
        _ffn_body,
        out_shape=jax.ShapeDtypeStruct(h.shape, F32),
        grid=(n_tok // TOKEN_TILE,),
        in_specs=[
            pl.BlockSpec((TOKEN_TILE, D_MODEL), lambda i: (i, 0)),
            _const_spec((1, D_MODEL)),
            _layer_spec(l, D_MODEL, D_FF),
            _layer_spec(l, D_MODEL, D_FF),
            _layer_spec(l, D_FF, D_MODEL),
        ] + [_const_spec((1, D_MODEL))] * len(extra),
        out_specs=pl.BlockSpec((TOKEN_TILE, D_MODEL), lambda i: (i, 0)),
        compiler_params=pltpu.CompilerParams(
            dimension_semantics=("arbitrary",), vmem_limit_bytes=VMEM_LIMIT_BYTES),
        name="ffn",
    )(h, g, wg, wu, wd, *extra)


def _inproj_body(h_ref, g_ref, why_ref, *rest):
    w_qkv = rest[:3 * N_GROUPS]
    wrg_ref, hy_ref, q0_ref, q1_ref, q2_ref, rg_ref, xs_ref = rest[3 * N_GROUPS:]
    xn = _rms(h_ref[0], g_ref[...])
    xb = xn.astype(BF16)
    hy_ref[0] = jnp.dot(xb, why_ref[0], preferred_element_type=F32).astype(BF16)
    rg_ref[0] = jnp.dot(xb, wrg_ref[0], preferred_element_type=F32).astype(BF16)

    for s, slab in enumerate(_lane_slabs(xn)):
        xs_ref[s] = slab

    def by_residue(d):
        if d == 1:
            return xb
        n = TOKEN_TILE // d
        cols = [jnp.concatenate([xs_ref.at[s][pl.ds(r, n, stride=d), :] for r in range(d)], axis=0)
                for s in range(D_MODEL // LANES)]
        return jnp.concatenate(cols, axis=1).astype(BF16)

    for g, (q_ref, (_, d)) in enumerate(zip((q0_ref, q1_ref, q2_ref), ATTN_GROUPS)):
        lhs = by_residue(d)
        for part in range(3):
            out = jnp.dot(lhs, w_qkv[3 * g + part][0], preferred_element_type=F32)
            if part == 0:
                out = out * Q_SCALE
            q_ref[0, :, :, part * ATTN_WIDTH:(part + 1) * ATTN_WIDTH] = (
                out.astype(BF16).reshape(d, TOKEN_TILE // d, ATTN_WIDTH))


def _inproj(h3, g, w_in, l):
    nb, seq, _ = h3.shape
    dils = [d for _, d in ATTN_GROUPS]
    qshape = lambda d: jax.ShapeDtypeStruct((nb, d, seq // d, 3 * ATTN_WIDTH), BF16)
    qspec = lambda d: pl.BlockSpec((1, d, TOKEN_TILE // d, 3 * ATTN_WIDTH), lambda b, i: (b, 0, i, 0))
    qkv_block0 = HY_COLS // ATTN_WIDTH
    w_qkv_specs = [_layer_spec(l, D_MODEL, ATTN_WIDTH, qkv_block0 + part * N_GROUPS + grp)
                   for grp in range(N_GROUPS) for part in range(3)]
    return pl.pallas_call(
        _inproj_body,
        out_shape=(
            jax.ShapeDtypeStruct((nb, seq, HY_COLS), BF16),
            qshape(dils[0]), qshape(dils[1]), qshape(dils[2]),
            jax.ShapeDtypeStruct((nb, seq, RG_COLS), BF16),
        ),
        grid=(nb, seq // TOKEN_TILE),
        in_specs=[
            pl.BlockSpec((1, TOKEN_TILE, D_MODEL), lambda b, i: (b, i, 0)),
            _const_spec((1, D_MODEL)),
            _layer_spec(l, D_MODEL, HY_COLS, 0),
            *w_qkv_specs,
            _layer_spec(l, D_MODEL, RG_COLS, (HY_COLS + QKV_COLS) // RG_COLS),
        ],
        out_specs=(
            pl.BlockSpec((1, TOKEN_TILE, HY_COLS), lambda b, i: (b, i, 0)),
            qspec(dils[0]), qspec(dils[1]), qspec(dils[2]),
            pl.BlockSpec((1, TOKEN_TILE, RG_COLS), lambda b, i: (b, i, 0)),
        ),
        scratch_shapes=[pltpu.VMEM((D_MODEL // LANES, TOKEN_TILE, LANES), F32)],
        compiler_params=pltpu.CompilerParams(
            dimension_semantics=("arbitrary", "arbitrary"), vmem_limit_bytes=VMEM_LIMIT_BYTES),
        name="inproj",
    )(h3, g, w_in, *([w_in] * (3 * N_GROUPS)), w_in)


def _dft_mats(half):
    n = jnp.arange(half, dtype=jnp.int32)
    kn = (n[:, None] * n[None, :]) % (2 * half)
    ang = kn.astype(F32) * (math.pi / half)
    return jnp.cos(ang).astype(BF16), (-jnp.sin(ang)).astype(BF16)


def _hy_positions(seq):
    t = jnp.linspace(0.0, 1.0, seq, dtype=F32)[:, None]
    tr = jnp.arange(seq, dtype=F32)[:, None]
    wpos = 2.0 * math.pi * tr / seq
    fb = jnp.linspace(1e-4, HY_BANDS - 1, HY_BANDS, dtype=F32)[None, :]
    z = jnp.concatenate([t, jnp.cos(fb * wpos), -jnp.sin(fb * wpos)], axis=-1)
    return jnp.pad(z, ((0, 0), (0, LANES - HY_EMB)))


def _hy_filter_body(z_ref, w1_ref, b1_ref, w2_ref, b2_ref, w3_ref, b3_ref, fr_ref,
                    wf_ref, wb_ref, dl_ref, fre_ref, fim_ref,
                    g_ref, gn_ref, hdn_ref):
    hi = lax.Precision.HIGHEST
    seq = z_ref.shape[0]
    width = wf_ref.shape[-1]

    @pl.when((pl.program_id(0) == 0) & (pl.program_id(1) == 0))
    def _():
        fr = fr_ref[...]
        hdn = jnp.sin(fr * (jnp.dot(z_ref[...], w1_ref[...], precision=hi,
                                    preferred_element_type=F32) + b1_ref[...]))
        hdn = jnp.sin(fr * (jnp.dot(hdn, w2_ref[...], precision=hi,
                                    preferred_element_type=F32) + b2_ref[...]))
        hdn_ref[...] = jnp.sin(fr * (jnp.dot(hdn, w3_ref[...], precision=hi,
                                             preferred_element_type=F32) + b3_ref[...]))

    hdn = hdn_ref[...]
    row = lax.broadcasted_iota(jnp.int32, (seq, width), 0)
    t = row.astype(F32) * (1.0 / (seq - 1))
    decay = jnp.exp(-t * dl_ref[...]) + HY_MOD_SHIFT
    kf = jnp.dot(hdn, wf_ref[0], precision=hi, preferred_element_type=F32) * decay
    kb = jnp.dot(hdn, wb_ref[0], precision=hi, preferred_element_type=F32) * decay
    kb = jnp.where(row == 0, 0.0, kb)
    norm = (jnp.sum(jnp.abs(kf), axis=0, keepdims=True)
            + jnp.sum(jnp.abs(kb), axis=0, keepdims=True) + HY_NORM_EPS)
    kf = kf / norm
    kb = kb / norm

    nseg = seq // HY_P
    krow = lax.broadcasted_iota(jnp.int32, (HY_P, width), 0)
    sk = (1 - 2 * (krow & 1)).astype(F32)
    ck = jnp.where(krow == 0, 0.5 / HY_P, sk * (1.0 / HY_P))
    zero_row = jnp.zeros((1, width), F32)

    def dft(x):
        xb = x.astype(BF16)
        return (jnp.dot(fre_ref[...], xb, preferred_element_type=F32),
                jnp.dot(fim_ref[...], xb, preferred_element_type=F32),
                jnp.sum(x * sk, axis=0, keepdims=True))

    spectra = {}
    for e in range(nseg):
        spectra[e] = dft(kf[e * HY_P:(e + 1) * HY_P])
        lead = kb[e * HY_P:(e + 1) * HY_P]
        bre, bim, bny = dft(jnp.where(krow == 0, 0.0, lead))
        first = kb[(e + 1) * HY_P:(e + 1) * HY_P + 1] if e + 1 < nseg else zero_row
        spectra[-(e + 1)] = (first + sk * bre, -(sk * bim), first + bny)

    for delta in range(-(nseg - 1), nseg):
        are, aim, any_ = spectra[delta - 1]
        bre, bim, bny = spectra[delta]
        gre = (are + sk * bre) * ck
        gim = (aim + sk * bim) * ck
        g_ref[0, delta + nseg - 1, 0] = gre
        g_ref[0, delta + nseg - 1, 1] = gim - gre
        g_ref[0, delta + nseg - 1, 2] = gre + gim
        gn_ref[0, delta + nseg - 1:delta + nseg, :] = (any_ + bny) * (0.5 / HY_P)
    gn_ref[0, 2 * nseg - 1:2 * nseg, :] = zero_row


def _hy_filter(z, w1, b1, w2, b2, w3, b3, freq, wout, deltas, fre, fim):
    seq = z.shape[0]
    wf = wout[:, :HY_ORDER * HY_WIDTH].reshape(HY_HIDDEN, HY_ORDER, HY_WIDTH).transpose(1, 0, 2)
    wb = wout[:, HY_ORDER * HY_WIDTH:].reshape(HY_HIDDEN, HY_ORDER, HY_WIDTH).transpose(1, 0, 2)
    w1p = jnp.pad(w1, ((0, LANES - HY_EMB), (0, 0)))
    row = lambda a: a.reshape(1, -1)
    nseg = seq // HY_P
    return pl.pallas_call(
        _hy_filter_body,
        out_shape=(
            jax.ShapeDtypeStruct((HY_ORDER, 2 * nseg - 1, 3, HY_P, HY_WIDTH), F32),
            jax.ShapeDtypeStruct((HY_ORDER, 2 * nseg, HY_WIDTH), F32),
        ),
        grid=(HY_ORDER, HY_WIDTH // HY_CBLK),
        in_specs=[
            _const_spec((seq, LANES)),
            _const_spec((LANES, HY_HIDDEN)), _const_spec((1, HY_HIDDEN)),
            _const_spec((HY_HIDDEN, HY_HIDDEN)), _const_spec((1, HY_HIDDEN)),
            _const_spec((HY_HIDDEN, HY_HIDDEN)), _const_spec((1, HY_HIDDEN)),
            _const_spec((1, HY_HIDDEN)),
            pl.BlockSpec((1, HY_HIDDEN, HY_CBLK), lambda o, c: (o, 0, c)),
            pl.BlockSpec((1, HY_HIDDEN, HY_CBLK), lambda o, c: (o, 0, c)),
            pl.BlockSpec((1, HY_CBLK), lambda o, c: (0, c)),
            _const_spec((HY_P, HY_P)), _const_spec((HY_P, HY_P)),
        ],
        out_specs=(
            pl.BlockSpec((1, 2 * nseg - 1, 3, HY_P, HY_CBLK), lambda o, c: (o, 0, 0, 0, c)),
            pl.BlockSpec((1, 2 * nseg, HY_CBLK), lambda o, c: (o, 0, c)),
        ),
        scratch_shapes=[pltpu.VMEM((seq, HY_HIDDEN), F32)],
        compiler_params=pltpu.CompilerParams(
            dimension_semantics=("arbitrary", "arbitrary"), vmem_limit_bytes=VMEM_LIMIT_BYTES),
        name="hy_filter",
    )(z, w1p, row(b1), w2, row(b2), w3, row(b3), row(freq), wf, wb, row(deltas), fre, fim)


def _hy_conv_body(v_ref, x1_ref, x2_ref, wv_ref, w1_ref, w2_ref, bv_ref, b1_ref, b2_ref,
                  fre_ref, fim_ref, g_ref, gn_ref, skip_ref, o_ref,
                  z_ref, g1_ref, g2_ref, ure_ref, uim_ref, usum_ref):
    seq = v_ref.shape[1]
    nseg = seq // HY_P
    sgn = (1 - 2 * (lax.broadcasted_iota(jnp.int32, (HY_P, HY_CBLK), 0) & 1)).astype(F32)

    edge = lax.broadcasted_iota(jnp.int32, (SUBLANES, HY_CBLK), 0)

    def short_conv(x_ref, w_ref, b_ref, dst_ref):
        x = x_ref[0].astype(F32)
        w = w_ref[...]
        dst_ref[...] = (w[0:1] * pltpu.roll(x, 1, 0) + w[1:2] * x
                        + w[2:3] * pltpu.roll(x, seq - 1, 0) + b_ref[...])
        dst_ref[0:SUBLANES, :] = dst_ref[0:SUBLANES, :] - jnp.where(
            edge == 0, w[0:1] * x[seq - 1:seq], 0.0)
        dst_ref[seq - SUBLANES:seq, :] = dst_ref[seq - SUBLANES:seq, :] - jnp.where(
            edge == SUBLANES - 1, w[2:3] * x[0:1], 0.0)

    short_conv(v_ref, wv_ref, bv_ref, z_ref)
    short_conv(x1_ref, w1_ref, b1_ref, g1_ref)
    short_conv(x2_ref, w2_ref, b2_ref, g2_ref)
    skip = skip_ref[...]
    blk = lambda i: slice(i * HY_P, (i + 1) * HY_P)

    for o, gate_ref in enumerate((g1_ref, g2_ref)):
        nyq = []
        for j in range(nseg):
            x = z_ref[blk(j), :]
            xb = x.astype(BF16)
            ure = jnp.dot(fre_ref[...], xb, preferred_element_type=F32)
            uim = jnp.dot(fim_ref[...], xb, preferred_element_type=F32)
            ure_ref[j] = ure
            uim_ref[j] = uim
            usum_ref[j] = ure + uim
            nyq.append(jnp.sum(x * sgn, axis=0, keepdims=True))
        for i in range(nseg):
            k1 = k2 = k3 = zny = None
            for j in range(nseg):
                d = i - j + nseg - 1
                t1 = g_ref[o, d, 0] * usum_ref[j]
                t2 = g_ref[o, d, 1] * ure_ref[j]
                t3 = g_ref[o, d, 2] * uim_ref[j]
                tny = gn_ref[o, d:d + 1, :] * nyq[j]
                k1, k2, k3, zny = (t1, t2, t3, tny) if k1 is None else (k1 + t1, k2 + t2, k3 + t3, zny + tny)
            y = (jnp.dot(fre_ref[...], (k1 - k3).astype(BF16), preferred_element_type=F32)
                 + jnp.dot(fim_ref[...], (k1 + k2).astype(BF16), preferred_element_type=F32)
                 + sgn * zny)
            z = gate_ref[blk(i), :] * (y + skip[o:o + 1] * z_ref[blk(i), :])
            if o + 1 < HY_ORDER:
                z_ref[blk(i), :] = z
            else:
                o_ref[0, blk(i), :] = z.astype(BF16)


def _hy_conv(u_hy, conv_w, conv_b, fre, fim, g, gn, skip):
    nb, seq, _ = u_hy.shape
    ncb = HY_WIDTH // HY_CBLK
    nseg = seq // HY_P
    data = lambda part: pl.BlockSpec((1, seq, HY_CBLK), lambda c, b: (b, 0, part * ncb + c))
    wspec = lambda part: pl.BlockSpec((3, HY_CBLK), lambda c, b: (0, part * ncb + c))
    bspec = lambda part: pl.BlockSpec((1, HY_CBLK), lambda c, b: (0, part * ncb + c))
    return pl.pallas_call(
        _hy_conv_body,
        out_shape=jax.ShapeDtypeStruct((nb, seq, HY_WIDTH), BF16),
        grid=(ncb, nb),
        in_specs=[
            data(0), data(1), data(2),
            wspec(0), wspec(1), wspec(2),
            bspec(0), bspec(1), bspec(2),
            _const_spec((HY_P, HY_P)), _const_spec((HY_P, HY_P)),
            pl.BlockSpec((HY_ORDER, 2 * nseg - 1, 3, HY_P, HY_CBLK), lambda c, b: (0, 0, 0, 0, c),
                         pipeline_mode=pl.Buffered(1)),
            pl.BlockSpec((HY_ORDER, 2 * nseg, HY_CBLK), lambda c, b: (0, 0, c)),
            pl.BlockSpec((HY_ORDER, HY_CBLK), lambda c, b: (0, c)),
        ],
        out_specs=pl.BlockSpec((1, seq, HY_CBLK), lambda c, b: (b, 0, c)),
        scratch_shapes=[
            pltpu.VMEM((seq, HY_CBLK), F32), pltpu.VMEM((seq, HY_CBLK), F32),
            pltpu.VMEM((seq, HY_CBLK), F32),
            pltpu.VMEM((nseg, HY_P, HY_CBLK), F32), pltpu.VMEM((nseg, HY_P, HY_CBLK), F32),
            pltpu.VMEM((nseg, HY_P, HY_CBLK), F32),
        ],
        compiler_params=pltpu.CompilerParams(
            dimension_semantics=("arbitrary", "arbitrary"), vmem_limit_bytes=VMEM_LIMIT_BYTES),
        name="hy_conv",
    )(u_hy, u_hy, u_hy, conv_w, conv_w, conv_w, conv_b, conv_b, conv_b,
      fre, fim, g, gn, skip)


def _bucket_matrix(dilation):
    qi = np.arange(Q_TILE, dtype=np.int64)[:, None]
    kj = np.arange(K_WIN, dtype=np.int64)[None, :]
    delta = kj - BAND - qi
    rel = delta * dilation
    half = N_BUCKETS // 2
    n = np.abs(rel)
    nf = np.maximum(n, 1).astype(np.float32)
    large = BUCKET_MAX_EXACT + (
        np.log(nf / np.float32(BUCKET_MAX_EXACT)) / np.float32(math.log(BUCKET_MAX_DIST / BUCKET_MAX_EXACT))
        * np.float32(half - BUCKET_MAX_EXACT)).astype(np.int32)
    large = np.minimum(large, half - 1)
    bucket = np.where(rel > 0, half, 0) + np.where(n < BUCKET_MAX_EXACT, n, large)
    return np.where(np.abs(delta) <= BAND, bucket, -1).astype(np.int32)


def _attn_body(group, tbl_ref, bkt_ref, q_ref, k_ref, v_ref, o_ref, l_ref,
               bias_ref, kpad_ref, vpad_ref):
    _, dil, ls, _ = q_ref.shape
    tiles_per_res = ls // Q_TILE

    kcol = lax.broadcasted_iota(jnp.int32, (Q_TILE, K_WIN), 1)

    @pl.when(pl.program_id(0) == 0)
    def _():
        bkt = bkt_ref[...]
        for h in range(HEADS):
            acc = jnp.full((Q_TILE, K_WIN), NEG_INF, F32)
            for bk in range(N_BUCKETS):
                acc = jnp.where(bkt == bk, tbl_ref[bk, group * HEADS + h] * LOG2E, acc)
            no_head = jnp.where(kcol < BAND, NEG_INF, acc)
            bias_ref[0, h] = acc
            bias_ref[1, h] = no_head
            bias_ref[2, h] = jnp.where(kcol >= Q_TILE + BAND, NEG_INF, acc)
            bias_ref[3, h] = jnp.where(kcol >= Q_TILE + BAND, NEG_INF, no_head)
        zeros = jnp.zeros((dil, BAND, ATTN_WIDTH), BF16)
        kpad_ref[:, 0:BAND] = zeros
        kpad_ref[:, BAND + ls:BAND + ls + BAND] = zeros
        vpad_ref[:, 0:BAND] = zeros
        vpad_ref[:, BAND + ls:BAND + ls + BAND] = zeros

    kpad_ref[:, BAND:BAND + ls] = k_ref[0]
    vpad_ref[:, BAND:BAND + ls] = v_ref[0]

    lane = lax.broadcasted_iota(jnp.int32, (Q_TILE, LANES), 1)
    low_half = lane < HEAD_DIM

    def tile(t, carry):
        r = t // tiles_per_res
        jt = t % tiles_per_res
        j0 = pl.multiple_of(jt * Q_TILE, Q_TILE)
        edge = jnp.where(jt == 0, 1, 0) + jnp.where(jt == tiles_per_res - 1, 2, 0)
        lse_tile = jnp.zeros((Q_TILE, LANES), F32)
        for hp in range(HEADS // 2):
            cols = slice(hp * LANES, (hp + 1) * LANES)
            q2 = q_ref[0, r, pl.ds(j0, Q_TILE), cols]
            kw = kpad_ref[r, pl.ds(j0, K_WIN), cols]
            vw = vpad_ref[r, pl.ds(j0, K_WIN), cols]
            q_pair = jnp.concatenate(
                [jnp.where(low_half, q2, 0.0), jnp.where(low_half, 0.0, q2)], axis=0).astype(BF16)
            s_pair = lax.dot_general(q_pair, kw, (((1,), (1,)), ((), ())), preferred_element_type=F32)
            probs = []
            for half in range(2):
                h = 2 * hp + half
                s = s_pair[half * Q_TILE:(half + 1) * Q_TILE] + bias_ref[edge, h]
                m = jnp.max(s, axis=-1, keepdims=True)
                p = jnp.exp2(s - m)
                den = jnp.sum(p, axis=-1, keepdims=True)
                probs.append(p.astype(BF16))
                lse_tile = jnp.where(lane == h, m, jnp.where(lane == HEADS + h, den, lse_tile))
            pv = jnp.dot(jnp.concatenate(probs, axis=0), vw, preferred_element_type=F32)
            o_ref[0, r, pl.ds(j0, Q_TILE), cols] = jnp.where(
                low_half, pv[:Q_TILE], pv[Q_TILE:]).astype(BF16)
        l_ref[0, r, pl.ds(j0, Q_TILE), :] = lse_tile
        return carry

    lax.fori_loop(0, dil * tiles_per_res, tile, 0, unroll=ATTN_UNROLL)


def _attn_group(qkv_g, tbl, group, dilation):
    nb, _, ls, _ = qkv_g.shape
    part = lambda which: pl.BlockSpec((1, dilation, ls, ATTN_WIDTH), lambda b: (b, 0, 0, which))
    bkt = jnp.asarray(_bucket_matrix(dilation))
    return pl.pallas_call(
        functools.partial(_attn_body, group),
        out_shape=(
            jax.ShapeDtypeStruct((nb, dilation, ls, ATTN_WIDTH), BF16),
            jax.ShapeDtypeStruct((nb, dilation, ls, LANES), F32),
        ),
        grid=(nb,),
        in_specs=[
            pl.BlockSpec(memory_space=pltpu.SMEM),
            _const_spec((Q_TILE, K_WIN)),
            part(0), part(1), part(2),
        ],
        out_specs=(
            pl.BlockSpec((1, dilation, ls, ATTN_WIDTH), lambda b: (b, 0, 0, 0)),
            pl.BlockSpec((1, dilation, ls, LANES), lambda b: (b, 0, 0, 0)),
        ),
        scratch_shapes=[
            pltpu.VMEM((N_EDGE_CASES, HEADS, Q_TILE, K_WIN), F32),
            pltpu.VMEM((dilation, ls + 2 * BAND, ATTN_WIDTH), BF16),
            pltpu.VMEM((dilation, ls + 2 * BAND, ATTN_WIDTH), BF16),
        ],
        compiler_params=pltpu.CompilerParams(
            dimension_semantics=("arbitrary",), vmem_limit_bytes=VMEM_LIMIT_BYTES),
        name=f"attn_g{group}",
    )(tbl, bkt, qkv_g, qkv_g, qkv_g)


def _rglru_body(nt, cw_ref, cb_ref, wf_ref, wb_ref, bf_ref, bb_ref, lam_ref,
                xf_ref, pf_ref, nf_ref, xb_ref, pb_ref, nb_ref,
                hf_ref, hb_ref, xw_ref, a_ref, u_ref, hs_ref, cf_ref, cbk_ref):
    i = pl.program_id(0)
    nbat, tt, width = xf_ref.shape
    n_slab = width // LANES
    n_grp = nbat // SUBLANES
    grp_rows = tt * SUBLANES
    halo_rows = HALO_T * SUBLANES

    @pl.when(i == 0)
    def _():
        cf_ref[...] = jnp.zeros_like(cf_ref)
        cbk_ref[...] = jnp.zeros_like(cbk_ref)

    def to_time_major(src_ref, n_t, row0, keep):
        for b in range(nbat):
            grp, b8 = divmod(b, SUBLANES)
            x = src_ref[b].astype(F32)
            x = x if keep is None else jnp.where(keep, x, 0.0)
            for s, slab in enumerate(_lane_slabs(x)):
                xw_ref.at[s * n_grp + grp][pl.ds(row0 + b8, n_t, stride=SUBLANES), :] = slab

    def gates(x_ref, p_ref, n_ref, tile_idx, w_ref, b_ref, lam):
        to_time_major(p_ref, HALO_T, 0, tile_idx > 0)
        to_time_major(x_ref, tt, halo_rows, None)
        to_time_major(n_ref, HALO_T, halo_rows + grp_rows, tile_idx < nt - 1)
        cw = cw_ref[...]

        def tap(k):
            start = halo_rows + (k - 2) * SUBLANES
            return jnp.concatenate(
                [jnp.concatenate([xw_ref[s * n_grp + grp, start:start + grp_rows, :]
                                  for s in range(n_slab)], axis=1) for grp in range(n_grp)], axis=0)

        xc = cw[0:1] * tap(0) + cw[1:2] * tap(1) + cw[2:3] * tap(2) + cw[3:4] * tap(3) + cb_ref[...]
        g = jnp.dot(xc.astype(BF16), w_ref[...], preferred_element_type=F32) + b_ref[...]
        r2 = 1.0 + jnp.tanh(g[:, :width])
        gi2 = 1.0 + jnp.tanh(g[:, width:])
        softplus = jnp.maximum(-lam, 0.0) + jnp.log(1.0 + jnp.exp(-jnp.abs(lam)))
        log_a = (-0.5 * RG_C * softplus) * r2
        a = jnp.exp(log_a)
        a_ref[...] = a
        one_m_a2 = 1.0 - a * a
        root = jnp.where(one_m_a2 > 0.0, one_m_a2 * lax.rsqrt(one_m_a2), 0.0)
        u_ref[...] = root * (gi2 * xc)

    def step(t, hs):
        r0 = pl.multiple_of(t * SUBLANES, SUBLANES)
        out = []
        for grp, h in enumerate(hs):
            rows = pl.ds(grp * grp_rows + r0, SUBLANES)
            h = a_ref[rows, :] * h + u_ref[rows, :]
            for s, slab in enumerate(_lane_slabs(h)):
                hs_ref[s * n_grp + grp, pl.ds(r0, SUBLANES), :] = slab
            out.append(h)
        return tuple(out)

    def scan(carry_ref, time_of):
        init = tuple(carry_ref[grp * SUBLANES:(grp + 1) * SUBLANES, :] for grp in range(n_grp))
        last = lax.fori_loop(0, tt, lambda k, hs: step(time_of(k), hs), init, unroll=RG_SCAN_UNROLL)
        carry_ref[...] = jnp.concatenate(last, axis=0)

    def to_batch_major(dst_ref):
        for b in range(nbat):
            grp, b8 = divmod(b, SUBLANES)
            dst_ref[b] = jnp.concatenate(
                [hs_ref.at[s * n_grp + grp][pl.ds(b8, tt, stride=SUBLANES), :] for s in range(n_slab)],
                axis=1).astype(BF16)

    gates(xf_ref, pf_ref, nf_ref, i, wf_ref, bf_ref, lam_ref[0:1])
    scan(cf_ref, lambda k: k)
    to_batch_major(hf_ref)

    gates(xb_ref, pb_ref, nb_ref, nt - 1 - i, wb_ref, bb_ref, lam_ref[1:2])
    scan(cbk_ref, lambda k: tt - 1 - k)
    to_batch_major(hb_ref)


def _rglru(rg, conv_w, conv_b, w_f, w_b, b_f, b_b, lam):
    nbat, seq, _ = rg.shape
    nt = seq // RG_TT
    hpt = RG_TT // HALO_T
    nh = seq // HALO_T
    cur = lambda f: pl.BlockSpec((nbat, RG_TT, RG_WIDTH), lambda i: (0, f(i), 0))
    prv = lambda f: pl.BlockSpec(
        (nbat, HALO_T, RG_WIDTH), lambda i: (0, jnp.maximum(f(i) * hpt - 1, 0), 0))
    nxt = lambda f: pl.BlockSpec(
        (nbat, HALO_T, RG_WIDTH), lambda i: (0, jnp.minimum((f(i) + 1) * hpt, nh - 1), 0))
    fw = lambda i: i
    bw = lambda i: nt - 1 - i
    n_slab = RG_WIDTH // LANES
    n_grp = nbat // SUBLANES
    return pl.pallas_call(
        functools.partial(_rglru_body, nt),
        out_shape=(
            jax.ShapeDtypeStruct((nbat, seq, RG_WIDTH), BF16),
            jax.ShapeDtypeStruct((nbat, seq, RG_WIDTH), BF16),
        ),
        grid=(nt,),
        in_specs=[
            _const_spec((RG_CONV, RG_WIDTH)), _const_spec((1, RG_WIDTH)),
            _const_spec((RG_WIDTH, 2 * RG_WIDTH)), _const_spec((RG_WIDTH, 2 * RG_WIDTH)),
            _const_spec((1, 2 * RG_WIDTH)), _const_spec((1, 2 * RG_WIDTH)),
            _const_spec((2, RG_WIDTH)),
            cur(fw), prv(fw), nxt(fw), cur(bw), prv(bw), nxt(bw),
        ],
        out_specs=(cur(fw), cur(bw)),
        scratch_shapes=[
            pltpu.VMEM((n_slab * n_grp, (RG_TT + 2 * HALO_T) * SUBLANES, LANES), F32),
            pltpu.VMEM((RG_TT * nbat, RG_WIDTH), F32),
            pltpu.VMEM((RG_TT * nbat, RG_WIDTH), F32),
            pltpu.VMEM((n_slab * n_grp, RG_TT * SUBLANES, LANES), F32),
            pltpu.VMEM((nbat, RG_WIDTH), F32),
            pltpu.VMEM((nbat, RG_WIDTH), F32),
        ],
        compiler_params=pltpu.CompilerParams(
            dimension_semantics=("arbitrary",), vmem_limit_bytes=VMEM_LIMIT_BYTES),
        name="rglru",
    )(conv_w, conv_b, w_f, w_b, b_f, b_b, lam, rg, rg, rg, rg, rg, rg)


def _block_diag(w):
    eye = jnp.eye(RG_BLOCKS, dtype=w.dtype)
    return jnp.einsum("hij,hk->hikj", w, eye).reshape(RG_WIDTH, RG_WIDTH)


def _merge_body(h_ref, g_ref, wg_ref, bg_ref, ya_ref, o0_ref, o1_ref, o2_ref,
                l0_ref, l1_ref, l2_ref, hf_ref, hb_ref, gate_ref, exp_ref,
                wphy_ref, wpat_ref, wprg_ref, wout_ref, out_ref, os_ref, ls_ref):
    x = h_ref[0]
    xb = _rms(x, g_ref[...]).astype(BF16)
    gates = 1.0 + jnp.tanh(jnp.dot(xb, wg_ref[0], preferred_element_type=F32) + bg_ref[...])

    def token_order(o_ref, l_ref):
        d, n = o_ref.shape[1], o_ref.shape[2]
        if d == 1:
            return o_ref[0, 0].astype(F32), l_ref[0, 0]
        for r in range(d):
            for s, slab in enumerate(_lane_slabs(o_ref[0, r].astype(F32))):
                os_ref.at[s][pl.ds(r, n, stride=d), :] = slab
            ls_ref[pl.ds(r, n, stride=d), :] = l_ref[0, r]
        o = jnp.concatenate([os_ref[s] for s in range(ATTN_WIDTH // LANES)], axis=1)
        return o, ls_ref[...]

    o0, l0 = token_order(o0_ref, l0_ref)
    o1, l1 = token_order(o1_ref, l1_ref)
    o2, l2 = token_order(o2_ref, l2_ref)

    head_lane = lax.broadcasted_iota(jnp.int32, l0.shape, 1) < HEADS
    d0, d1, d2 = (pltpu.roll(l, LANES - HEADS, 1) for l in (l0, l1, l2))
    m = jnp.maximum(jnp.maximum(l0, l1), l2)
    e0, e1, e2 = jnp.exp2(l0 - m), jnp.exp2(l1 - m), jnp.exp2(l2 - m)
    inv = 1.0 / jnp.where(head_lane, e0 * d0 + e1 * d1 + e2 * d2, 1.0)

    def widen(w):
        return jnp.dot(w.astype(BF16), exp_ref[...], preferred_element_type=F32)

    y_b = widen(e0 * inv) * o0 + widen(e1 * inv) * o1 + widen(e2 * inv) * o2

    gt = gate_ref[0].astype(F32)
    gelu = 0.5 * gt * (1.0 + jnp.tanh(math.sqrt(2.0 / math.pi) * (gt + 0.044715 * (gt * gt * gt))))
    y_c = (hf_ref[0].astype(F32) + hb_ref[0].astype(F32)) * gelu

    merged = (gates[:, :D_MODEL]
              * jnp.dot(ya_ref[0], wphy_ref[0], preferred_element_type=F32)
              + gates[:, D_MODEL:2 * D_MODEL]
              * jnp.dot(y_b.astype(BF16), wpat_ref[0], preferred_element_type=F32)
              + gates[:, 2 * D_MODEL:]
              * jnp.dot(y_c.astype(BF16), wprg_ref[0], preferred_element_type=F32))
    out_ref[0] = x + jnp.dot(merged.astype(BF16), wout_ref[0], preferred_element_type=F32)


def _merge(h3, g, w_gate, b_gate, y_a, attn, hf, hb, rg, expand, wp_hy, wp_attn, wp_rg, w_out, l):
    nb, seq, _ = h3.shape
    tok = lambda width, col=0: pl.BlockSpec((1, TOKEN_TILE, width), lambda b, i: (b, i, col))
    res = lambda d, width: pl.BlockSpec((1, d, TOKEN_TILE // d, width), lambda b, i: (b, 0, i, 0))
    (o0, l0), (o1, l1), (o2, l2) = attn
    dils = [d for _, d in ATTN_GROUPS]
    return pl.pallas_call(
        _merge_body,
        out_shape=jax.ShapeDtypeStruct(h3.shape, F32),
        grid=(nb, seq // TOKEN_TILE),
        in_specs=[
            tok(D_MODEL), _const_spec((1, D_MODEL)),
            _layer_spec(l, D_MODEL, N_BRANCH * D_MODEL), _const_spec((1, N_BRANCH * D_MODEL)),
            tok(HY_WIDTH),
            res(dils[0], ATTN_WIDTH), res(dils[1], ATTN_WIDTH), res(dils[2], ATTN_WIDTH),
            res(dils[0], LANES), res(dils[1], LANES), res(dils[2], LANES),
            tok(RG_WIDTH), tok(RG_WIDTH), tok(RG_WIDTH, 1),
            _const_spec((LANES, ATTN_WIDTH)),
            _layer_spec(l, HY_WIDTH, D_MODEL), _layer_spec(l, ATTN_WIDTH, D_MODEL),
            _layer_spec(l, RG_WIDTH, D_MODEL), _layer_spec(l, D_MODEL, D_MODEL),
        ],
        out_specs=tok(D_MODEL),
        scratch_shapes=[
            pltpu.VMEM((ATTN_WIDTH // LANES, TOKEN_TILE, LANES), F32),
            pltpu.VMEM((TOKEN_TILE, LANES), F32),
        ],
        compiler_params=pltpu.CompilerParams(
            dimension_semantics=("arbitrary", "arbitrary"), vmem_limit_bytes=VMEM_LIMIT_BYTES),
        name="merge",
    )(h3, g, w_gate, b_gate, y_a, o0, o1, o2, l0, l1, l2, hf, hb, rg, expand,
      wp_hy, wp_attn, wp_rg, w_out)


def _mixer(h3, l, p, wb, consts):
    fre, fim, zpos, deltas, expand = consts
    hy, q0, q1, q2, rg = _inproj(h3, p["mix_norm"][l].reshape(1, -1), wb["w_in"], l)

    g_spec, g_nyq = _hy_filter(
        zpos, p["hy_w1"][l], p["hy_b1"][l], p["hy_w2"][l], p["hy_b2"][l], p["hy_w3"][l],
        p["hy_b3"][l], p["hy_freq"][l], p["hy_wout"][l], deltas, fre, fim)
    y_a = _hy_conv(hy, p["hy_conv_w"][l], p["hy_conv_b"][l].reshape(1, -1),
                   fre, fim, g_spec, g_nyq, p["hy_skip"][l])

    attn = [_attn_group(q, p["rel_bias"], g, dil)
            for g, (q, (_, dil)) in enumerate(zip((q0, q1, q2), ATTN_GROUPS))]

    rg_w = lambda d: (0.5 * jnp.concatenate(
        [_block_diag(p["rg_wa"][l, d]), _block_diag(p["rg_wx"][l, d])], axis=1)).astype(BF16)
    rg_b = lambda d: 0.5 * jnp.concatenate([p["rg_ba"][l, d], p["rg_bx"][l, d]]).reshape(1, -1)
    hf, hb = _rglru(rg, p["rg_conv_w"][l], p["rg_conv_b"][l].reshape(1, -1),
                    rg_w(0), rg_w(1), rg_b(0), rg_b(1), p["rg_lambda"][l])

    return _merge(
        h3, p["mix_norm"][l].reshape(1, -1), wb["w_gate"], 0.5 * p["b_gate"][l].reshape(1, -1),
        y_a, attn, hf, hb, rg, expand,
        wb["w_proj_hy"], wb["w_proj_attn"], wb["w_proj_rg"], wb["w_out"], l)


def _forward(x, p):
    nb, seq, _ = x.shape
    fre, fim = _dft_mats(HY_P)
    deltas = jnp.abs(jnp.linspace(math.log(HY_DECAY_TARGET) / HY_FAST_DECAY,
                                  math.log(HY_DECAY_TARGET) / HY_SLOW_DECAY, HY_WIDTH, dtype=F32))
    head_of_lane = jnp.arange(ATTN_WIDTH, dtype=jnp.int32) // HEAD_DIM
    expand = (jnp.arange(LANES, dtype=jnp.int32)[:, None] == head_of_lane[None, :]).astype(BF16)
    consts = (fre, fim, _hy_positions(seq), deltas, expand)

    halved = ("w_gate", "w_out", "w_proj_rg")
    wb = {k: (0.5 * p[k] if k in halved else p[k]).astype(BF16) for k in (
        "ffn1_wg", "ffn1_wu", "ffn1_wd", "ffn2_wg", "ffn2_wu", "ffn2_wd",
        "w_in", "w_gate", "w_proj_hy", "w_proj_attn", "w_proj_rg", "w_out")}

    def ffn(h3, which, l, g_out=None):
        out = _ffn(h3.reshape(nb * seq, D_MODEL), p[which + "_norm"][l].reshape(1, -1),
                   wb[which + "_wg"], wb[which + "_wu"], wb[which + "_wd"], l, g_out)
        return out.reshape(nb, seq, D_MODEL)

    h = x
    for l in range(DEPTH):
        h = ffn(h, "ffn1", l)
        h = _mixer(h, l, p, wb, consts)
        h = ffn(h, "ffn2", l, p["final_norm"].reshape(1, -1) if l == DEPTH - 1 else None)
    return h


def kernel(x, ffn1_norm, ffn1_wg, ffn1_wu, ffn1_wd, mix_norm, w_in, hy_conv_w, hy_conv_b, hy_w1, hy_b1, hy_w2, hy_b2, hy_w3, hy_b3, hy_freq, hy_wout, hy_skip, rel_bias, rg_conv_w, rg_conv_b, rg_wa, rg_ba, rg_wx, rg_bx, rg_lambda, w_gate, b_gate, w_proj_hy, w_proj_attn, w_proj_rg, w_out, ffn2_norm, ffn2_wg, ffn2_wu, ffn2_wd, final_norm):
    p = dict(
        ffn1_norm=ffn1_norm, ffn1_wg=ffn1_wg, ffn1_wu=ffn1_wu, ffn1_wd=ffn1_wd, mix_norm=mix_norm,
        w_in=w_in, hy_conv_w=hy_conv_w, hy_conv_b=hy_conv_b, hy_w1=hy_w1, hy_b1=hy_b1, hy_w2=hy_w2,
        hy_b2=hy_b2, hy_w3=hy_w3, hy_b3=hy_b3, hy_freq=hy_freq, hy_wout=hy_wout, hy_skip=hy_skip,
        rel_bias=rel_bias, rg_conv_w=rg_conv_w, rg_conv_b=rg_conv_b, rg_wa=rg_wa, rg_ba=rg_ba,
        rg_wx=rg_wx, rg_bx=rg_bx, rg_lambda=rg_lambda, w_gate=w_gate, b_gate=b_gate,
        w_proj_hy=w_proj_hy, w_proj_attn=w_proj_attn, w_proj_rg=w_proj_rg, w_out=w_out,
        ffn2_norm=ffn2_norm, ffn2_wg=ffn2_wg, ffn2_wu=ffn2_wu, ffn2_wd=ffn2_wd,
        final_norm=final_norm)
    return _forward(x, p)
```

```python
import functools
import math

import numpy as np
import jax
import jax.numpy as jnp
from jax import lax
from jax.experimental import pallas as pl
from jax.experimental.pallas import tpu as pltpu

F32 = jnp.float32
BF16 = jnp.bfloat16

D_MODEL = 1024
D_FF = 2816
DEPTH = 4
RMS_EPS = 1e-6

HY_WIDTH = 512
HY_ORDER = 2
HY_EMB = 33
HY_BANDS = 16
HY_HIDDEN = 64
HY_COLS = 3 * HY_WIDTH
HY_DECAY_TARGET = 1e-2
HY_FAST_DECAY = 0.3
HY_SLOW_DECAY = 1.5
HY_MOD_SHIFT = 0.05
HY_NORM_EPS = 1e-6

ATTN_GROUPS = ((128, 1), (512, 4), (2048, 16))
N_GROUPS = 3
HEADS = 8
HEAD_DIM = 64
ATTN_WIDTH = HEADS * HEAD_DIM
QKV_COLS = 3 * N_GROUPS * ATTN_WIDTH
BAND = 64
N_BUCKETS = 32
BUCKET_MAX_EXACT = 8
BUCKET_MAX_DIST = 1024
NEG_INF = -1e30
LOG2E = math.log2(math.e)
Q_SCALE = HEAD_DIM ** -0.5 * LOG2E

RG_WIDTH = 512
RG_BLOCKS = 8
RG_CONV = 4
RG_C = 8.0
RG_COLS = 2 * RG_WIDTH

N_BRANCH = 3

LANES = 128
SUBLANES = 8
VMEM_LIMIT_BYTES = 56 * 1024 * 1024

TOKEN_TILE = 512
Q_TILE = 128
K_WIN = Q_TILE + 2 * BAND
ATTN_UNROLL = 16
N_EDGE_CASES = 4
RG_SCAN_UNROLL = 16
HY_CBLK = 256
HY_P = 512
RG_TT = 64
HALO_T = 8


def _const_spec(shape):
    nd = len(shape)
    return pl.BlockSpec(shape, lambda *_: (0,) * nd, pipeline_mode=pl.Buffered(1))


def _layer_spec(l, rows, cols, col_block=0):
    return pl.BlockSpec((1, rows, cols), lambda *_: (l, 0, col_block), pipeline_mode=pl.Buffered(1))


def _rms(x, g):
    ms = jnp.mean(x * x, axis=-1, keepdims=True)
    return x * lax.rsqrt(ms + RMS_EPS) * g


def _sigmoid(x):
    return 0.5 * jnp.tanh(0.5 * x) + 0.5


def _lane_slabs(x):
    return [x[:, s * LANES:(s + 1) * LANES] for s in range(x.shape[1] // LANES)]


def _ffn_body(h_ref, g_ref, wg_ref, wu_ref, wd_ref, *rest):
    gout_ref, o_ref = rest if len(rest) == 2 else (None, rest[0])
    x = h_ref[...]
    xb = _rms(x, g_ref[...]).astype(BF16)
    gate = jnp.dot(xb, wg_ref[0], preferred_element_type=F32)
    up = jnp.dot(xb, wu_ref[0], preferred_element_type=F32)
    act = (gate * _sigmoid(gate)) * up
    out = x + 0.5 * jnp.dot(act.astype(BF16), wd_ref[0], preferred_element_type=F32)
    o_ref[...] = out if gout_ref is None else _rms(out, gout_ref[...])


def _ffn(h, g, wg, wu, wd, l, g_out=None):
    n_tok = h.shape[0]
    extra = [] if g_out is None else [g_out]
    return pl.pallas_call(
        _ffn_body,
        out_shape=jax.ShapeDtypeStruct(h.shape, F32),
        grid=(n_tok // TOKEN_TILE,),
        in_specs=[
            pl.BlockSpec((TOKEN_TILE, D_MODEL), lambda i: (i, 0)),
            _const_spec((1, D_MODEL)),
            _layer_spec(l, D_MODEL, D_FF),
            _layer_spec(l, D_MODEL, D_FF),
            _layer_spec(l, D_FF, D_MODEL),
        ] + [_const_spec((1, D_MODEL))] * len(extra),
        out_specs=pl.BlockSpec((TOKEN_TILE, D_MODEL), lambda i: (i, 0)),
        compiler_params=pltpu.CompilerParams(
            dimension_semantics=("arbitrary",), vmem_limit_bytes=VMEM_LIMIT_BYTES),
        name="ffn",
    )(h, g, wg, wu, wd, *extra)


def _inproj_body(h_ref, g_ref, why_ref, *rest):
    w_qkv = rest[:3 * N_GROUPS]
    wrg_ref, hy_ref, q0_ref, q1_ref, q2_ref, rg_ref, xs_ref = rest[3 * N_GROUPS:]
    xn = _rms(h_ref[0], g_ref[...])
    xb = xn.astype(BF16)
    hy_ref[0] = jnp.dot(xb, why_ref[0], preferred_element_type=F32).astype(BF16)
    rg_ref[0] = jnp.dot(xb, wrg_ref[0], preferred_element_type=F32).astype(BF16)

    for s, slab in enumerate(_lane_slabs(xn)):
        xs_ref[s] = slab

    def by_residue(d):
        if d == 1:
            return xb
        n = TOKEN_TILE // d
        cols = [jnp.concatenate([xs_ref.at[s][pl.ds(r, n, stride=d), :] for r in range(d)], axis=0)
                for s in range(D_MODEL // LANES)]
        return jnp.concatenate(cols, axis=1).astype(BF16)

    for g, (q_ref, (_, d)) in enumerate(zip((q0_ref, q1_ref, q2_ref), ATTN_GROUPS)):
        lhs = by_residue(d)
        for part in range(3):
            out = jnp.dot(lhs, w_qkv[3 * g + part][0], preferred_element_type=F32)
            if part == 0:
                out = out * Q_SCALE
            q_ref[0, :, :, part * ATTN_WIDTH:(part + 1) * ATTN_WIDTH] = (
                out.astype(BF16).reshape(d, TOKEN_TILE // d, ATTN_WIDTH))


def _inproj(h3, g, w_in, l):
    nb, seq, _ = h3.shape
    dils = [d for _, d in ATTN_GROUPS]
    qshape = lambda d: jax.ShapeDtypeStruct((nb, d, seq // d, 3 * ATTN_WIDTH), BF16)
    qspec = lambda d: pl.BlockSpec((1, d, TOKEN_TILE // d, 3 * ATTN_WIDTH), lambda b, i: (b, 0, i, 0))
    qkv_block0 = HY_COLS // ATTN_WIDTH
    w_qkv_specs = [_layer_spec(l, D_MODEL, ATTN_WIDTH, qkv_block0 + part * N_GROUPS + grp)
                   for grp in range(N_GROUPS) for part in range(3)]
    return pl.pallas_call(
        _inproj_body,
        out_shape=(
            jax.ShapeDtypeStruct((nb, seq, HY_COLS), BF16),
            qshape(dils[0]), qshape(dils[1]), qshape(dils[2]),
            jax.ShapeDtypeStruct((nb, seq, RG_COLS), BF16),
        ),
        grid=(nb, seq // TOKEN_TILE),
        in_specs=[
            pl.BlockSpec((1, TOKEN_TILE, D_MODEL), lambda b, i: (b, i, 0)),
            _const_spec((1, D_MODEL)),
            _layer_spec(l, D_MODEL, HY_COLS, 0),
            *w_qkv_specs,
            _layer_spec(l, D_MODEL, RG_COLS, (HY_COLS + QKV_COLS) // RG_COLS),
        ],
        out_specs=(
            pl.BlockSpec((1, TOKEN_TILE, HY_COLS), lambda b, i: (b, i, 0)),
            qspec(dils[0]), qspec(dils[1]), qspec(dils[2]),
            pl.BlockSpec((1, TOKEN_TILE, RG_COLS), lambda b, i: (b, i, 0)),
        ),
        scratch_shapes=[pltpu.VMEM((D_MODEL // LANES, TOKEN_TILE, LANES), F32)],
        compiler_params=pltpu.CompilerParams(
            dimension_semantics=("arbitrary", "arbitrary"), vmem_limit_bytes=VMEM_LIMIT_BYTES),
        name="inproj",
    )(h3, g, w_in, *([w_in] * (3 * N_GROUPS)), w_in)


def _dft_mats(half):
    n = jnp.arange(half, dtype=jnp.int32)
    kn = (n[:, None] * n[None, :]) % (2 * half)
    ang = kn.astype(F32) * (math.pi / half)
    return jnp.cos(ang).astype(BF16), (-jnp.sin(ang)).astype(BF16)


def _hy_positions(seq):
    t = jnp.linspace(0.0, 1.0, seq, dtype=F32)[:, None]
    tr = jnp.arange(seq, dtype=F32)[:, None]
    wpos = 2.0 * math.pi * tr / seq
    fb = jnp.linspace(1e-4, HY_BANDS - 1, HY_BANDS, dtype=F32)[None, :]
    z = jnp.concatenate([t, jnp.cos(fb * wpos), -jnp.sin(fb * wpos)], axis=-1)
    return jnp.pad(z, ((0, 0), (0, LANES - HY_EMB)))


def _hy_filter_body(z_ref, w1_ref, b1_ref, w2_ref, b2_ref, w3_ref, b3_ref, fr_ref,
                    wf_ref, wb_ref, dl_ref, fre_ref, fim_ref,
                    g_ref, gn_ref, hdn_ref):
    hi = lax.Precision.HIGHEST
    seq = z_ref.shape[0]
    width = wf_ref.shape[-1]

    @pl.when((pl.program_id(0) == 0) & (pl.program_id(1) == 0))
    def _():
        fr = fr_ref[...]
        hdn = jnp.sin(fr * (jnp.dot(z_ref[...], w1_ref[...], precision=hi,
                                    preferred_element_type=F32) + b1_ref[...]))
        hdn = jnp.sin(fr * (jnp.dot(hdn, w2_ref[...], precision=hi,
                                    preferred_element_type=F32) + b2_ref[...]))
        hdn_ref[...] = jnp.sin(fr * (jnp.dot(hdn, w3_ref[...], precision=hi,
                                             preferred_element_type=F32) + b3_ref[...]))

    hdn = hdn_ref[...]
    row = lax.broadcasted_iota(jnp.int32, (seq, width), 0)
    t = row.astype(F32) * (1.0 / (seq - 1))
    decay = jnp.exp(-t * dl_ref[...]) + HY_MOD_SHIFT
    kf = jnp.dot(hdn, wf_ref[0], precision=hi, preferred_element_type=F32) * decay
    kb = jnp.dot(hdn, wb_ref[0], precision=hi, preferred_element_type=F32) * decay
    kb = jnp.where(row == 0, 0.0, kb)
    norm = (jnp.sum(jnp.abs(kf), axis=0, keepdims=True)
            + jnp.sum(jnp.abs(kb), axis=0, keepdims=True) + HY_NORM_EPS)
    kf = kf / norm
    kb = kb / norm

    nseg = seq // HY_P
    krow = lax.broadcasted_iota(jnp.int32, (HY_P, width), 0)
    sk = (1 - 2 * (krow & 1)).astype(F32)
    ck = jnp.where(krow == 0, 0.5 / HY_P, sk * (1.0 / HY_P))
    zero_row = jnp.zeros((1, width), F32)

    def dft(x):
        xb = x.astype(BF16)
        return (jnp.dot(fre_ref[...], xb, preferred_element_type=F32),
                jnp.dot(fim_ref[...], xb, preferred_element_type=F32),
                jnp.sum(x * sk, axis=0, keepdims=True))

    spectra = {}
    for e in range(nseg):
        spectra[e] = dft(kf[e * HY_P:(e + 1) * HY_P])
        lead = kb[e * HY_P:(e + 1) * HY_P]
        bre, bim, bny = dft(jnp.where(krow == 0, 0.0, lead))
        first = kb[(e + 1) * HY_P:(e + 1) * HY_P + 1] if e + 1 < nseg else zero_row
        spectra[-(e + 1)] = (first + sk * bre, -(sk * bim), first + bny)

    for delta in range(-(nseg - 1), nseg):
        are, aim, any_ = spectra[delta - 1]
        bre, bim, bny = spectra[delta]
        gre = (are + sk * bre) * ck
        gim = (aim + sk * bim) * ck
        g_ref[0, delta + nseg - 1, 0] = gre.astype(BF16)
        g_ref[0, delta + nseg - 1, 1] = gim.astype(BF16)
        gn_ref[0, delta + nseg - 1:delta + nseg, :] = (any_ + bny) * (0.5 / HY_P)
    gn_ref[0, 2 * nseg - 1:2 * nseg, :] = zero_row


def _hy_filter(z, w1, b1, w2, b2, w3, b3, freq, wout, deltas, fre, fim):
    seq = z.shape[0]
    wf = wout[:, :HY_ORDER * HY_WIDTH].reshape(HY_HIDDEN, HY_ORDER, HY_WIDTH).transpose(1, 0, 2)
    wb = wout[:, HY_ORDER * HY_WIDTH:].reshape(HY_HIDDEN, HY_ORDER, HY_WIDTH).transpose(1, 0, 2)
    w1p = jnp.pad(w1, ((0, LANES - HY_EMB), (0, 0)))
    row = lambda a: a.reshape(1, -1)
    nseg = seq // HY_P
    return pl.pallas_call(
        _hy_filter_body,
        out_shape=(
            jax.ShapeDtypeStruct((HY_ORDER, 2 * nseg - 1, 2, HY_P, HY_WIDTH), BF16),
            jax.ShapeDtypeStruct((HY_ORDER, 2 * nseg, HY_WIDTH), F32),
        ),
        grid=(HY_ORDER, HY_WIDTH // HY_CBLK),
        in_specs=[
            _const_spec((seq, LANES)),
            _const_spec((LANES, HY_HIDDEN)), _const_spec((1, HY_HIDDEN)),
            _const_spec((HY_HIDDEN, HY_HIDDEN)), _const_spec((1, HY_HIDDEN)),
            _const_spec((HY_HIDDEN, HY_HIDDEN)), _const_spec((1, HY_HIDDEN)),
            _const_spec((1, HY_HIDDEN)),
            pl.BlockSpec((1, HY_HIDDEN, HY_CBLK), lambda o, c: (o, 0, c)),
            pl.BlockSpec((1, HY_HIDDEN, HY_CBLK), lambda o, c: (o, 0, c)),
            pl.BlockSpec((1, HY_CBLK), lambda o, c: (0, c)),
            _const_spec((HY_P, HY_P)), _const_spec((HY_P, HY_P)),
        ],
        out_specs=(
            pl.BlockSpec((1, 2 * nseg - 1, 2, HY_P, HY_CBLK), lambda o, c: (o, 0, 0, 0, c)),
            pl.BlockSpec((1, 2 * nseg, HY_CBLK), lambda o, c: (o, 0, c)),
        ),
        scratch_shapes=[pltpu.VMEM((seq, HY_HIDDEN), F32)],
        compiler_params=pltpu.CompilerParams(
            dimension_semantics=("arbitrary", "arbitrary"), vmem_limit_bytes=VMEM_LIMIT_BYTES),
        name="hy_filter",
    )(z, w1p, row(b1), w2, row(b2), w3, row(b3), row(freq), wf, wb, row(deltas), fre, fim)


def _hy_conv_body(v_ref, x1_ref, x2_ref, wv_ref, w1_ref, w2_ref, bv_ref, b1_ref, b2_ref,
                  fre_ref, fim_ref, g_ref, gn_ref, skip_ref, o_ref,
                  z_ref, g1_ref, g2_ref, ure_ref, uim_ref):
    seq = v_ref.shape[1]
    nseg = seq // HY_P
    sgn = (1 - 2 * (lax.broadcasted_iota(jnp.int32, (HY_P, HY_CBLK), 0) & 1)).astype(F32)

    edge = lax.broadcasted_iota(jnp.int32, (SUBLANES, HY_CBLK), 0)

    def short_conv(x_ref, w_ref, b_ref, dst_ref):
        x = x_ref[0].astype(F32)
        w = w_ref[...]
        dst_ref[...] = (w[0:1] * pltpu.roll(x, 1, 0) + w[1:2] * x
                        + w[2:3] * pltpu.roll(x, seq - 1, 0) + b_ref[...])
        dst_ref[0:SUBLANES, :] = dst_ref[0:SUBLANES, :] - jnp.where(
            edge == 0, w[0:1] * x[seq - 1:seq], 0.0)
        dst_ref[seq - SUBLANES:seq, :] = dst_ref[seq - SUBLANES:seq, :] - jnp.where(
            edge == SUBLANES - 1, w[2:3] * x[0:1], 0.0)

    short_conv(v_ref, wv_ref, bv_ref, z_ref)
    short_conv(x1_ref, w1_ref, b1_ref, g1_ref)
    short_conv(x2_ref, w2_ref, b2_ref, g2_ref)
    skip = skip_ref[...]
    blk = lambda i: slice(i * HY_P, (i + 1) * HY_P)

    for o, gate_ref in enumerate((g1_ref, g2_ref)):
        nyq = []
        for j in range(nseg):
            x = z_ref[blk(j), :]
            xb = x.astype(BF16)
            ure_ref[j] = jnp.dot(fre_ref[...], xb, preferred_element_type=F32).astype(BF16)
            uim_ref[j] = jnp.dot(fim_ref[...], xb, preferred_element_type=F32).astype(BF16)
            nyq.append(jnp.sum(x * sgn, axis=0, keepdims=True))
        for i in range(nseg):
            zre = zim = zny = None
            for j in range(nseg):
                d = i - j + nseg - 1
                gre, gim = g_ref[o, d, 0], g_ref[o, d, 1]
                ure, uim = ure_ref[j], uim_ref[j]
                tre = gre * ure - gim * uim
                tim = gre * uim + gim * ure
                tny = gn_ref[o, d:d + 1, :] * nyq[j]
                zre, zim, zny = (tre, tim, tny) if zre is None else (zre + tre, zim + tim, zny + tny)
            y = (jnp.dot(fre_ref[...], zre, preferred_element_type=F32)
                 + jnp.dot(fim_ref[...], zim, preferred_element_type=F32)
                 + sgn * zny)
            z = gate_ref[blk(i), :] * (y + skip[o:o + 1] * z_ref[blk(i), :])
            if o + 1 < HY_ORDER:
                z_ref[blk(i), :] = z
            else:
                o_ref[0, blk(i), :] = z.astype(BF16)


def _hy_conv(u_hy, conv_w, conv_b, fre, fim, g, gn, skip):
    nb, seq, _ = u_hy.shape
    ncb = HY_WIDTH // HY_CBLK
    nseg = seq // HY_P
    data = lambda part: pl.BlockSpec((1, seq, HY_CBLK), lambda c, b: (b, 0, part * ncb + c))
    wspec = lambda part: pl.BlockSpec((3, HY_CBLK), lambda c, b: (0, part * ncb + c))
    bspec = lambda part: pl.BlockSpec((1, HY_CBLK), lambda c, b: (0, part * ncb + c))
    return pl.pallas_call(
        _hy_conv_body,
        out_shape=jax.ShapeDtypeStruct((nb, seq, HY_WIDTH), BF16),
        grid=(ncb, nb),
        in_specs=[
            data(0), data(1), data(2),
            wspec(0), wspec(1), wspec(2),
            bspec(0), bspec(1), bspec(2),
            _const_spec((HY_P, HY_P)), _const_spec((HY_P, HY_P)),
            pl.BlockSpec((HY_ORDER, 2 * nseg - 1, 2, HY_P, HY_CBLK), lambda c, b: (0, 0, 0, 0, c),
                         pipeline_mode=pl.Buffered(1)),
            pl.BlockSpec((HY_ORDER, 2 * nseg, HY_CBLK), lambda c, b: (0, 0, c)),
            pl.BlockSpec((HY_ORDER, HY_CBLK), lambda c, b: (0, c)),
        ],
        out_specs=pl.BlockSpec((1, seq, HY_CBLK), lambda c, b: (b, 0, c)),
        scratch_shapes=[
            pltpu.VMEM((seq, HY_CBLK), F32), pltpu.VMEM((seq, HY_CBLK), F32),
            pltpu.VMEM((seq, HY_CBLK), F32),
            pltpu.VMEM((nseg, HY_P, HY_CBLK), BF16), pltpu.VMEM((nseg, HY_P, HY_CBLK), BF16),
        ],
        compiler_params=pltpu.CompilerParams(
            dimension_semantics=("arbitrary", "arbitrary"), vmem_limit_bytes=VMEM_LIMIT_BYTES),
        name="hy_conv",
    )(u_hy, u_hy, u_hy, conv_w, conv_w, conv_w, conv_b, conv_b, conv_b,
      fre, fim, g, gn, skip)


def _bucket_matrix(dilation):
    qi = np.arange(Q_TILE, dtype=np.int64)[:, None]
    kj = np.arange(K_WIN, dtype=np.int64)[None, :]
    delta = kj - BAND - qi
    rel = delta * dilation
    half = N_BUCKETS // 2
    n = np.abs(rel)
    nf = np.maximum(n, 1).astype(np.float32)
    large = BUCKET_MAX_EXACT + (
        np.log(nf / np.float32(BUCKET_MAX_EXACT)) / np.float32(math.log(BUCKET_MAX_DIST / BUCKET_MAX_EXACT))
        * np.float32(half - BUCKET_MAX_EXACT)).astype(np.int32)
    large = np.minimum(large, half - 1)
    bucket = np.where(rel > 0, half, 0) + np.where(n < BUCKET_MAX_EXACT, n, large)
    return np.where(np.abs(delta) <= BAND, bucket, -1).astype(np.int32)


def _attn_body(group, tbl_ref, bkt_ref, q_ref, k_ref, v_ref, o_ref, l_ref,
               bias_ref, kpad_ref, vpad_ref):
    _, dil, ls, _ = q_ref.shape
    tiles_per_res = ls // Q_TILE

    kcol = lax.broadcasted_iota(jnp.int32, (Q_TILE, K_WIN), 1)

    @pl.when(pl.program_id(0) == 0)
    def _():
        bkt = bkt_ref[...]
        for h in range(HEADS):
            acc = jnp.full((Q_TILE, K_WIN), NEG_INF, F32)
            for bk in range(N_BUCKETS):
                acc = jnp.where(bkt == bk, tbl_ref[bk, group * HEADS + h] * LOG2E, acc)
            no_head = jnp.where(kcol < BAND, NEG_INF, acc)
            bias_ref[0, h] = acc
            bias_ref[1, h] = no_head
            bias_ref[2, h] = jnp.where(kcol >= Q_TILE + BAND, NEG_INF, acc)
            bias_ref[3, h] = jnp.where(kcol >= Q_TILE + BAND, NEG_INF, no_head)
        zeros = jnp.zeros((dil, BAND, ATTN_WIDTH), BF16)
        kpad_ref[:, 0:BAND] = zeros
        kpad_ref[:, BAND + ls:BAND + ls + BAND] = zeros
        vpad_ref[:, 0:BAND] = zeros
        vpad_ref[:, BAND + ls:BAND + ls + BAND] = zeros

    kpad_ref[:, BAND:BAND + ls] = k_ref[0]
    vpad_ref[:, BAND:BAND + ls] = v_ref[0]

    lane = lax.broadcasted_iota(jnp.int32, (Q_TILE, LANES), 1)
    low_half = lane < HEAD_DIM

    def tile(t, carry):
        r = t // tiles_per_res
        jt = t % tiles_per_res
        j0 = pl.multiple_of(jt * Q_TILE, Q_TILE)
        edge = jnp.where(jt == 0, 1, 0) + jnp.where(jt == tiles_per_res - 1, 2, 0)
        lse_tile = jnp.zeros((Q_TILE, LANES), F32)
        for hp in range(HEADS // 2):
            cols = slice(hp * LANES, (hp + 1) * LANES)
            q2 = q_ref[0, r, pl.ds(j0, Q_TILE), cols]
            kw = kpad_ref[r, pl.ds(j0, K_WIN), cols]
            vw = vpad_ref[r, pl.ds(j0, K_WIN), cols]
            q_pair = jnp.concatenate(
                [jnp.where(low_half, q2, 0.0), jnp.where(low_half, 0.0, q2)], axis=0).astype(BF16)
            s_pair = lax.dot_general(q_pair, kw, (((1,), (1,)), ((), ())), preferred_element_type=F32)
            probs = []
            for half in range(2):
                h = 2 * hp + half
                s = s_pair[half * Q_TILE:(half + 1) * Q_TILE] + bias_ref[edge, h]
                m = jnp.max(s, axis=-1, keepdims=True)
                p = jnp.exp2(s - m)
                den = jnp.sum(p, axis=-1, keepdims=True)
                probs.append(p.astype(BF16))
                lse_tile = jnp.where(lane == h, m, jnp.where(lane == HEADS + h, den, lse_tile))
            pv = jnp.dot(jnp.concatenate(probs, axis=0), vw, preferred_element_type=F32)
            o_ref[0, r, pl.ds(j0, Q_TILE), cols] = jnp.where(
                low_half, pv[:Q_TILE], pv[Q_TILE:]).astype(BF16)
        l_ref[0, r, pl.ds(j0, Q_TILE), :] = lse_tile
        return carry

    lax.fori_loop(0, dil * tiles_per_res, tile, 0, unroll=ATTN_UNROLL)


def _attn_group(qkv_g, tbl, group, dilation):
    nb, _, ls, _ = qkv_g.shape
    part = lambda which: pl.BlockSpec((1, dilation, ls, ATTN_WIDTH), lambda b: (b, 0, 0, which))
    bkt = jnp.asarray(_bucket_matrix(dilation))
    return pl.pallas_call(
        functools.partial(_attn_body, group),
        out_shape=(
            jax.ShapeDtypeStruct((nb, dilation, ls, ATTN_WIDTH), BF16),
            jax.ShapeDtypeStruct((nb, dilation, ls, LANES), F32),
        ),
        grid=(nb,),
        in_specs=[
            pl.BlockSpec(memory_space=pltpu.SMEM),
            _const_spec((Q_TILE, K_WIN)),
            part(0), part(1), part(2),
        ],
        out_specs=(
            pl.BlockSpec((1, dilation, ls, ATTN_WIDTH), lambda b: (b, 0, 0, 0)),
            pl.BlockSpec((1, dilation, ls, LANES), lambda b: (b, 0, 0, 0)),
        ),
        scratch_shapes=[
            pltpu.VMEM((N_EDGE_CASES, HEADS, Q_TILE, K_WIN), F32),
            pltpu.VMEM((dilation, ls + 2 * BAND, ATTN_WIDTH), BF16),
            pltpu.VMEM((dilation, ls + 2 * BAND, ATTN_WIDTH), BF16),
        ],
        compiler_params=pltpu.CompilerParams(
            dimension_semantics=("arbitrary",), vmem_limit_bytes=VMEM_LIMIT_BYTES),
        name=f"attn_g{group}",
    )(tbl, bkt, qkv_g, qkv_g, qkv_g)


def _rglru_body(nt, cw_ref, cb_ref, wf_ref, wb_ref, bf_ref, bb_ref, lam_ref,
                xf_ref, pf_ref, nf_ref, xb_ref, pb_ref, nb_ref,
                hf_ref, hb_ref, xw_ref, a_ref, u_ref, hs_ref, cf_ref, cbk_ref):
    i = pl.program_id(0)
    nbat, tt, width = xf_ref.shape
    n_slab = width // LANES
    n_grp = nbat // SUBLANES
    grp_rows = tt * SUBLANES
    halo_rows = HALO_T * SUBLANES

    @pl.when(i == 0)
    def _():
        cf_ref[...] = jnp.zeros_like(cf_ref)
        cbk_ref[...] = jnp.zeros_like(cbk_ref)

    def to_time_major(src_ref, n_t, row0, keep):
        for b in range(nbat):
            grp, b8 = divmod(b, SUBLANES)
            x = src_ref[b].astype(F32)
            x = x if keep is None else jnp.where(keep, x, 0.0)
            for s, slab in enumerate(_lane_slabs(x)):
                xw_ref.at[s * n_grp + grp][pl.ds(row0 + b8, n_t, stride=SUBLANES), :] = slab

    def gates(x_ref, p_ref, n_ref, tile_idx, w_ref, b_ref, lam):
        to_time_major(p_ref, HALO_T, 0, tile_idx > 0)
        to_time_major(x_ref, tt, halo_rows, None)
        to_time_major(n_ref, HALO_T, halo_rows + grp_rows, tile_idx < nt - 1)
        cw = cw_ref[...]

        def tap(k):
            start = halo_rows + (k - 2) * SUBLANES
            return jnp.concatenate(
                [jnp.concatenate([xw_ref[s * n_grp + grp, start:start + grp_rows, :]
                                  for s in range(n_slab)], axis=1) for grp in range(n_grp)], axis=0)

        xc = cw[0:1] * tap(0) + cw[1:2] * tap(1) + cw[2:3] * tap(2) + cw[3:4] * tap(3) + cb_ref[...]
        g = jnp.dot(xc.astype(BF16), w_ref[...], preferred_element_type=F32) + b_ref[...]
        r2 = 1.0 + jnp.tanh(g[:, :width])
        gi2 = 1.0 + jnp.tanh(g[:, width:])
        softplus = jnp.maximum(-lam, 0.0) + jnp.log(1.0 + jnp.exp(-jnp.abs(lam)))
        log_a = (-0.5 * RG_C * softplus) * r2
        a = jnp.exp(log_a)
        a_ref[...] = a
        one_m_a2 = 1.0 - a * a
        root = jnp.where(one_m_a2 > 0.0, one_m_a2 * lax.rsqrt(one_m_a2), 0.0)
        u_ref[...] = root * (gi2 * xc)

    def step(t, hs):
        r0 = pl.multiple_of(t * SUBLANES, SUBLANES)
        out = []
        for grp, h in enumerate(hs):
            rows = pl.ds(grp * grp_rows + r0, SUBLANES)
            h = a_ref[rows, :] * h + u_ref[rows, :]
            for s, slab in enumerate(_lane_slabs(h)):
                hs_ref[s * n_grp + grp, pl.ds(r0, SUBLANES), :] = slab
            out.append(h)
        return tuple(out)

    def scan(carry_ref, time_of):
        init = tuple(carry_ref[grp * SUBLANES:(grp + 1) * SUBLANES, :] for grp in range(n_grp))
        last = lax.fori_loop(0, tt, lambda k, hs: step(time_of(k), hs), init, unroll=RG_SCAN_UNROLL)
        carry_ref[...] = jnp.concatenate(last, axis=0)

    def to_batch_major(dst_ref):
        for b in range(nbat):
            grp, b8 = divmod(b, SUBLANES)
            dst_ref[b] = jnp.concatenate(
                [hs_ref.at[s * n_grp + grp][pl.ds(b8, tt, stride=SUBLANES), :] for s in range(n_slab)],
                axis=1).astype(BF16)

    gates(xf_ref, pf_ref, nf_ref, i, wf_ref, bf_ref, lam_ref[0:1])
    scan(cf_ref, lambda k: k)
    to_batch_major(hf_ref)

    gates(xb_ref, pb_ref, nb_ref, nt - 1 - i, wb_ref, bb_ref, lam_ref[1:2])
    scan(cbk_ref, lambda k: tt - 1 - k)
    to_batch_major(hb_ref)


def _rglru(rg, conv_w, conv_b, w_f, w_b, b_f, b_b, lam):
    nbat, seq, _ = rg.shape
    nt = seq // RG_TT
    hpt = RG_TT // HALO_T
    nh = seq // HALO_T
    cur = lambda f: pl.BlockSpec((nbat, RG_TT, RG_WIDTH), lambda i: (0, f(i), 0))
    prv = lambda f: pl.BlockSpec(
        (nbat, HALO_T, RG_WIDTH), lambda i: (0, jnp.maximum(f(i) * hpt - 1, 0), 0))
    nxt = lambda f: pl.BlockSpec(
        (nbat, HALO_T, RG_WIDTH), lambda i: (0, jnp.minimum((f(i) + 1) * hpt, nh - 1), 0))
    fw = lambda i: i
    bw = lambda i: nt - 1 - i
    n_slab = RG_WIDTH // LANES
    n_grp = nbat // SUBLANES
    return pl.pallas_call(
        functools.partial(_rglru_body, nt),
        out_shape=(
            jax.ShapeDtypeStruct((nbat, seq, RG_WIDTH), BF16),
            jax.ShapeDtypeStruct((nbat, seq, RG_WIDTH), BF16),
        ),
        grid=(nt,),
        in_specs=[
            _const_spec((RG_CONV, RG_WIDTH)), _const_spec((1, RG_WIDTH)),
            _const_spec((RG_WIDTH, 2 * RG_WIDTH)), _const_spec((RG_WIDTH, 2 * RG_WIDTH)),
            _const_spec((1, 2 * RG_WIDTH)), _const_spec((1, 2 * RG_WIDTH)),
            _const_spec((2, RG_WIDTH)),
            cur(fw), prv(fw), nxt(fw), cur(bw), prv(bw), nxt(bw),
        ],
        out_specs=(cur(fw), cur(bw)),
        scratch_shapes=[
            pltpu.VMEM((n_slab * n_grp, (RG_TT + 2 * HALO_T) * SUBLANES, LANES), F32),
            pltpu.VMEM((RG_TT * nbat, RG_WIDTH), F32),
            pltpu.VMEM((RG_TT * nbat, RG_WIDTH), F32),
            pltpu.VMEM((n_slab * n_grp, RG_TT * SUBLANES, LANES), F32),
            pltpu.VMEM((nbat, RG_WIDTH), F32),
            pltpu.VMEM((nbat, RG_WIDTH), F32),
        ],
        compiler_params=pltpu.CompilerParams(
            dimension_semantics=("arbitrary",), vmem_limit_bytes=VMEM_LIMIT_BYTES),
        name="rglru",
    )(conv_w, conv_b, w_f, w_b, b_f, b_b, lam, rg, rg, rg, rg, rg, rg)


def _block_diag(w):
    eye = jnp.eye(RG_BLOCKS, dtype=w.dtype)
    return jnp.einsum("hij,hk->hikj", w, eye).reshape(RG_WIDTH, RG_WIDTH)


def _merge_body(h_ref, g_ref, wg_ref, bg_ref, ya_ref, o0_ref, o1_ref, o2_ref,
                l0_ref, l1_ref, l2_ref, hf_ref, hb_ref, gate_ref, exp_ref,
                wphy_ref, wpat_ref, wprg_ref, wout_ref, out_ref, os_ref, ls_ref):
    x = h_ref[0]
    xb = _rms(x, g_ref[...]).astype(BF16)
    gates = 1.0 + jnp.tanh(jnp.dot(xb, wg_ref[0], preferred_element_type=F32) + bg_ref[...])

    def token_order(o_ref, l_ref):
        d, n = o_ref.shape[1], o_ref.shape[2]
        if d == 1:
            return o_ref[0, 0].astype(F32), l_ref[0, 0]
        for r in range(d):
            for s, slab in enumerate(_lane_slabs(o_ref[0, r].astype(F32))):
                os_ref.at[s][pl.ds(r, n, stride=d), :] = slab
            ls_ref[pl.ds(r, n, stride=d), :] = l_ref[0, r]
        o = jnp.concatenate([os_ref[s] for s in range(ATTN_WIDTH // LANES)], axis=1)
        return o, ls_ref[...]

    o0, l0 = token_order(o0_ref, l0_ref)
    o1, l1 = token_order(o1_ref, l1_ref)
    o2, l2 = token_order(o2_ref, l2_ref)

    head_lane = lax.broadcasted_iota(jnp.int32, l0.shape, 1) < HEADS
    d0, d1, d2 = (pltpu.roll(l, LANES - HEADS, 1) for l in (l0, l1, l2))
    m = jnp.maximum(jnp.maximum(l0, l1), l2)
    e0, e1, e2 = jnp.exp2(l0 - m), jnp.exp2(l1 - m), jnp.exp2(l2 - m)
    inv = 1.0 / jnp.where(head_lane, e0 * d0 + e1 * d1 + e2 * d2, 1.0)

    def widen(w):
        return jnp.dot(w.astype(BF16), exp_ref[...], preferred_element_type=F32)

    y_b = widen(e0 * inv) * o0 + widen(e1 * inv) * o1 + widen(e2 * inv) * o2

    gt = gate_ref[0].astype(F32)
    gelu = 0.5 * gt * (1.0 + jnp.tanh(math.sqrt(2.0 / math.pi) * (gt + 0.044715 * (gt * gt * gt))))
    y_c = (hf_ref[0].astype(F32) + hb_ref[0].astype(F32)) * gelu

    merged = (gates[:, :D_MODEL]
              * jnp.dot(ya_ref[0], wphy_ref[0], preferred_element_type=F32)
              + gates[:, D_MODEL:2 * D_MODEL]
              * jnp.dot(y_b.astype(BF16), wpat_ref[0], preferred_element_type=F32)
              + gates[:, 2 * D_MODEL:]
              * jnp.dot(y_c.astype(BF16), wprg_ref[0], preferred_element_type=F32))
    out_ref[0] = x + jnp.dot(merged.astype(BF16), wout_ref[0], preferred_element_type=F32)


def _merge(h3, g, w_gate, b_gate, y_a, attn, hf, hb, rg, expand, wp_hy, wp_attn, wp_rg, w_out, l):
    nb, seq, _ = h3.shape
    tok = lambda width, col=0: pl.BlockSpec((1, TOKEN_TILE, width), lambda b, i: (b, i, col))
    res = lambda d, width: pl.BlockSpec((1, d, TOKEN_TILE // d, width), lambda b, i: (b, 0, i, 0))
    (o0, l0), (o1, l1), (o2, l2) = attn
    dils = [d for _, d in ATTN_GROUPS]
    return pl.pallas_call(
        _merge_body,
        out_shape=jax.ShapeDtypeStruct(h3.shape, F32),
        grid=(nb, seq // TOKEN_TILE),
        in_specs=[
            tok(D_MODEL), _const_spec((1, D_MODEL)),
            _layer_spec(l, D_MODEL, N_BRANCH * D_MODEL), _const_spec((1, N_BRANCH * D_MODEL)),
            tok(HY_WIDTH),
            res(dils[0], ATTN_WIDTH), res(dils[1], ATTN_WIDTH), res(dils[2], ATTN_WIDTH),
            res(dils[0], LANES), res(dils[1], LANES), res(dils[2], LANES),
            tok(RG_WIDTH), tok(RG_WIDTH), tok(RG_WIDTH, 1),
            _const_spec((LANES, ATTN_WIDTH)),
            _layer_spec(l, HY_WIDTH, D_MODEL), _layer_spec(l, ATTN_WIDTH, D_MODEL),
            _layer_spec(l, RG_WIDTH, D_MODEL), _layer_spec(l, D_MODEL, D_MODEL),
        ],
        out_specs=tok(D_MODEL),
        scratch_shapes=[
            pltpu.VMEM((ATTN_WIDTH // LANES, TOKEN_TILE, LANES), F32),
            pltpu.VMEM((TOKEN_TILE, LANES), F32),
        ],
        compiler_params=pltpu.CompilerParams(
            dimension_semantics=("arbitrary", "arbitrary"), vmem_limit_bytes=VMEM_LIMIT_BYTES),
        name="merge",
    )(h3, g, w_gate, b_gate, y_a, o0, o1, o2, l0, l1, l2, hf, hb, rg, expand,
      wp_hy, wp_attn, wp_rg, w_out)


def _mixer(h3, l, p, wb, consts):
    fre, fim, zpos, deltas, expand = consts
    hy, q0, q1, q2, rg = _inproj(h3, p["mix_norm"][l].reshape(1, -1), wb["w_in"], l)

    g_spec, g_nyq = _hy_filter(
        zpos, p["hy_w1"][l], p["hy_b1"][l], p["hy_w2"][l], p["hy_b2"][l], p["hy_w3"][l],
        p["hy_b3"][l], p["hy_freq"][l], p["hy_wout"][l], deltas, fre, fim)
    y_a = _hy_conv(hy, p["hy_conv_w"][l], p["hy_conv_b"][l].reshape(1, -1),
                   fre, fim, g_spec, g_nyq, p["hy_skip"][l])

    attn = [_attn_group(q, p["rel_bias"], g, dil)
            for g, (q, (_, dil)) in enumerate(zip((q0, q1, q2), ATTN_GROUPS))]

    rg_w = lambda d: (0.5 * jnp.concatenate(
        [_block_diag(p["rg_wa"][l, d]), _block_diag(p["rg_wx"][l, d])], axis=1)).astype(BF16)
    rg_b = lambda d: 0.5 * jnp.concatenate([p["rg_ba"][l, d], p["rg_bx"][l, d]]).reshape(1, -1)
    hf, hb = _rglru(rg, p["rg_conv_w"][l], p["rg_conv_b"][l].reshape(1, -1),
                    rg_w(0), rg_w(1), rg_b(0), rg_b(1), p["rg_lambda"][l])

    return _merge(
        h3, p["mix_norm"][l].reshape(1, -1), wb["w_gate"], 0.5 * p["b_gate"][l].reshape(1, -1),
        y_a, attn, hf, hb, rg, expand,
        wb["w_proj_hy"], wb["w_proj_attn"], wb["w_proj_rg"], wb["w_out"], l)


def _forward(x, p):
    nb, seq, _ = x.shape
    fre, fim = _dft_mats(HY_P)
    deltas = jnp.abs(jnp.linspace(math.log(HY_DECAY_TARGET) / HY_FAST_DECAY,
                                  math.log(HY_DECAY_TARGET) / HY_SLOW_DECAY, HY_WIDTH, dtype=F32))
    head_of_lane = jnp.arange(ATTN_WIDTH, dtype=jnp.int32) // HEAD_DIM
    expand = (jnp.arange(LANES, dtype=jnp.int32)[:, None] == head_of_lane[None, :]).astype(BF16)
    consts = (fre, fim, _hy_positions(seq), deltas, expand)

    halved = ("w_gate", "w_out", "w_proj_rg")
    wb = {k: (0.5 * p[k] if k in halved else p[k]).astype(BF16) for k in (
        "ffn1_wg", "ffn1_wu", "ffn1_wd", "ffn2_wg", "ffn2_wu", "ffn2_wd",
        "w_in", "w_gate", "w_proj_hy", "w_proj_attn", "w_proj_rg", "w_out")}

    def ffn(h3, which, l, g_out=None):
        out = _ffn(h3.reshape(nb * seq, D_MODEL), p[which + "_norm"][l].reshape(1, -1),
                   wb[which + "_wg"], wb[which + "_wu"], wb[which + "_wd"], l, g_out)
        return out.reshape(nb, seq, D_MODEL)

    h = x
    for l in range(DEPTH):
        h = ffn(h, "ffn1", l)
        h = _mixer(h, l, p, wb, consts)
        h = ffn(h, "ffn2", l, p["final_norm"].reshape(1, -1) if l == DEPTH - 1 else None)
    return h


def kernel(x, ffn1_norm, ffn1_wg, ffn1_wu, ffn1_wd, mix_norm, w_in, hy_conv_w, hy_conv_b, hy_w1, hy_b1, hy_w2, hy_b2, hy_w3, hy_b3, hy_freq, hy_wout, hy_skip, rel_bias, rg_conv_w, rg_conv_b, rg_wa, rg_ba, rg_wx, rg_bx, rg_lambda, w_gate, b_gate, w_proj_hy, w_proj_attn, w_proj_rg, w_out, ffn2_norm, ffn2_wg, ffn2_wu, ffn2_wd, final_norm):
    p = dict(
        ffn1_norm=ffn1_norm, ffn1_wg=ffn1_wg, ffn1_wu=ffn1_wu, ffn1_wd=ffn1_wd, mix_norm=mix_norm,
        w_in=w_in, hy_conv_w=hy_conv_w, hy_conv_b=hy_conv_b, hy_w1=hy_w1, hy_b1=hy_b1, hy_w2=hy_w2,
        hy_b2=hy_b2, hy_w3=hy_w3, hy_b3=hy_b3, hy_freq=hy_freq, hy_wout=hy_wout, hy_skip=hy_skip,
        rel_bias=rel_bias, rg_conv_w=rg_conv_w, rg_conv_b=rg_conv_b, rg_wa=rg_wa, rg_ba=rg_ba,
        rg_wx=rg_wx, rg_bx=rg_bx, rg_lambda=rg_lambda, w_gate=w_gate, b_gate=b_gate,
        w_proj_hy=w_proj_hy, w_proj_attn=w_proj_attn, w_proj_rg=w_proj_rg, w_out=w_out,
        ffn2_norm=ffn2_norm, ffn2_wg=ffn2_wg, ffn2_wu=ffn2_wu, ffn2_wd=ffn2_wd,
        final_norm=final_norm)
    return _forward(x, p)
```

```python
import functools
import math

import numpy as np
import jax
import jax.numpy as jnp
from jax import lax
from jax.experimental import pallas as pl
from jax.experimental.pallas import tpu as pltpu

F32 = jnp.float32
BF16 = jnp.bfloat16

D_MODEL = 1024
D_FF = 2816
DEPTH = 4
RMS_EPS = 1e-6

HY_WIDTH = 512
HY_ORDER = 2
HY_EMB = 33
HY_BANDS = 16
HY_HIDDEN = 64
HY_COLS = 3 * HY_WIDTH
HY_DECAY_TARGET = 1e-2
HY_FAST_DECAY = 0.3
HY_SLOW_DECAY = 1.5
HY_MOD_SHIFT = 0.05
HY_NORM_EPS = 1e-6

ATTN_GROUPS = ((128, 1), (512, 4), (2048, 16))
N_GROUPS = 3
HEADS = 8
HEAD_DIM = 64
ATTN_WIDTH = HEADS * HEAD_DIM
QKV_COLS = 3 * N_GROUPS * ATTN_WIDTH
BAND = 64
N_BUCKETS = 32
BUCKET_MAX_EXACT = 8
BUCKET_MAX_DIST = 1024
NEG_INF = -1e30
LOG2E = math.log2(math.e)
Q_SCALE = HEAD_DIM ** -0.5 * LOG2E

RG_WIDTH = 512
RG_BLOCKS = 8
RG_CONV = 4
RG_C = 8.0
RG_COLS = 2 * RG_WIDTH

N_BRANCH = 3

LANES = 128
SUBLANES = 8
VMEM_LIMIT_BYTES = 56 * 1024 * 1024

TOKEN_TILE = 512
Q_TILE = 128
K_WIN = Q_TILE + 2 * BAND
ATTN_UNROLL = 16
N_EDGE_CASES = 4
RG_SCAN_UNROLL = 16
HY_CBLK = 256
HY_P = 512
RG_TT = 64
HALO_T = 8


def _const_spec(shape):
    nd = len(shape)
    return pl.BlockSpec(shape, lambda *_: (0,) * nd, pipeline_mode=pl.Buffered(1))


def _layer_spec(l, rows, cols, col_block=0):
    return pl.BlockSpec((1, rows, cols), lambda *_: (l, 0, col_block), pipeline_mode=pl.Buffered(1))


def _rms(x, g):
    ms = jnp.mean(x * x, axis=-1, keepdims=True)
    return x * lax.rsqrt(ms + RMS_EPS) * g


def _sigmoid(x):
    return 0.5 * jnp.tanh(0.5 * x) + 0.5


def _lane_slabs(x):
    return [x[:, s * LANES:(s + 1) * LANES] for s in range(x.shape[1] // LANES)]


def _ffn_body(h_ref, g_ref, wg_ref, wu_ref, wd_ref, *rest):
    gout_ref, o_ref = rest if len(rest) == 2 else (None, rest[0])
    x = h_ref[...]
    xb = _rms(x, g_ref[...]).astype(BF16)
    gate = jnp.dot(xb, wg_ref[0], preferred_element_type=F32)
    up = jnp.dot(xb, wu_ref[0], preferred_element_type=F32)
    act = (gate * _sigmoid(gate)) * up
    out = x + 0.5 * jnp.dot(act.astype(BF16), wd_ref[0], preferred_element_type=F32)
    o_ref[...] = out if gout_ref is None else _rms(out, gout_ref[...])


def _ffn(h, g, wg, wu, wd, l, g_out=None):
    n_tok = h.shape[0]
    extra = [] if g_out is None else [g_out]
    return pl.pallas_call(
        _ffn_body,
        out_shape=jax.ShapeDtypeStruct(h.shape, F32),
        grid=(n_tok // TOKEN_TILE,),
        in_specs=[
            pl.BlockSpec((TOKEN_TILE, D_MODEL), lambda i: (i, 0)),
            _const_spec((1, D_MODEL)),
            _layer_spec(l, D_MODEL, D_FF),
            _layer_spec(l, D_MODEL, D_FF),
            _layer_spec(l, D_FF, D_MODEL),
        ] + [_const_spec((1, D_MODEL))] * len(extra),
        out_specs=pl.BlockSpec((TOKEN_TILE, D_MODEL), lambda i: (i, 0)),
        compiler_params=pltpu.CompilerParams(
            dimension_semantics=("arbitrary",), vmem_limit_bytes=VMEM_LIMIT_BYTES),
        name="ffn",
    )(h, g, wg, wu, wd, *extra)


def _inproj_body(h_ref, g_ref, why_ref, *rest):
    w_qkv = rest[:3 * N_GROUPS]
    wrg_ref, hy_ref, q0_ref, q1_ref, q2_ref, rg_ref, xs_ref = rest[3 * N_GROUPS:]
    xn = _rms(h_ref[0], g_ref[...])
    xb = xn.astype(BF16)
    hy_ref[0] = jnp.dot(xb, why_ref[0], preferred_element_type=F32).astype(BF16)
    rg_ref[0] = jnp.dot(xb, wrg_ref[0], preferred_element_type=F32).astype(BF16)

    for s, slab in enumerate(_lane_slabs(xn)):
        xs_ref[s] = slab

    def by_residue(d):
        if d == 1:
            return xb
        n = TOKEN_TILE // d
        cols = [jnp.concatenate([xs_ref.at[s][pl.ds(r, n, stride=d), :] for r in range(d)], axis=0)
                for s in range(D_MODEL // LANES)]
        return jnp.concatenate(cols, axis=1).astype(BF16)

    for g, (q_ref, (_, d)) in enumerate(zip((q0_ref, q1_ref, q2_ref), ATTN_GROUPS)):
        lhs = by_residue(d)
        for part in range(3):
            out = jnp.dot(lhs, w_qkv[3 * g + part][0], preferred_element_type=F32)
            if part == 0:
                out = out * Q_SCALE
            q_ref[0, :, :, part * ATTN_WIDTH:(part + 1) * ATTN_WIDTH] = (
                out.astype(BF16).reshape(d, TOKEN_TILE // d, ATTN_WIDTH))


def _inproj(h3, g, w_in, l):
    nb, seq, _ = h3.shape
    dils = [d for _, d in ATTN_GROUPS]
    qshape = lambda d: jax.ShapeDtypeStruct((nb, d, seq // d, 3 * ATTN_WIDTH), BF16)
    qspec = lambda d: pl.BlockSpec((1, d, TOKEN_TILE // d, 3 * ATTN_WIDTH), lambda b, i: (b, 0, i, 0))
    qkv_block0 = HY_COLS // ATTN_WIDTH
    w_qkv_specs = [_layer_spec(l, D_MODEL, ATTN_WIDTH, qkv_block0 + part * N_GROUPS + grp)
                   for grp in range(N_GROUPS) for part in range(3)]
    return pl.pallas_call(
        _inproj_body,
        out_shape=(
            jax.ShapeDtypeStruct((nb, seq, HY_COLS), BF16),
            qshape(dils[0]), qshape(dils[1]), qshape(dils[2]),
            jax.ShapeDtypeStruct((nb, seq, RG_COLS), BF16),
        ),
        grid=(nb, seq // TOKEN_TILE),
        in_specs=[
            pl.BlockSpec((1, TOKEN_TILE, D_MODEL), lambda b, i: (b, i, 0)),
            _const_spec((1, D_MODEL)),
            _layer_spec(l, D_MODEL, HY_COLS, 0),
            *w_qkv_specs,
            _layer_spec(l, D_MODEL, RG_COLS, (HY_COLS + QKV_COLS) // RG_COLS),
        ],
        out_specs=(
            pl.BlockSpec((1, TOKEN_TILE, HY_COLS), lambda b, i: (b, i, 0)),
            qspec(dils[0]), qspec(dils[1]), qspec(dils[2]),
            pl.BlockSpec((1, TOKEN_TILE, RG_COLS), lambda b, i: (b, i, 0)),
        ),
        scratch_shapes=[pltpu.VMEM((D_MODEL // LANES, TOKEN_TILE, LANES), F32)],
        compiler_params=pltpu.CompilerParams(
            dimension_semantics=("arbitrary", "arbitrary"), vmem_limit_bytes=VMEM_LIMIT_BYTES),
        name="inproj",
    )(h3, g, w_in, *([w_in] * (3 * N_GROUPS)), w_in)


def _dft_mats(half):
    n = jnp.arange(half, dtype=jnp.int32)
    kn = (n[:, None] * n[None, :]) % (2 * half)
    ang = kn.astype(F32) * (math.pi / half)
    return jnp.cos(ang).astype(BF16), (-jnp.sin(ang)).astype(BF16)


def _hy_positions(seq):
    t = jnp.linspace(0.0, 1.0, seq, dtype=F32)[:, None]
    tr = jnp.arange(seq, dtype=F32)[:, None]
    wpos = 2.0 * math.pi * tr / seq
    fb = jnp.linspace(1e-4, HY_BANDS - 1, HY_BANDS, dtype=F32)[None, :]
    z = jnp.concatenate([t, jnp.cos(fb * wpos), -jnp.sin(fb * wpos)], axis=-1)
    return jnp.pad(z, ((0, 0), (0, LANES - HY_EMB)))


def _hy_filter_body(z_ref, w1_ref, b1_ref, w2_ref, b2_ref, w3_ref, b3_ref, fr_ref,
                    wf_ref, wb_ref, dl_ref, fre_ref, fim_ref,
                    g_ref, gn_ref, hdn_ref):
    hi = lax.Precision.HIGHEST
    seq = z_ref.shape[0]
    width = wf_ref.shape[-1]

    @pl.when((pl.program_id(0) == 0) & (pl.program_id(1) == 0))
    def _():
        fr = fr_ref[...]
        hdn = jnp.sin(fr * (jnp.dot(z_ref[...], w1_ref[...], precision=hi,
                                    preferred_element_type=F32) + b1_ref[...]))
        hdn = jnp.sin(fr * (jnp.dot(hdn, w2_ref[...], precision=hi,
                                    preferred_element_type=F32) + b2_ref[...]))
        hdn_ref[...] = jnp.sin(fr * (jnp.dot(hdn, w3_ref[...], precision=hi,
                                             preferred_element_type=F32) + b3_ref[...]))

    hdn = hdn_ref[...]
    row = lax.broadcasted_iota(jnp.int32, (seq, width), 0)
    t = row.astype(F32) * (1.0 / (seq - 1))
    decay = jnp.exp(-t * dl_ref[...]) + HY_MOD_SHIFT
    kf = jnp.dot(hdn, wf_ref[0], precision=hi, preferred_element_type=F32) * decay
    kb = jnp.dot(hdn, wb_ref[0], precision=hi, preferred_element_type=F32) * decay
    kb = jnp.where(row == 0, 0.0, kb)
    norm = (jnp.sum(jnp.abs(kf), axis=0, keepdims=True)
            + jnp.sum(jnp.abs(kb), axis=0, keepdims=True) + HY_NORM_EPS)
    kf = kf / norm
    kb = kb / norm

    nseg = seq // HY_P
    krow = lax.broadcasted_iota(jnp.int32, (HY_P, width), 0)
    sk = (1 - 2 * (krow & 1)).astype(F32)
    ck = jnp.where(krow == 0, 0.5 / HY_P, sk * (1.0 / HY_P))
    zero_row = jnp.zeros((1, width), F32)

    def dft(x):
        xb = x.astype(BF16)
        return (jnp.dot(fre_ref[...], xb, preferred_element_type=F32),
                jnp.dot(fim_ref[...], xb, preferred_element_type=F32),
                jnp.sum(x * sk, axis=0, keepdims=True))

    spectra = {}
    for e in range(nseg):
        spectra[e] = dft(kf[e * HY_P:(e + 1) * HY_P])
        lead = kb[e * HY_P:(e + 1) * HY_P]
        bre, bim, bny = dft(jnp.where(krow == 0, 0.0, lead))
        first = kb[(e + 1) * HY_P:(e + 1) * HY_P + 1] if e + 1 < nseg else zero_row
        spectra[-(e + 1)] = (first + sk * bre, -(sk * bim), first + bny)

    for delta in range(-(nseg - 1), nseg):
        are, aim, any_ = spectra[delta - 1]
        bre, bim, bny = spectra[delta]
        gre = (are + sk * bre) * ck
        gim = (aim + sk * bim) * ck
        g_ref[0, delta + nseg - 1, 0] = gre.astype(BF16)
        g_ref[0, delta + nseg - 1, 1] = gim.astype(BF16)
        gn_ref[0, delta + nseg - 1:delta + nseg, :] = (any_ + bny) * (0.5 / HY_P)
    gn_ref[0, 2 * nseg - 1:2 * nseg, :] = zero_row


def _hy_filter(z, w1, b1, w2, b2, w3, b3, freq, wout, deltas, fre, fim):
    seq = z.shape[0]
    wf = wout[:, :HY_ORDER * HY_WIDTH].reshape(HY_HIDDEN, HY_ORDER, HY_WIDTH).transpose(1, 0, 2)
    wb = wout[:, HY_ORDER * HY_WIDTH:].reshape(HY_HIDDEN, HY_ORDER, HY_WIDTH).transpose(1, 0, 2)
    w1p = jnp.pad(w1, ((0, LANES - HY_EMB), (0, 0)))
    row = lambda a: a.reshape(1, -1)
    nseg = seq // HY_P
    return pl.pallas_call(
        _hy_filter_body,
        out_shape=(
            jax.ShapeDtypeStruct((HY_ORDER, 2 * nseg - 1, 2, HY_P, HY_WIDTH), BF16),
            jax.ShapeDtypeStruct((HY_ORDER, 2 * nseg, HY_WIDTH), F32),
        ),
        grid=(HY_ORDER, HY_WIDTH // HY_CBLK),
        in_specs=[
            _const_spec((seq, LANES)),
            _const_spec((LANES, HY_HIDDEN)), _const_spec((1, HY_HIDDEN)),
            _const_spec((HY_HIDDEN, HY_HIDDEN)), _const_spec((1, HY_HIDDEN)),
            _const_spec((HY_HIDDEN, HY_HIDDEN)), _const_spec((1, HY_HIDDEN)),
            _const_spec((1, HY_HIDDEN)),
            pl.BlockSpec((1, HY_HIDDEN, HY_CBLK), lambda o, c: (o, 0, c)),
            pl.BlockSpec((1, HY_HIDDEN, HY_CBLK), lambda o, c: (o, 0, c)),
            pl.BlockSpec((1, HY_CBLK), lambda o, c: (0, c)),
            _const_spec((HY_P, HY_P)), _const_spec((HY_P, HY_P)),
        ],
        out_specs=(
            pl.BlockSpec((1, 2 * nseg - 1, 2, HY_P, HY_CBLK), lambda o, c: (o, 0, 0, 0, c)),
            pl.BlockSpec((1, 2 * nseg, HY_CBLK), lambda o, c: (o, 0, c)),
        ),
        scratch_shapes=[pltpu.VMEM((seq, HY_HIDDEN), F32)],
        compiler_params=pltpu.CompilerParams(
            dimension_semantics=("arbitrary", "arbitrary"), vmem_limit_bytes=VMEM_LIMIT_BYTES),
        name="hy_filter",
    )(z, w1p, row(b1), w2, row(b2), w3, row(b3), row(freq), wf, wb, row(deltas), fre, fim)


def _hy_conv_body(v_ref, x1_ref, x2_ref, wv_ref, w1_ref, w2_ref, bv_ref, b1_ref, b2_ref,
                  fre_ref, fim_ref, g_ref, gn_ref, skip_ref, o_ref,
                  z_ref, g1_ref, g2_ref, ure_ref, uim_ref):
    seq = v_ref.shape[1]
    nseg = seq // HY_P
    sgn = (1 - 2 * (lax.broadcasted_iota(jnp.int32, (HY_P, HY_CBLK), 0) & 1)).astype(F32)

    edge = lax.broadcasted_iota(jnp.int32, (SUBLANES, HY_CBLK), 0)

    def short_conv(x_ref, w_ref, b_ref, dst_ref):
        x = x_ref[0].astype(F32)
        w = w_ref[...]
        dst_ref[...] = (w[0:1] * pltpu.roll(x, 1, 0) + w[1:2] * x
                        + w[2:3] * pltpu.roll(x, seq - 1, 0) + b_ref[...])
        dst_ref[0:SUBLANES, :] = dst_ref[0:SUBLANES, :] - jnp.where(
            edge == 0, w[0:1] * x[seq - 1:seq], 0.0)
        dst_ref[seq - SUBLANES:seq, :] = dst_ref[seq - SUBLANES:seq, :] - jnp.where(
            edge == SUBLANES - 1, w[2:3] * x[0:1], 0.0)

    short_conv(v_ref, wv_ref, bv_ref, z_ref)
    short_conv(x1_ref, w1_ref, b1_ref, g1_ref)
    short_conv(x2_ref, w2_ref, b2_ref, g2_ref)
    skip = skip_ref[...]
    blk = lambda i: slice(i * HY_P, (i + 1) * HY_P)

    for o, gate_ref in enumerate((g1_ref, g2_ref)):
        nyq = []
        for j in range(nseg):
            x = z_ref[blk(j), :]
            xb = x.astype(BF16)
            ure_ref[j] = jnp.dot(fre_ref[...], xb, preferred_element_type=F32).astype(BF16)
            uim_ref[j] = jnp.dot(fim_ref[...], xb, preferred_element_type=F32).astype(BF16)
            nyq.append(jnp.sum(x * sgn, axis=0, keepdims=True))
        for i in range(nseg):
            zre = zim = zny = None
            for j in range(nseg):
                d = i - j + nseg - 1
                gre, gim = g_ref[o, d, 0], g_ref[o, d, 1]
                ure, uim = ure_ref[j], uim_ref[j]
                tre = gre * ure - gim * uim
                tim = gre * uim + gim * ure
                tny = gn_ref[o, d:d + 1, :] * nyq[j]
                zre, zim, zny = (tre, tim, tny) if zre is None else (zre + tre, zim + tim, zny + tny)
            y = (jnp.dot(fre_ref[...], zre, preferred_element_type=F32)
                 + jnp.dot(fim_ref[...], zim, preferred_element_type=F32)
                 + sgn * zny)
            z = gate_ref[blk(i), :] * (y + skip[o:o + 1] * z_ref[blk(i), :])
            if o + 1 < HY_ORDER:
                z_ref[blk(i), :] = z
            else:
                o_ref[0, blk(i), :] = z.astype(BF16)


def _hy_conv(u_hy, conv_w, conv_b, fre, fim, g, gn, skip):
    nb, seq, _ = u_hy.shape
    ncb = HY_WIDTH // HY_CBLK
    nseg = seq // HY_P
    data = lambda part: pl.BlockSpec((1, seq, HY_CBLK), lambda c, b: (b, 0, part * ncb + c))
    wspec = lambda part: pl.BlockSpec((3, HY_CBLK), lambda c, b: (0, part * ncb + c))
    bspec = lambda part: pl.BlockSpec((1, HY_CBLK), lambda c, b: (0, part * ncb + c))
    return pl.pallas_call(
        _hy_conv_body,
        out_shape=jax.ShapeDtypeStruct((nb, seq, HY_WIDTH), BF16),
        grid=(ncb, nb),
        in_specs=[
            data(0), data(1), data(2),
            wspec(0), wspec(1), wspec(2),
            bspec(0), bspec(1), bspec(2),
            _const_spec((HY_P, HY_P)), _const_spec((HY_P, HY_P)),
            pl.BlockSpec((HY_ORDER, 2 * nseg - 1, 2, HY_P, HY_CBLK), lambda c, b: (0, 0, 0, 0, c),
                         pipeline_mode=pl.Buffered(1)),
            pl.BlockSpec((HY_ORDER, 2 * nseg, HY_CBLK), lambda c, b: (0, 0, c)),
            pl.BlockSpec((HY_ORDER, HY_CBLK), lambda c, b: (0, c)),
        ],
        out_specs=pl.BlockSpec((1, seq, HY_CBLK), lambda c, b: (b, 0, c)),
        scratch_shapes=[
            pltpu.VMEM((seq, HY_CBLK), F32), pltpu.VMEM((seq, HY_CBLK), F32),
            pltpu.VMEM((seq, HY_CBLK), F32),
            pltpu.VMEM((nseg, HY_P, HY_CBLK), BF16), pltpu.VMEM((nseg, HY_P, HY_CBLK), BF16),
        ],
        compiler_params=pltpu.CompilerParams(
            dimension_semantics=("arbitrary", "arbitrary"), vmem_limit_bytes=VMEM_LIMIT_BYTES),
        name="hy_conv",
    )(u_hy, u_hy, u_hy, conv_w, conv_w, conv_w, conv_b, conv_b, conv_b,
      fre, fim, g, gn, skip)


def _bucket_matrix(dilation):
    qi = np.arange(Q_TILE, dtype=np.int64)[:, None]
    kj = np.arange(K_WIN, dtype=np.int64)[None, :]
    delta = kj - BAND - qi
    rel = delta * dilation
    half = N_BUCKETS // 2
    n = np.abs(rel)
    nf = np.maximum(n, 1).astype(np.float32)
    large = BUCKET_MAX_EXACT + (
        np.log(nf / np.float32(BUCKET_MAX_EXACT)) / np.float32(math.log(BUCKET_MAX_DIST / BUCKET_MAX_EXACT))
        * np.float32(half - BUCKET_MAX_EXACT)).astype(np.int32)
    large = np.minimum(large, half - 1)
    bucket = np.where(rel > 0, half, 0) + np.where(n < BUCKET_MAX_EXACT, n, large)
    return np.where(np.abs(delta) <= BAND, bucket, -1).astype(np.int32)


def _attn_body(group, tbl_ref, bkt_ref, q_ref, k_ref, v_ref, o_ref, l_ref,
               bias_ref, kpad_ref, vpad_ref):
    _, dil, ls, _ = q_ref.shape
    tiles_per_res = ls // Q_TILE

    kcol = lax.broadcasted_iota(jnp.int32, (Q_TILE, K_WIN), 1)

    @pl.when(pl.program_id(0) == 0)
    def _():
        bkt = bkt_ref[...]
        for h in range(HEADS):
            acc = jnp.full((Q_TILE, K_WIN), NEG_INF, F32)
            for bk in range(N_BUCKETS):
                acc = jnp.where(bkt == bk, tbl_ref[bk, group * HEADS + h] * LOG2E, acc)
            no_head = jnp.where(kcol < BAND, NEG_INF, acc)
            bias_ref[0, h] = acc
            bias_ref[1, h] = no_head
            bias_ref[2, h] = jnp.where(kcol >= Q_TILE + BAND, NEG_INF, acc)
            bias_ref[3, h] = jnp.where(kcol >= Q_TILE + BAND, NEG_INF, no_head)
        zeros = jnp.zeros((dil, BAND, ATTN_WIDTH), BF16)
        kpad_ref[:, 0:BAND] = zeros
        kpad_ref[:, BAND + ls:BAND + ls + BAND] = zeros
        vpad_ref[:, 0:BAND] = zeros
        vpad_ref[:, BAND + ls:BAND + ls + BAND] = zeros

    kpad_ref[:, BAND:BAND + ls] = k_ref[0]
    vpad_ref[:, BAND:BAND + ls] = v_ref[0]

    lane = lax.broadcasted_iota(jnp.int32, (Q_TILE, LANES), 1)
    low_half = lane < HEAD_DIM

    def tile(t, carry):
        r = t // tiles_per_res
        jt = t % tiles_per_res
        j0 = pl.multiple_of(jt * Q_TILE, Q_TILE)
        edge = jnp.where(jt == 0, 1, 0) + jnp.where(jt == tiles_per_res - 1, 2, 0)
        l_ref[0, r, pl.ds(j0, Q_TILE), :] = jnp.zeros((Q_TILE, LANES), F32)
        for hp in range(HEADS // 2):
            cols = slice(hp * LANES, (hp + 1) * LANES)
            q2 = q_ref[0, r, pl.ds(j0, Q_TILE), cols]
            kw = kpad_ref[r, pl.ds(j0, K_WIN), cols]
            vw = vpad_ref[r, pl.ds(j0, K_WIN), cols]
            q_pair = jnp.concatenate(
                [jnp.where(low_half, q2, 0.0), jnp.where(low_half, 0.0, q2)], axis=0).astype(BF16)
            s_pair = lax.dot_general(q_pair, kw, (((1,), (1,)), ((), ())), preferred_element_type=F32)
            probs = []
            for half in range(2):
                h = 2 * hp + half
                s = s_pair[half * Q_TILE:(half + 1) * Q_TILE] + bias_ref[edge, h]
                m = jnp.max(s, axis=-1, keepdims=True)
                p = jnp.exp2(s - m)
                den = jnp.sum(p, axis=-1, keepdims=True)
                probs.append(p.astype(BF16))
                l_ref[0, r, pl.ds(j0, Q_TILE), h:h + 1] = m
                l_ref[0, r, pl.ds(j0, Q_TILE), HEADS + h:HEADS + h + 1] = den
            pv = jnp.dot(jnp.concatenate(probs, axis=0), vw, preferred_element_type=F32)
            o_ref[0, r, pl.ds(j0, Q_TILE), cols] = jnp.where(
                low_half, pv[:Q_TILE], pv[Q_TILE:]).astype(BF16)
        return carry

    lax.fori_loop(0, dil * tiles_per_res, tile, 0, unroll=ATTN_UNROLL)


def _attn_group(qkv_g, tbl, group, dilation):
    nb, _, ls, _ = qkv_g.shape
    part = lambda which: pl.BlockSpec((1, dilation, ls, ATTN_WIDTH), lambda b: (b, 0, 0, which))
    bkt = jnp.asarray(_bucket_matrix(dilation))
    return pl.pallas_call(
        functools.partial(_attn_body, group),
        out_shape=(
            jax.ShapeDtypeStruct((nb, dilation, ls, ATTN_WIDTH), BF16),
            jax.ShapeDtypeStruct((nb, dilation, ls, LANES), F32),
        ),
        grid=(nb,),
        in_specs=[
            pl.BlockSpec(memory_space=pltpu.SMEM),
            _const_spec((Q_TILE, K_WIN)),
            part(0), part(1), part(2),
        ],
        out_specs=(
            pl.BlockSpec((1, dilation, ls, ATTN_WIDTH), lambda b: (b, 0, 0, 0)),
            pl.BlockSpec((1, dilation, ls, LANES), lambda b: (b, 0, 0, 0)),
        ),
        scratch_shapes=[
            pltpu.VMEM((N_EDGE_CASES, HEADS, Q_TILE, K_WIN), F32),
            pltpu.VMEM((dilation, ls + 2 * BAND, ATTN_WIDTH), BF16),
            pltpu.VMEM((dilation, ls + 2 * BAND, ATTN_WIDTH), BF16),
        ],
        compiler_params=pltpu.CompilerParams(
            dimension_semantics=("arbitrary",), vmem_limit_bytes=VMEM_LIMIT_BYTES),
        name=f"attn_g{group}",
    )(tbl, bkt, qkv_g, qkv_g, qkv_g)


def _rglru_body(nt, cw_ref, cb_ref, wf_ref, wb_ref, bf_ref, bb_ref, lam_ref,
                xf_ref, pf_ref, nf_ref, xb_ref, pb_ref, nb_ref,
                hf_ref, hb_ref, xw_ref, a_ref, u_ref, hs_ref, cf_ref, cbk_ref):
    i = pl.program_id(0)
    nbat, tt, width = xf_ref.shape
    n_slab = width // LANES
    n_grp = nbat // SUBLANES
    grp_rows = tt * SUBLANES
    halo_rows = HALO_T * SUBLANES

    @pl.when(i == 0)
    def _():
        cf_ref[...] = jnp.zeros_like(cf_ref)
        cbk_ref[...] = jnp.zeros_like(cbk_ref)

    def to_time_major(src_ref, n_t, row0, keep):
        for b in range(nbat):
            grp, b8 = divmod(b, SUBLANES)
            x = src_ref[b].astype(F32)
            x = x if keep is None else jnp.where(keep, x, 0.0)
            for s, slab in enumerate(_lane_slabs(x)):
                xw_ref.at[s * n_grp + grp][pl.ds(row0 + b8, n_t, stride=SUBLANES), :] = slab

    def gates(x_ref, p_ref, n_ref, tile_idx, w_ref, b_ref, lam):
        to_time_major(p_ref, HALO_T, 0, tile_idx > 0)
        to_time_major(x_ref, tt, halo_rows, None)
        to_time_major(n_ref, HALO_T, halo_rows + grp_rows, tile_idx < nt - 1)
        cw = cw_ref[...]

        def tap(k):
            start = halo_rows + (k - 2) * SUBLANES
            return jnp.concatenate(
                [jnp.concatenate([xw_ref[s * n_grp + grp, start:start + grp_rows, :]
                                  for s in range(n_slab)], axis=1) for grp in range(n_grp)], axis=0)

        xc = cw[0:1] * tap(0) + cw[1:2] * tap(1) + cw[2:3] * tap(2) + cw[3:4] * tap(3) + cb_ref[...]
        g = jnp.dot(xc.astype(BF16), w_ref[...], preferred_element_type=F32) + b_ref[...]
        r2 = 1.0 + jnp.tanh(g[:, :width])
        gi2 = 1.0 + jnp.tanh(g[:, width:])
        softplus = jnp.maximum(-lam, 0.0) + jnp.log(1.0 + jnp.exp(-jnp.abs(lam)))
        log_a = (-0.5 * RG_C * softplus) * r2
        a = jnp.exp(log_a)
        a_ref[...] = a
        one_m_a2 = 1.0 - a * a
        root = jnp.where(one_m_a2 > 0.0, one_m_a2 * lax.rsqrt(one_m_a2), 0.0)
        u_ref[...] = root * (gi2 * xc)

    def step(t, hs):
        r0 = pl.multiple_of(t * SUBLANES, SUBLANES)
        out = []
        for grp, h in enumerate(hs):
            rows = pl.ds(grp * grp_rows + r0, SUBLANES)
            h = a_ref[rows, :] * h + u_ref[rows, :]
            for s, slab in enumerate(_lane_slabs(h)):
                hs_ref[s * n_grp + grp, pl.ds(r0, SUBLANES), :] = slab
            out.append(h)
        return tuple(out)

    def scan(carry_ref, time_of):
        init = tuple(carry_ref[grp * SUBLANES:(grp + 1) * SUBLANES, :] for grp in range(n_grp))
        last = lax.fori_loop(0, tt, lambda k, hs: step(time_of(k), hs), init, unroll=RG_SCAN_UNROLL)
        carry_ref[...] = jnp.concatenate(last, axis=0)

    def to_batch_major(dst_ref):
        for b in range(nbat):
            grp, b8 = divmod(b, SUBLANES)
            dst_ref[b] = jnp.concatenate(
                [hs_ref.at[s * n_grp + grp][pl.ds(b8, tt, stride=SUBLANES), :] for s in range(n_slab)],
                axis=1).astype(BF16)

    gates(xf_ref, pf_ref, nf_ref, i, wf_ref, bf_ref, lam_ref[0:1])
    scan(cf_ref, lambda k: k)
    to_batch_major(hf_ref)

    gates(xb_ref, pb_ref, nb_ref, nt - 1 - i, wb_ref, bb_ref, lam_ref[1:2])
    scan(cbk_ref, lambda k: tt - 1 - k)
    to_batch_major(hb_ref)


def _rglru(rg, conv_w, conv_b, w_f, w_b, b_f, b_b, lam):
    nbat, seq, _ = rg.shape
    nt = seq // RG_TT
    hpt = RG_TT // HALO_T
    nh = seq // HALO_T
    cur = lambda f: pl.BlockSpec((nbat, RG_TT, RG_WIDTH), lambda i: (0, f(i), 0))
    prv = lambda f: pl.BlockSpec(
        (nbat, HALO_T, RG_WIDTH), lambda i: (0, jnp.maximum(f(i) * hpt - 1, 0), 0))
    nxt = lambda f: pl.BlockSpec(
        (nbat, HALO_T, RG_WIDTH), lambda i: (0, jnp.minimum((f(i) + 1) * hpt, nh - 1), 0))
    fw = lambda i: i
    bw = lambda i: nt - 1 - i
    n_slab = RG_WIDTH // LANES
    n_grp = nbat // SUBLANES
    return pl.pallas_call(
        functools.partial(_rglru_body, nt),
        out_shape=(
            jax.ShapeDtypeStruct((nbat, seq, RG_WIDTH), BF16),
            jax.ShapeDtypeStruct((nbat, seq, RG_WIDTH), BF16),
        ),
        grid=(nt,),
        in_specs=[
            _const_spec((RG_CONV, RG_WIDTH)), _const_spec((1, RG_WIDTH)),
            _const_spec((RG_WIDTH, 2 * RG_WIDTH)), _const_spec((RG_WIDTH, 2 * RG_WIDTH)),
            _const_spec((1, 2 * RG_WIDTH)), _const_spec((1, 2 * RG_WIDTH)),
            _const_spec((2, RG_WIDTH)),
            cur(fw), prv(fw), nxt(fw), cur(bw), prv(bw), nxt(bw),
        ],
        out_specs=(cur(fw), cur(bw)),
        scratch_shapes=[
            pltpu.VMEM((n_slab * n_grp, (RG_TT + 2 * HALO_T) * SUBLANES, LANES), F32),
            pltpu.VMEM((RG_TT * nbat, RG_WIDTH), F32),
            pltpu.VMEM((RG_TT * nbat, RG_WIDTH), F32),
            pltpu.VMEM((n_slab * n_grp, RG_TT * SUBLANES, LANES), F32),
            pltpu.VMEM((nbat, RG_WIDTH), F32),
            pltpu.VMEM((nbat, RG_WIDTH), F32),
        ],
        compiler_params=pltpu.CompilerParams(
            dimension_semantics=("arbitrary",), vmem_limit_bytes=VMEM_LIMIT_BYTES),
        name="rglru",
    )(conv_w, conv_b, w_f, w_b, b_f, b_b, lam, rg, rg, rg, rg, rg, rg)


def _block_diag(w):
    eye = jnp.eye(RG_BLOCKS, dtype=w.dtype)
    return jnp.einsum("hij,hk->hikj", w, eye).reshape(RG_WIDTH, RG_WIDTH)


def _merge_body(h_ref, g_ref, wg_ref, bg_ref, ya_ref, o0_ref, o1_ref, o2_ref,
                l0_ref, l1_ref, l2_ref, hf_ref, hb_ref, gate_ref, exp_ref,
                wphy_ref, wpat_ref, wprg_ref, wout_ref, out_ref, os_ref, ls_ref):
    x = h_ref[0]
    xb = _rms(x, g_ref[...]).astype(BF16)
    def gate(branch):
        cols = slice(branch * D_MODEL, (branch + 1) * D_MODEL)
        return 1.0 + jnp.tanh(jnp.dot(xb, wg_ref[0, :, cols], preferred_element_type=F32) + bg_ref[:, cols])

    def token_order(o_ref, l_ref):
        d, n = o_ref.shape[1], o_ref.shape[2]
        if d == 1:
            return o_ref[0, 0].astype(F32), l_ref[0, 0]
        for r in range(d):
            for s, slab in enumerate(_lane_slabs(o_ref[0, r].astype(F32))):
                os_ref.at[s][pl.ds(r, n, stride=d), :] = slab
            ls_ref[pl.ds(r, n, stride=d), :] = l_ref[0, r]
        o = jnp.concatenate([os_ref[s] for s in range(ATTN_WIDTH // LANES)], axis=1)
        return o, ls_ref[...]

    o0, l0 = token_order(o0_ref, l0_ref)
    o1, l1 = token_order(o1_ref, l1_ref)
    o2, l2 = token_order(o2_ref, l2_ref)

    head_lane = lax.broadcasted_iota(jnp.int32, l0.shape, 1) < HEADS
    d0, d1, d2 = (pltpu.roll(l, LANES - HEADS, 1) for l in (l0, l1, l2))
    m = jnp.maximum(jnp.maximum(l0, l1), l2)
    e0, e1, e2 = jnp.exp2(l0 - m), jnp.exp2(l1 - m), jnp.exp2(l2 - m)
    inv = 1.0 / jnp.where(head_lane, e0 * d0 + e1 * d1 + e2 * d2, 1.0)

    def widen(w):
        return jnp.dot(w.astype(BF16), exp_ref[...], preferred_element_type=F32)

    y_b = widen(e0 * inv) * o0 + widen(e1 * inv) * o1 + widen(e2 * inv) * o2

    gt = gate_ref[0].astype(F32)
    gelu = 0.5 * gt * (1.0 + jnp.tanh(math.sqrt(2.0 / math.pi) * (gt + 0.044715 * (gt * gt * gt))))
    y_c = (hf_ref[0].astype(F32) + hb_ref[0].astype(F32)) * gelu

    merged = gate(0) * jnp.dot(ya_ref[0], wphy_ref[0], preferred_element_type=F32)
    merged = merged + gate(1) * jnp.dot(y_b.astype(BF16), wpat_ref[0], preferred_element_type=F32)
    merged = merged + gate(2) * jnp.dot(y_c.astype(BF16), wprg_ref[0], preferred_element_type=F32)
    out_ref[0] = x + jnp.dot(merged.astype(BF16), wout_ref[0], preferred_element_type=F32)


def _merge(h3, g, w_gate, b_gate, y_a, attn, hf, hb, rg, expand, wp_hy, wp_attn, wp_rg, w_out, l):
    nb, seq, _ = h3.shape
    tok = lambda width, col=0: pl.BlockSpec((1, TOKEN_TILE, width), lambda b, i: (b, i, col))
    res = lambda d, width: pl.BlockSpec((1, d, TOKEN_TILE // d, width), lambda b, i: (b, 0, i, 0))
    (o0, l0), (o1, l1), (o2, l2) = attn
    dils = [d for _, d in ATTN_GROUPS]
    return pl.pallas_call(
        _merge_body,
        out_shape=jax.ShapeDtypeStruct(h3.shape, F32),
        grid=(nb, seq // TOKEN_TILE),
        in_specs=[
            tok(D_MODEL), _const_spec((1, D_MODEL)),
            _layer_spec(l, D_MODEL, N_BRANCH * D_MODEL), _const_spec((1, N_BRANCH * D_MODEL)),
            tok(HY_WIDTH),
            res(dils[0], ATTN_WIDTH), res(dils[1], ATTN_WIDTH), res(dils[2], ATTN_WIDTH),
            res(dils[0], LANES), res(dils[1], LANES), res(dils[2], LANES),
            tok(RG_WIDTH), tok(RG_WIDTH), tok(RG_WIDTH, 1),
            _const_spec((LANES, ATTN_WIDTH)),
            _layer_spec(l, HY_WIDTH, D_MODEL), _layer_spec(l, ATTN_WIDTH, D_MODEL),
            _layer_spec(l, RG_WIDTH, D_MODEL), _layer_spec(l, D_MODEL, D_MODEL),
        ],
        out_specs=tok(D_MODEL),
        scratch_shapes=[
            pltpu.VMEM((ATTN_WIDTH // LANES, TOKEN_TILE, LANES), F32),
            pltpu.VMEM((TOKEN_TILE, LANES), F32),
        ],
        compiler_params=pltpu.CompilerParams(
            dimension_semantics=("arbitrary", "arbitrary"), vmem_limit_bytes=VMEM_LIMIT_BYTES),
        name="merge",
    )(h3, g, w_gate, b_gate, y_a, o0, o1, o2, l0, l1, l2, hf, hb, rg, expand,
      wp_hy, wp_attn, wp_rg, w_out)


def _mixer(h3, l, p, wb, consts):
    fre, fim, zpos, deltas, expand = consts
    hy, q0, q1, q2, rg = _inproj(h3, p["mix_norm"][l].reshape(1, -1), wb["w_in"], l)

    g_spec, g_nyq = _hy_filter(
        zpos, p["hy_w1"][l], p["hy_b1"][l], p["hy_w2"][l], p["hy_b2"][l], p["hy_w3"][l],
        p["hy_b3"][l], p["hy_freq"][l], p["hy_wout"][l], deltas, fre, fim)
    y_a = _hy_conv(hy, p["hy_conv_w"][l], p["hy_conv_b"][l].reshape(1, -1),
                   fre, fim, g_spec, g_nyq, p["hy_skip"][l])

    attn = [_attn_group(q, p["rel_bias"], g, dil)
            for g, (q, (_, dil)) in enumerate(zip((q0, q1, q2), ATTN_GROUPS))]

    rg_w = lambda d: (0.5 * jnp.concatenate(
        [_block_diag(p["rg_wa"][l, d]), _block_diag(p["rg_wx"][l, d])], axis=1)).astype(BF16)
    rg_b = lambda d: 0.5 * jnp.concatenate([p["rg_ba"][l, d], p["rg_bx"][l, d]]).reshape(1, -1)
    hf, hb = _rglru(rg, p["rg_conv_w"][l], p["rg_conv_b"][l].reshape(1, -1),
                    rg_w(0), rg_w(1), rg_b(0), rg_b(1), p["rg_lambda"][l])

    return _merge(
        h3, p["mix_norm"][l].reshape(1, -1), wb["w_gate"], 0.5 * p["b_gate"][l].reshape(1, -1),
        y_a, attn, hf, hb, rg, expand,
        wb["w_proj_hy"], wb["w_proj_attn"], wb["w_proj_rg"], wb["w_out"], l)


def _forward(x, p):
    nb, seq, _ = x.shape
    fre, fim = _dft_mats(HY_P)
    deltas = jnp.abs(jnp.linspace(math.log(HY_DECAY_TARGET) / HY_FAST_DECAY,
                                  math.log(HY_DECAY_TARGET) / HY_SLOW_DECAY, HY_WIDTH, dtype=F32))
    head_of_lane = jnp.arange(ATTN_WIDTH, dtype=jnp.int32) // HEAD_DIM
    expand = (jnp.arange(LANES, dtype=jnp.int32)[:, None] == head_of_lane[None, :]).astype(BF16)
    consts = (fre, fim, _hy_positions(seq), deltas, expand)

    halved = ("w_gate", "w_out", "w_proj_rg")
    wb = {k: (0.5 * p[k] if k in halved else p[k]).astype(BF16) for k in (
        "ffn1_wg", "ffn1_wu", "ffn1_wd", "ffn2_wg", "ffn2_wu", "ffn2_wd",
        "w_in", "w_gate", "w_proj_hy", "w_proj_attn", "w_proj_rg", "w_out")}

    def ffn(h3, which, l, g_out=None):
        out = _ffn(h3.reshape(nb * seq, D_MODEL), p[which + "_norm"][l].reshape(1, -1),
                   wb[which + "_wg"], wb[which + "_wu"], wb[which + "_wd"], l, g_out)
        return out.reshape(nb, seq, D_MODEL)

    h = x
    for l in range(DEPTH):
        h = ffn(h, "ffn1", l)
        h = _mixer(h, l, p, wb, consts)
        h = ffn(h, "ffn2", l, p["final_norm"].reshape(1, -1) if l == DEPTH - 1 else None)
    return h


def kernel(x, ffn1_norm, ffn1_wg, ffn1_wu, ffn1_wd, mix_norm, w_in, hy_conv_w, hy_conv_b, hy_w1, hy_b1, hy_w2, hy_b2, hy_w3, hy_b3, hy_freq, hy_wout, hy_skip, rel_bias, rg_conv_w, rg_conv_b, rg_wa, rg_ba, rg_wx, rg_bx, rg_lambda, w_gate, b_gate, w_proj_hy, w_proj_attn, w_proj_rg, w_out, ffn2_norm, ffn2_wg, ffn2_wu, ffn2_wd, final_norm):
    p = dict(
        ffn1_norm=ffn1_norm, ffn1_wg=ffn1_wg, ffn1_wu=ffn1_wu, ffn1_wd=ffn1_wd, mix_norm=mix_norm,
        w_in=w_in, hy_conv_w=hy_conv_w, hy_conv_b=hy_conv_b, hy_w1=hy_w1, hy_b1=hy_b1, hy_w2=hy_w2,
        hy_b2=hy_b2, hy_w3=hy_w3, hy_b3=hy_b3, hy_freq=hy_freq, hy_wout=hy_wout, hy_skip=hy_skip,
        rel_bias=rel_bias, rg_conv_w=rg_conv_w, rg_conv_b=rg_conv_b, rg_wa=rg_wa, rg_ba=rg_ba,
        rg_wx=rg_wx, rg_bx=rg_bx, rg_lambda=rg_lambda, w_gate=w_gate, b_gate=b_gate,
        w_proj_hy=w_proj_hy, w_proj_attn=w_proj_attn, w_proj_rg=w_proj_rg, w_out=w_out,
        ffn2_norm=ffn2_norm, ffn2_wg=ffn2_wg, ffn2_wu=ffn2_wu, ffn2_wd=ffn2_wd,
        final_norm=final_norm)
    return _forward(x, p)
```
